```python
import jax, jax.numpy as jnp
from jax import lax
import numpy as np

D_MODEL = 1024
BATCH = 8
SEQ = 4096
DEPTH = 1

A_HEADS = 4
A_DK = 128
A_DV = 128
A_CHUNK = 64
A_QK_W = A_HEADS * A_DK
A_V_W = A_HEADS * A_DV
B_HEADS = 8
B_KV_GROUPS = 2
B_HPG = B_HEADS // B_KV_GROUPS
B_HEAD_DIM = 64
B_Q_W = B_HEADS * B_HEAD_DIM
B_KV_W = B_KV_GROUPS * B_HEAD_DIM
CMP_LEN = 32
CMP_STRIDE = 16
CMP_HIDDEN = 256
SEL_LEN = 64
SEL_TOPK = 16
WINDOW = 512
Q_BLOCK = 128
FORCE_SCORE = 1e4
NEG = -1e30
ROPE_THETA = 500000.0
ROT_DIM = B_HEAD_DIM // 4
D_FF = 4 * D_MODEL
EPS = 1e-6

IN_SIZES = [A_QK_W, A_QK_W, A_V_W, A_V_W,
            B_Q_W, 6 * B_KV_W, 3 * B_HEADS,
            2 * D_MODEL]
N_IN = int(sum(IN_SIZES))
SPLIT_POINTS = [int(v) for v in np.cumsum(IN_SIZES)[:-1]]

kernel_name = "hgrn2_nsa_gated_hybrid"


def rmsnorm(x, g):
    xf = x.astype(jnp.float32)
    y = xf * lax.rsqrt(jnp.mean(xf * xf, axis=-1, keepdims=True) + EPS) * g.astype(jnp.float32)
    return y.astype(x.dtype)


def masked_softmax(s, mask):
    s = jnp.where(mask, s.astype(jnp.float32), NEG)
    m = jnp.max(s, axis=-1, keepdims=True)
    p = jnp.where(mask, jnp.exp(s - m), 0.0)
    return p / jnp.maximum(jnp.sum(p, axis=-1, keepdims=True), 1e-30)


def partial_rope(t, positions):
    half = ROT_DIM // 2
    inv_freq = ROPE_THETA ** (-jnp.arange(0, ROT_DIM, 2, dtype=jnp.float32) / ROT_DIM)
    ang = positions.astype(jnp.float32)[..., None] * inv_freq
    cos = jnp.cos(ang)[:, :, None, :].astype(t.dtype)
    sin = jnp.sin(ang)[:, :, None, :].astype(t.dtype)
    t1, t2, rest = t[..., :half], t[..., half:ROT_DIM], t[..., ROT_DIM:]
    return jnp.concatenate([t1 * cos - t2 * sin, t2 * cos + t1 * sin, rest], axis=-1)


def hgrn2_mixer(q, f_pre, v, g, lb, norm_g):
    bsz, s, _ = q.shape
    n = s // A_CHUNK
    f32 = jnp.float32
    f = lb + (1.0 - lb) * jax.nn.sigmoid(f_pre.astype(f32))
    k = 1.0 - f

    def heads(t, d):
        return t.astype(f32).reshape(bsz, n, A_CHUNK, A_HEADS, d).transpose(0, 3, 1, 2, 4)

    qh, kh, vh, lfh = heads(q, A_DK), heads(k, A_DK), heads(v, A_DV), heads(jnp.log(f), A_DK)
    b = jnp.cumsum(lfh, axis=3)
    b_last = b[..., -1:, :]
    q_dec = qh * jnp.exp(b)
    k_dec = kh * jnp.exp(-b)
    causal = jnp.tril(jnp.ones((A_CHUNK, A_CHUNK), dtype=bool))
    attn = jnp.where(causal, jnp.einsum('bhnsk,bhntk->bhnst', q_dec, k_dec), 0.0)
    o_intra = jnp.einsum('bhnst,bhntv->bhnsv', attn, vh)
    kv = jnp.einsum('bhntk,bhntv->nbhkv', kh * jnp.exp(b_last - b), vh)
    decay = jnp.exp(b_last[..., 0, :]).transpose(2, 0, 1, 3)

    def step(state, inp):
        dec, kv_n = inp
        return dec[..., None] * state + kv_n, state

    s0 = jnp.zeros((bsz, A_HEADS, A_DK, A_DV), f32)
    _, s_start = lax.scan(step, s0, (decay, kv))
    o_inter = jnp.einsum('bhnsk,nbhkv->bhnsv', q_dec, s_start)
    o = (o_intra + o_inter).transpose(0, 2, 3, 1, 4).reshape(bsz, s, A_HEADS, A_DV)
    o = rmsnorm(o, norm_g) * jax.nn.silu(g.astype(f32).reshape(bsz, s, A_HEADS, A_DV))
    return o.reshape(bsz, s, A_V_W).astype(q.dtype)


def nsa_mixer(q, kc_tok, vc_tok, ks, vs, kw, vw, gate_pre, positions,
              pe_k, pe_v, w1_k, w2_k, w1_v, w2_v):
    bsz, s, _ = q.shape
    G, HPG, HD = B_KV_GROUPS, B_HPG, B_HEAD_DIM

    def kvh(t):
        return t.reshape(bsz, s, G, HD)

    q = partial_rope(q.reshape(bsz, s, B_HEADS, HD), positions)
    kc_tok = partial_rope(kvh(kc_tok), positions)
    vc_tok = kvh(vc_tok)
    ks = partial_rope(kvh(ks), positions)
    kw = partial_rope(kvh(kw), positions)

    n_cmp = (s - CMP_LEN) // CMP_STRIDE + 1
    blk_idx = np.arange(n_cmp)[:, None] * CMP_STRIDE + np.arange(CMP_LEN)[None, :]

    def compress(tok, pe, w1, w2):
        blocks = tok[:, blk_idx] + pe[None, None, :, None, :]
        flat = blocks.transpose(0, 1, 3, 2, 4).reshape(bsz, n_cmp, G, CMP_LEN * HD)
        return (jax.nn.gelu(flat @ w1) @ w2).transpose(0, 2, 1, 3)

    kc = compress(kc_tok, pe_k, w1_k, w2_k)
    vc = compress(vc_tok, pe_v, w1_v, w2_v)
    cmp_end = jnp.asarray(blk_idx[:, -1], jnp.int32)

    n_slc = s // SEL_LEN
    cs = np.arange(n_cmp)[:, None] * CMP_STRIDE
    ss = np.arange(n_slc)[None, :] * SEL_LEN
    overlap = np.clip(np.minimum(cs + CMP_LEN, ss + SEL_LEN) - np.maximum(cs, ss), 0, None) / CMP_LEN
    overlap = jnp.asarray(overlap, jnp.float32)
    n_top = min(SEL_TOPK, n_slc)

    qg = q.reshape(bsz, s, G, HPG, HD).transpose(0, 2, 3, 1, 4) * (HD ** -0.5)
    ks_g = ks.transpose(0, 2, 1, 3)
    vs_g = kvh(vs).transpose(0, 2, 1, 3)
    pad = ((0, 0), (0, 0), (WINDOW, 0), (0, 0))
    kw_pad = jnp.pad(kw.transpose(0, 2, 1, 3), pad)
    vw_pad = jnp.pad(kvh(vw).transpose(0, 2, 1, 3), pad)
    gates = jax.nn.sigmoid(gate_pre.reshape(bsz, s, G, HPG, 3)).transpose(0, 2, 3, 1, 4)
    bi = jnp.arange(bsz)[:, None, None]
    gi = jnp.arange(G)[None, :, None]
    sel_off = jnp.arange(SEL_LEN)
    jblk = jnp.arange(n_slc)

    def block_fn(blk):
        q0 = blk * Q_BLOCK
        qb = lax.dynamic_slice_in_dim(qg, q0, Q_BLOCK, axis=3)
        gb = lax.dynamic_slice_in_dim(gates, q0, Q_BLOCK, axis=3)
        t = q0 + jnp.arange(Q_BLOCK)
        p_c = masked_softmax(jnp.einsum('bghqd,bgcd->bghqc', qb, kc),
                             cmp_end[None, :] <= t[:, None])
        o_c = jnp.einsum('bghqc,bgcd->bghqd', p_c.astype(vc.dtype), vc)
        imp = jnp.einsum('bghqc,cj->bgqj', p_c, overlap)
        cur = t // SEL_LEN
        causal_blk = jblk[None, :] <= cur[:, None]
        forced = (jblk[None, :] == 0) | (jblk[None, :] == cur[:, None]) | (jblk[None, :] == cur[:, None] - 1)
        imp = jnp.where(forced & causal_blk, FORCE_SCORE, imp)
        imp = jnp.where(causal_blk, imp, NEG)
        top_val, top_idx = lax.top_k(imp, n_top)
        tok_idx = top_idx[..., None] * SEL_LEN + sel_off
        tok_ok = (top_val > 0.5 * NEG)[..., None] & (tok_idx <= t[None, None, :, None, None])
        flat_idx = tok_idx.reshape(bsz, G, Q_BLOCK * n_top * SEL_LEN)
        k_sel = ks_g[bi, gi, flat_idx].reshape(bsz, G, Q_BLOCK, n_top * SEL_LEN, HD)
        v_sel = vs_g[bi, gi, flat_idx].reshape(bsz, G, Q_BLOCK, n_top * SEL_LEN, HD)
        p_s = masked_softmax(jnp.einsum('bghqd,bgqkd->bghqk', qb, k_sel),
                             tok_ok.reshape(bsz, G, 1, Q_BLOCK, n_top * SEL_LEN))
        o_s = jnp.einsum('bghqk,bgqkd->bghqd', p_s.astype(v_sel.dtype), v_sel)
        k_w = lax.dynamic_slice_in_dim(kw_pad, q0, WINDOW + Q_BLOCK, axis=2)
        v_w = lax.dynamic_slice_in_dim(vw_pad, q0, WINDOW + Q_BLOCK, axis=2)
        s_pos = q0 - WINDOW + jnp.arange(WINDOW + Q_BLOCK)
        m_w = (s_pos[None, :] <= t[:, None]) & (s_pos[None, :] > t[:, None] - WINDOW) & (s_pos[None, :] >= 0)
        p_w = masked_softmax(jnp.einsum('bghqd,bgkd->bghqk', qb, k_w), m_w)
        o_w = jnp.einsum('bghqk,bgkd->bghqd', p_w.astype(v_w.dtype), v_w)
        return gb[..., 0:1] * o_c + gb[..., 1:2] * o_s + gb[..., 2:3] * o_w

    out = lax.map(block_fn, jnp.arange(s // Q_BLOCK))
    return out.transpose(1, 0, 4, 2, 3, 5).reshape(bsz, s, B_Q_W)


def setup_inputs(seed: int = 0) -> dict:
    key = jax.random.key(seed)
    ks = jax.random.split(key, 20)
    f32 = jnp.float32

    def nrm(k, shape, scale):
        return jax.random.normal(k, shape, f32) * scale

    L = DEPTH
    x = jax.random.normal(ks[0], (BATCH, SEQ, D_MODEL), f32)
    offsets = jax.random.randint(ks[1], (BATCH, 1), 0, 2048, dtype=jnp.int32)
    positions = (offsets + jnp.arange(SEQ, dtype=jnp.int32)[None, :]).astype(jnp.int32)
    return {
        "x": x,
        "positions": positions,
        "norm1_g": 1.0 + nrm(ks[2], (L, D_MODEL), 0.02),
        "w_in": nrm(ks[3], (L, D_MODEL, N_IN), D_MODEL ** -0.5),
        "lb_param": nrm(ks[4], (L + 1, A_QK_W), 0.1),
        "hgrn_norm_g": 1.0 + nrm(ks[5], (L, A_DV), 0.02),
        "cmp_pe_k": nrm(ks[6], (L, CMP_LEN, B_HEAD_DIM), 0.1),
        "cmp_pe_v": nrm(ks[7], (L, CMP_LEN, B_HEAD_DIM), 0.1),
        "cmp_w1_k": nrm(ks[8], (L, CMP_LEN * B_HEAD_DIM, CMP_HIDDEN), (CMP_LEN * B_HEAD_DIM) ** -0.5),
        "cmp_w2_k": nrm(ks[9], (L, CMP_HIDDEN, B_HEAD_DIM), CMP_HIDDEN ** -0.5),
        "cmp_w1_v": nrm(ks[10], (L, CMP_LEN * B_HEAD_DIM, CMP_HIDDEN), (CMP_LEN * B_HEAD_DIM) ** -0.5),
        "cmp_w2_v": nrm(ks[11], (L, CMP_HIDDEN, B_HEAD_DIM), CMP_HIDDEN ** -0.5),
        "w_br_a": nrm(ks[12], (L, A_V_W, D_MODEL), A_V_W ** -0.5),
        "w_br_b": nrm(ks[13], (L, B_Q_W, D_MODEL), B_Q_W ** -0.5),
        "w_out": nrm(ks[14], (L, D_MODEL, D_MODEL), D_MODEL ** -0.5),
        "norm2_g": 1.0 + nrm(ks[15], (L, D_MODEL), 0.02),
        "w_ff1": nrm(ks[16], (L, D_MODEL, D_FF), D_MODEL ** -0.5),
        "w_ff2": nrm(ks[17], (L, D_FF, D_MODEL), D_FF ** -0.5),
        "final_g": 1.0 + nrm(ks[18], (D_MODEL,), 0.02),
    }


def reference(x, positions, norm1_g, w_in, lb_param, hgrn_norm_g, cmp_pe_k, cmp_pe_v,
              cmp_w1_k, cmp_w2_k, cmp_w1_v, cmp_w2_v, w_br_a, w_br_b, w_out,
              norm2_g, w_ff1, w_ff2, final_g):
    lb_all = jnp.cumsum(jax.nn.softmax(lb_param.astype(jnp.float32), axis=0), axis=0)
    h = x
    for l in range(DEPTH):
        xn = rmsnorm(h, norm1_g[l])
        proj = xn @ w_in[l]
        a_q, a_f, a_i, a_g, b_q, b_kv, b_gate, m_gate = jnp.split(proj, SPLIT_POINTS, axis=-1)
        kc, vc, ks_, vs_, kw_, vw_ = jnp.split(b_kv, 6, axis=-1)
        y_a = hgrn2_mixer(a_q, a_f, a_i, a_g, lb_all[l], hgrn_norm_g[l])
        y_b = nsa_mixer(b_q, kc, vc, ks_, vs_, kw_, vw_, b_gate, positions,
                        cmp_pe_k[l], cmp_pe_v[l], cmp_w1_k[l], cmp_w2_k[l], cmp_w1_v[l], cmp_w2_v[l])
        g_a, g_b = jnp.split(jax.nn.sigmoid(m_gate), 2, axis=-1)
        merged = g_a * (y_a @ w_br_a[l]) + g_b * (y_b @ w_br_b[l])
        h = h + merged @ w_out[l]
        hn = rmsnorm(h, norm2_g[l])
        h = h + jnp.square(jax.nn.relu(hn @ w_ff1[l])) @ w_ff2[l]
    return rmsnorm(h, final_g)
```

```python
import functools

import jax
import jax.numpy as jnp
import numpy as np
from jax import lax
from jax.experimental import pallas as pl
from jax.experimental.pallas import tpu as pltpu

F32 = jnp.float32
BF16 = jnp.bfloat16

D_MODEL = 1024
A_HEADS = 4
A_DK = 128
A_DV = 128
A_CHUNK = 64
A_W = A_HEADS * A_DK
B_HEADS = 8
B_GROUPS = 2
B_HPG = B_HEADS // B_GROUPS
B_HD = 64
B_Q_W = B_HEADS * B_HD
B_KV_W = B_GROUPS * B_HD
CMP_LEN = 32
CMP_STRIDE = 16
CMP_HIDDEN = 256
SEL_LEN = 64
SEL_TOPK = 16
WINDOW = 512
Q_BLOCK = 128
FORCE_SCORE = 1e4
NEG = -1e30
ROPE_THETA = 500000.0
ROT_DIM = B_HD // 4
ROT_HALF = ROT_DIM // 2
D_FF = 4 * D_MODEL
EPS = 1e-6
GATE_ROWS = 16

VMEM_LIMIT = 56 * 1024 * 1024

NT = (((1,), (1,)), ((), ()))
TN = (((0,), (0,)), ((), ()))


def _dot(a, b):
    return jnp.dot(a, b, preferred_element_type=F32)


def _dot_nt(a, b):
    return lax.dot_general(a, b, NT, preferred_element_type=F32)


def _dot_tn(a, b):
    return lax.dot_general(a, b, TN, preferred_element_type=F32)


def _rms(x, g):
    return x * lax.rsqrt(jnp.mean(x * x, axis=-1, keepdims=True) + EPS) * g


def _proj_kernel(x_ref, pos_ref, g1_ref, invf_ref, wnat_ref, wt_ref,
                 hg_ref, mg_ref, vc_ref, kc_ref, ks_ref, kw_ref, qt_ref, vt_ref, gt_ref):
    xn = _rms(x_ref[0], g1_ref[...]).astype(BF16)

    def nat(c0, c1):
        return _dot(xn, wnat_ref[:, c0:c1])

    half = 2 * A_W
    for c in range(2):
        hg_ref[0, :, c * half:(c + 1) * half] = nat(c * half, (c + 1) * half).astype(BF16)
        mg_ref[0, :, c * half:(c + 1) * half] = nat(4 * A_W + c * half, 4 * A_W + (c + 1) * half).astype(BF16)
    vc_ref[0] = nat(4 * A_W + 2 * D_MODEL, 4 * A_W + 2 * D_MODEL + B_KV_W)

    def tr(r0, r1):
        return _dot_nt(wt_ref[r0:r1, :], xn)

    ang = invf_ref[...] * pos_ref[0].astype(F32)
    cos = jnp.cos(ang)
    sin = jnp.sin(ang)

    def rope(xt, n_heads):
        pieces = []
        for h in range(n_heads):
            b0 = h * B_HD
            t1 = xt[b0:b0 + ROT_HALF]
            t2 = xt[b0 + ROT_HALF:b0 + ROT_DIM]
            pieces += [t1 * cos - t2 * sin, t2 * cos + t1 * sin, xt[b0 + ROT_DIM:b0 + B_HD]]
        return jnp.concatenate(pieces, axis=0)

    r_q, r_k, r_v = B_Q_W, B_Q_W + 3 * B_KV_W, B_Q_W + 5 * B_KV_W
    qt_ref[0] = (rope(tr(0, r_q), B_HEADS) * (B_HD ** -0.5)).astype(BF16)
    k3 = rope(tr(r_q, r_k), 3 * B_GROUPS).T
    kc_ref[0] = k3[:, 0:B_KV_W]
    ks_ref[0] = k3[:, B_KV_W:2 * B_KV_W].astype(BF16)
    kw_ref[0] = k3[:, 2 * B_KV_W:3 * B_KV_W].astype(BF16)
    vt_ref[0] = tr(r_k, r_v).astype(BF16)
    gt_ref[0] = jax.nn.sigmoid(tr(r_v, r_v + B_GROUPS * GATE_ROWS))


def _proj(x, pos3, g1, invf, w_nat, w_t):
    bsz, s, _ = x.shape
    tm = min(512, s)
    n_nat = w_nat.shape[1]
    n_t = w_t.shape[0]
    tok = lambda w: pl.BlockSpec((1, tm, w), lambda b, t: (b, t, 0))
    trn = lambda r: pl.BlockSpec((1, r, tm), lambda b, t: (b, 0, t))
    const = lambda shp: pl.BlockSpec(shp, lambda b, t: (0,) * len(shp), pipeline_mode=pl.Buffered(1))
    out_shape = (
        jax.ShapeDtypeStruct((bsz, s, 4 * A_W), BF16),
        jax.ShapeDtypeStruct((bsz, s, 2 * D_MODEL), BF16),
        jax.ShapeDtypeStruct((bsz, s, B_KV_W), F32),
        jax.ShapeDtypeStruct((bsz, s, B_KV_W), F32),
        jax.ShapeDtypeStruct((bsz, s, B_KV_W), BF16),
        jax.ShapeDtypeStruct((bsz, s, B_KV_W), BF16),
        jax.ShapeDtypeStruct((bsz, B_Q_W, s), BF16),
        jax.ShapeDtypeStruct((bsz, 2 * B_KV_W, s), BF16),
        jax.ShapeDtypeStruct((bsz, B_GROUPS * GATE_ROWS, s), F32),
    )
    return pl.pallas_call(
        _proj_kernel,
        grid=(bsz, s // tm),
        in_specs=[tok(D_MODEL), trn(1), const((1, D_MODEL)), const((ROT_HALF, 1)),
                  const((D_MODEL, n_nat)), const((n_t, D_MODEL))],
        out_specs=(tok(4 * A_W), tok(2 * D_MODEL), tok(B_KV_W), tok(B_KV_W), tok(B_KV_W), tok(B_KV_W),
                   trn(B_Q_W), trn(2 * B_KV_W), trn(B_GROUPS * GATE_ROWS)),
        out_shape=out_shape,
        compiler_params=pltpu.CompilerParams(
            dimension_semantics=("parallel", "parallel"), vmem_limit_bytes=VMEM_LIMIT),
        name="proj",
    )(x, pos3, g1, invf, w_nat, w_t)


def _hgrn_kernel(hg_ref, lbp_ref, ng_ref, o_ref, st_ref, *, n_chunks):
    @pl.when(pl.program_id(1) == 0)
    def _():
        st_ref[...] = jnp.zeros_like(st_ref)

    lbp = lbp_ref[...]
    e = jnp.exp(lbp - jnp.max(lbp, axis=0, keepdims=True))
    lb = e[0:1] / jnp.sum(e, axis=0, keepdims=True)
    ng = ng_ref[...]
    row = lax.broadcasted_iota(jnp.int32, (A_CHUNK, A_CHUNK), 0)
    col = lax.broadcasted_iota(jnp.int32, (A_CHUNK, A_CHUNK), 1)
    causal = row >= col
    tri = causal.astype(F32)

    def chunk(c, carry):
        r0 = pl.multiple_of(c * A_CHUNK, A_CHUNK)
        blk = hg_ref[0, pl.ds(r0, A_CHUNK), :]
        q = blk[:, 0:A_W].astype(F32)
        fp = blk[:, A_W:2 * A_W].astype(F32)
        v = blk[:, 2 * A_W:3 * A_W]
        g = blk[:, 3 * A_W:4 * A_W].astype(F32)
        f = lb + (1.0 - lb) * jax.nn.sigmoid(fp)
        k = 1.0 - f
        b = jnp.dot(tri, jnp.log(f), precision=lax.Precision.HIGHEST,
                    preferred_element_type=F32)
        b_last = b[A_CHUNK - 1:A_CHUNK, :]
        q_dec = (q * jnp.exp(b)).astype(BF16)
        k_dec = (k * jnp.exp(-b)).astype(BF16)
        k_til = (k * jnp.exp(b_last - b)).astype(BF16)
        dec = jnp.exp(b_last)
        for h in range(A_HEADS):
            sl = slice(h * A_DK, (h + 1) * A_DK)
            attn = jnp.where(causal, _dot_nt(q_dec[:, sl], k_dec[:, sl]), 0.0).astype(BF16)
            st_t = st_ref[h]
            o = _dot(attn, v[:, sl]) + _dot_nt(q_dec[:, sl], st_t.astype(BF16))
            st_ref[h] = st_t * dec[:, sl] + _dot_tn(v[:, sl], k_til[:, sl])
            gh = g[:, sl]
            y = _rms(o, ng) * (gh * jax.nn.sigmoid(gh))
            o_ref[0, pl.ds(r0, A_CHUNK), sl] = y.astype(BF16)
        return carry

    lax.fori_loop(0, n_chunks, chunk, 0)


def _hgrn(hg, lb_param, norm_g):
    bsz, s, _ = hg.shape
    tt = min(512, s)
    return pl.pallas_call(
        functools.partial(_hgrn_kernel, n_chunks=tt // A_CHUNK),
        grid=(bsz, s // tt),
        in_specs=[pl.BlockSpec((1, tt, 4 * A_W), lambda b, t: (b, t, 0)),
                  pl.BlockSpec(lb_param.shape, lambda b, t: (0, 0)),
                  pl.BlockSpec((1, A_DV), lambda b, t: (0, 0))],
        out_specs=pl.BlockSpec((1, tt, A_W), lambda b, t: (b, t, 0)),
        out_shape=jax.ShapeDtypeStruct((bsz, s, A_W), BF16),
        scratch_shapes=[pltpu.VMEM((A_HEADS, A_DV, A_DK), F32)],
        compiler_params=pltpu.CompilerParams(
            dimension_semantics=("parallel", "arbitrary"), vmem_limit_bytes=VMEM_LIMIT),
        name="hgrn",
    )(hg, lb_param, norm_g)


def _compress_kernel(kt_ref, vt_ref, pek_ref, pev_ref, w1k_ref, w1v_ref, w2k_ref, w2vt_ref,
                     kc_ref, vct_ref, *, n_blk):
    half = CMP_LEN // 2

    def hidden(t_ref, pe_ref, w1_ref, g):
        acc = [jnp.zeros((n_blk, CMP_HIDDEN), F32), jnp.zeros((n_blk, CMP_HIDDEN), F32)]
        for l in range(CMP_LEN):
            rows = t_ref[0, pl.ds(l % half, n_blk, stride=CMP_STRIDE), :] + pe_ref[g, l:l + 1, :]
            acc[l // half] = acc[l // half] + _dot(rows.astype(BF16), w1_ref[g, l])
        pre = acc[0] + pltpu.roll(acc[1], n_blk - 1, 0)
        return jax.nn.gelu(pre).astype(BF16)

    valid_r = lax.broadcasted_iota(jnp.int32, (n_blk, B_KV_W), 0) < n_blk - 1
    valid_c = lax.broadcasted_iota(jnp.int32, (B_HD, n_blk), 1) < n_blk - 1
    kc = jnp.zeros((n_blk, B_KV_W), F32)
    for g in range(B_GROUPS):
        kc = kc + _dot(hidden(kt_ref, pek_ref, w1k_ref, g), w2k_ref[g])
        vct = _dot_nt(w2vt_ref[...], hidden(vt_ref, pev_ref, w1v_ref, g))
        vct_ref[0, g * B_HD:(g + 1) * B_HD, :] = jnp.where(valid_c, vct, 0.0).astype(BF16)
    kc_ref[0] = jnp.where(valid_r, kc, 0.0).astype(BF16)


def _compress(kc_tok, vc_tok, pek, pev, w1k, w1v, w2k, w2vt):
    bsz, s, _ = kc_tok.shape
    n_blk = s // CMP_STRIDE
    full = lambda a: pl.BlockSpec(a.shape, lambda b: (0,) * a.ndim)
    tokspec = pl.BlockSpec((1, s, B_KV_W), lambda b: (b, 0, 0))
    return pl.pallas_call(
        functools.partial(_compress_kernel, n_blk=n_blk),
        grid=(bsz,),
        in_specs=[tokspec, tokspec, full(pek), full(pev), full(w1k), full(w1v), full(w2k), full(w2vt)],
        out_specs=(pl.BlockSpec((1, n_blk, B_KV_W), lambda b: (b, 0, 0)),
                   pl.BlockSpec((1, B_KV_W, n_blk), lambda b: (b, 0, 0))),
        out_shape=(jax.ShapeDtypeStruct((bsz, n_blk, B_KV_W), BF16),
                   jax.ShapeDtypeStruct((bsz, B_KV_W, n_blk), BF16)),
        compiler_params=pltpu.CompilerParams(
            dimension_semantics=("parallel",), vmem_limit_bytes=VMEM_LIMIT),
        name="compress",
    )(kc_tok, vc_tok, pek, pev, w1k, w1v, w2k, w2vt)


KV_TILE = 256
WIN_KEYS = WINDOW + Q_BLOCK
QW = B_HPG * Q_BLOCK


def _nsa_kernel(qt_ref, gt_ref, kc_ref, vct_ref, ks_ref, kw_ref, vst_ref, vwt_ref, ovt_ref,
                o_ref, bias_ref, *, n_cmp_rows, n_slc):
    g = pl.program_id(1)
    qi = pl.program_id(2)
    q0 = qi * Q_BLOCK

    qblk = qt_ref[0]
    qt = jnp.concatenate([qblk[h * B_HD:(h + 1) * B_HD, :] for h in range(B_HPG)], axis=1)
    zq = jnp.zeros_like(qt)
    qtp = jnp.where(g == 0, jnp.concatenate([qt, zq], axis=0), jnp.concatenate([zq, qt], axis=0))

    def tq(rows):
        return lax.broadcasted_iota(jnp.int32, (rows, QW), 1) & (Q_BLOCK - 1)

    sc = _dot(kc_ref[0], qtp)
    crow = lax.broadcasted_iota(jnp.int32, (n_cmp_rows, QW), 0)
    okc = CMP_STRIDE * crow + (CMP_LEN - 1) <= q0 + tq(n_cmp_rows)
    sc = jnp.where(okc, sc, NEG)
    pc = jnp.where(okc, jnp.exp(sc - jnp.max(sc, axis=0, keepdims=True)), 0.0)
    pc = pc / jnp.maximum(jnp.sum(pc, axis=0, keepdims=True), 1e-30)
    oc = _dot(vct_ref[0], pc.astype(BF16))

    psum = pc[:, 0:Q_BLOCK]
    for h in range(1, B_HPG):
        psum = psum + pc[:, h * Q_BLOCK:(h + 1) * Q_BLOCK]
    p_hi = psum.astype(BF16)
    p_lo = (psum - p_hi.astype(F32)).astype(BF16)
    imp = _dot(ovt_ref[...], p_hi) + _dot(ovt_ref[...], p_lo)
    jrow = lax.broadcasted_iota(jnp.int32, (n_slc, Q_BLOCK), 0)
    cur = (q0 + lax.broadcasted_iota(jnp.int32, (n_slc, Q_BLOCK), 1)) // SEL_LEN
    causal_blk = jrow <= cur
    forced = (jrow == 0) | (jrow == cur) | (jrow == cur - 1)
    imp = jnp.where(forced & causal_blk, FORCE_SCORE, imp)
    imp = jnp.where(causal_blk, imp, NEG)
    rank = jnp.zeros((n_slc, Q_BLOCK), F32)
    for i in range(n_slc):
        ri = imp[i:i + 1, :]
        above = (ri > imp) | ((ri == imp) & (jrow > i))
        rank = rank + jnp.where(above, 1.0, 0.0)
    sel = (rank < float(min(SEL_TOPK, n_slc))) & causal_blk
    bias_ref[...] = jnp.where(sel, 0.0, NEG)

    krow = lax.broadcasted_iota(jnp.int32, (KV_TILE, QW), 0)
    tqk = tq(KV_TILE)

    def sel_tile(kt, carry):
        m, l, acc = carry
        k0 = pl.multiple_of(kt * KV_TILE, KV_TILE)
        s = _dot(ks_ref[0, pl.ds(k0, KV_TILE), :], qtp)
        rows = []
        for jb in range(KV_TILE // SEL_LEN):
            r = bias_ref[pl.ds(kt * (KV_TILE // SEL_LEN) + jb, 1), :]
            rows.append(jnp.broadcast_to(jnp.concatenate([r] * B_HPG, axis=1), (SEL_LEN, QW)))
        s = s + jnp.concatenate(rows, axis=0)
        s = jnp.where(k0 + krow <= q0 + tqk, s, NEG)
        m_new = jnp.maximum(m, jnp.max(s, axis=0, keepdims=True))
        alpha = jnp.exp(m - m_new)
        p = jnp.exp(s - m_new)
        l = alpha * l + jnp.sum(p, axis=0, keepdims=True)
        acc = alpha * acc + _dot(vst_ref[0, :, pl.ds(k0, KV_TILE)], p.astype(BF16))
        return m_new, l, acc

    init = (jnp.full((1, QW), NEG, F32), jnp.zeros((1, QW), F32), jnp.zeros((B_HD, QW), F32))
    n_kv = (q0 + Q_BLOCK + KV_TILE - 1) // KV_TILE
    _, l_s, acc_s = lax.fori_loop(0, n_kv, sel_tile, init)
    o_s = acc_s / l_s

    ws = pl.multiple_of(jnp.maximum(q0 - WINDOW, 0), Q_BLOCK)
    sw = _dot(kw_ref[0, pl.ds(ws, WIN_KEYS), :], qtp)
    dist = (q0 - ws) + tq(WIN_KEYS) - lax.broadcasted_iota(jnp.int32, (WIN_KEYS, QW), 0)
    okw = lax.bitcast_convert_type(dist, jnp.uint32) < jnp.uint32(WINDOW)
    sw = jnp.where(okw, sw, NEG)
    pw = jnp.exp(sw - jnp.max(sw, axis=0, keepdims=True))
    o_w = _dot(vwt_ref[0, :, pl.ds(ws, WIN_KEYS)], pw.astype(BF16)) / jnp.sum(pw, axis=0, keepdims=True)

    gts = gt_ref[0]
    for h in range(B_HPG):
        sl = slice(h * Q_BLOCK, (h + 1) * Q_BLOCK)
        o_h = (gts[3 * h:3 * h + 1] * oc[:, sl] + gts[3 * h + 1:3 * h + 2] * o_s[:, sl]
               + gts[3 * h + 2:3 * h + 3] * o_w[:, sl])
        o_ref[0, h * B_HD:(h + 1) * B_HD, :] = o_h.astype(BF16)


def _nsa(qt, gt, kc, vct, ks, kw, vt, ovt):
    bsz, _, s = qt.shape
    n_cmp_rows = kc.shape[1]
    n_slc = s // SEL_LEN
    gw = B_HPG * B_HD
    return pl.pallas_call(
        functools.partial(_nsa_kernel, n_cmp_rows=n_cmp_rows, n_slc=n_slc),
        grid=(bsz, B_GROUPS, s // Q_BLOCK),
        in_specs=[
            pl.BlockSpec((1, gw, Q_BLOCK), lambda b, g, i: (b, g, i)),
            pl.BlockSpec((1, GATE_ROWS, Q_BLOCK), lambda b, g, i: (b, g, i)),
            pl.BlockSpec((1, n_cmp_rows, B_KV_W), lambda b, g, i: (b, 0, 0)),
            pl.BlockSpec((1, B_HD, n_cmp_rows), lambda b, g, i: (b, g, 0)),
            pl.BlockSpec((1, s, B_KV_W), lambda b, g, i: (b, 0, 0)),
            pl.BlockSpec((1, s, B_KV_W), lambda b, g, i: (b, 0, 0)),
            pl.BlockSpec((1, B_HD, s), lambda b, g, i: (b, g, 0)),
            pl.BlockSpec((1, B_HD, s), lambda b, g, i: (b, B_GROUPS + g, 0)),
            pl.BlockSpec(ovt.shape, lambda b, g, i: (0, 0)),
        ],
        out_specs=pl.BlockSpec((1, gw, Q_BLOCK), lambda b, g, i: (b, g, i)),
        out_shape=jax.ShapeDtypeStruct((bsz, B_Q_W, s), BF16),
        scratch_shapes=[pltpu.VMEM((n_slc, Q_BLOCK), F32)],
        compiler_params=pltpu.CompilerParams(
            dimension_semantics=("parallel", "parallel", "arbitrary"), vmem_limit_bytes=VMEM_LIMIT),
        name="nsa",
    )(qt, gt, kc, vct, ks, kw, vt, vt, ovt)


FF_CHUNK = 1024


def _tail_kernel(x_ref, ya_ref, ybt_ref, mg_ref, wa_ref, wb_ref, wo_ref, g2_ref, w1_ref, w2_ref, gf_ref, o_ref):
    pa = _dot(ya_ref[0], wa_ref[...])
    pb = _dot_tn(ybt_ref[0], wb_ref[...])
    mg = mg_ref[0].astype(F32)
    merged = jax.nn.sigmoid(mg[:, 0:D_MODEL]) * pa + jax.nn.sigmoid(mg[:, D_MODEL:2 * D_MODEL]) * pb
    h = x_ref[0] + _dot(merged.astype(BF16), wo_ref[...])
    hn = _rms(h, g2_ref[...]).astype(BF16)
    acc = jnp.zeros_like(h)
    for c in range(D_FF // FF_CHUNK):
        z = _dot(hn, w1_ref[:, c * FF_CHUNK:(c + 1) * FF_CHUNK])
        acc = acc + _dot(jnp.square(jnp.maximum(z, 0.0)).astype(BF16), w2_ref[c * FF_CHUNK:(c + 1) * FF_CHUNK, :])
    o_ref[0] = _rms(h + acc, gf_ref[...])


def _tail(x, ya, ybt, mg, wa, wb, wo, g2, w1, w2, gf):
    bsz, s, _ = x.shape
    tm = min(256, s)
    tok = lambda w: pl.BlockSpec((1, tm, w), lambda b, t: (b, t, 0))
    const = lambda a: pl.BlockSpec(a.shape, lambda b, t: (0,) * a.ndim, pipeline_mode=pl.Buffered(1))
    return pl.pallas_call(
        _tail_kernel,
        grid=(bsz, s // tm),
        in_specs=[tok(D_MODEL), tok(A_W), pl.BlockSpec((1, B_Q_W, tm), lambda b, t: (b, 0, t)), tok(2 * D_MODEL),
                  const(wa), const(wb), const(wo), const(g2), const(w1), const(w2), const(gf)],
        out_specs=tok(D_MODEL),
        out_shape=jax.ShapeDtypeStruct((bsz, s, D_MODEL), F32),
        compiler_params=pltpu.CompilerParams(
            dimension_semantics=("parallel", "parallel"), vmem_limit_bytes=VMEM_LIMIT),
        name="tail",
    )(x, ya, ybt, mg, wa, wb, wo, g2, w1, w2, gf)


def _pad_groups(a, axis):
    outs = []
    for g in range(B_GROUPS):
        pads = [(0, 0)] * a.ndim
        pads[axis] = (g * B_HD, (B_GROUPS - 1 - g) * B_HD)
        outs.append(jnp.pad(a, pads))
    return jnp.stack(outs)


def _overlap_t(s):
    n_cmp = (s - CMP_LEN) // CMP_STRIDE + 1
    n_slc = s // SEL_LEN
    cs = np.arange(n_cmp)[:, None] * CMP_STRIDE
    ss = np.arange(n_slc)[None, :] * SEL_LEN
    ov = np.clip(np.minimum(cs + CMP_LEN, ss + SEL_LEN) - np.maximum(cs, ss), 0, None) / CMP_LEN
    ovt = np.zeros((n_slc, s // CMP_STRIDE), np.float32)
    ovt[:, :n_cmp] = ov.T
    return jnp.asarray(ovt, BF16)


def kernel(x, positions, norm1_g, w_in, lb_param, hgrn_norm_g, cmp_pe_k, cmp_pe_v, cmp_w1_k, cmp_w2_k,
           cmp_w1_v, cmp_w2_v, w_br_a, w_br_b, w_out, norm2_g, w_ff1, w_ff2, final_g):
    bsz, s, _ = x.shape
    assert norm1_g.shape[0] == 1, "single-layer block"

    w = w_in[0]
    o_bq = 4 * A_W
    o_kv = o_bq + B_Q_W
    o_gate = o_kv + 6 * B_KV_W
    o_mg = o_gate + 3 * B_HEADS
    kv = lambda i: w[:, o_kv + i * B_KV_W:o_kv + (i + 1) * B_KV_W]
    w_nat = jnp.concatenate([w[:, 0:o_bq], w[:, o_mg:o_mg + 2 * D_MODEL], kv(1)], axis=1).astype(BF16)
    wg = w[:, o_gate:o_mg].reshape(D_MODEL, B_GROUPS, 3 * B_HPG)
    wg = jnp.pad(wg, ((0, 0), (0, 0), (0, GATE_ROWS - 3 * B_HPG))).reshape(D_MODEL, B_GROUPS * GATE_ROWS)
    w_t = jnp.concatenate([w[:, o_bq:o_kv], kv(0), kv(2), kv(4), kv(3), kv(5), wg], axis=1).T.astype(BF16)
    invf = (ROPE_THETA ** (-jnp.arange(0, ROT_DIM, 2, dtype=F32) / ROT_DIM)).reshape(ROT_HALF, 1)

    hg, mg, vc_tok, kc_tok, ks, kw, qt, vt, gt = _proj(
        x, positions.reshape(bsz, 1, s), norm1_g, invf, w_nat, w_t)

    ya = _hgrn(hg, lb_param, hgrn_norm_g)

    w1k = _pad_groups(cmp_w1_k[0].reshape(CMP_LEN, B_HD, CMP_HIDDEN), 1).astype(BF16)
    w1v = _pad_groups(cmp_w1_v[0].reshape(CMP_LEN, B_HD, CMP_HIDDEN), 1).astype(BF16)
    kc, vct = _compress(
        kc_tok, vc_tok, _pad_groups(cmp_pe_k[0], 1), _pad_groups(cmp_pe_v[0], 1), w1k, w1v,
        _pad_groups(cmp_w2_k[0], 1).astype(BF16), cmp_w2_v[0].T.astype(BF16))

    ybt = _nsa(qt, gt, kc, vct, ks, kw, vt, _overlap_t(s))

    return _tail(x, ya, ybt, mg, w_br_a[0].astype(BF16), w_br_b[0].astype(BF16), w_out[0].astype(BF16),
                 norm2_g, w_ff1[0].astype(BF16), w_ff2[0].astype(BF16), final_g.reshape(1, D_MODEL))
```

```python
import functools

import jax
import jax.numpy as jnp
import numpy as np
from jax import lax
from jax.experimental import pallas as pl
from jax.experimental.pallas import tpu as pltpu

F32 = jnp.float32
BF16 = jnp.bfloat16

D_MODEL = 1024
A_HEADS = 4
A_DK = 128
A_DV = 128
A_CHUNK = 64
A_W = A_HEADS * A_DK
B_HEADS = 8
B_GROUPS = 2
B_HPG = B_HEADS // B_GROUPS
B_HD = 64
B_Q_W = B_HEADS * B_HD
B_KV_W = B_GROUPS * B_HD
CMP_LEN = 32
CMP_STRIDE = 16
CMP_HIDDEN = 256
SEL_LEN = 64
SEL_TOPK = 16
WINDOW = 512
Q_BLOCK = 128
FORCE_SCORE = 1e4
NEG = -1e30
ROPE_THETA = 500000.0
ROT_DIM = B_HD // 4
ROT_HALF = ROT_DIM // 2
D_FF = 4 * D_MODEL
EPS = 1e-6
GATE_ROWS = 16
SEL_SHIFT = SEL_LEN.bit_length() - 1
SEL_ONEHOT_W = 128
KS_AUG_W = B_KV_W + SEL_ONEHOT_W

VMEM_LIMIT = 56 * 1024 * 1024

NT = (((1,), (1,)), ((), ()))
TN = (((0,), (0,)), ((), ()))


def _dot(a, b):
    return jnp.dot(a, b, preferred_element_type=F32)


def _dot_nt(a, b):
    return lax.dot_general(a, b, NT, preferred_element_type=F32)


def _dot_tn(a, b):
    return lax.dot_general(a, b, TN, preferred_element_type=F32)


def _rms(x, g):
    return x * lax.rsqrt(jnp.mean(x * x, axis=-1, keepdims=True) + EPS) * g


def _proj_kernel(x_ref, pos_ref, g1_ref, invf_ref, wnat_ref, wt_ref,
                 hg_ref, mg_ref, vc_ref, kc_ref, ks_ref, kw_ref, qt_ref, vt_ref, gt_ref):
    xn = _rms(x_ref[0], g1_ref[...]).astype(BF16)

    def nat(c0, c1):
        return _dot(xn, wnat_ref[:, c0:c1])

    half = 2 * A_W
    for c in range(2):
        hg_ref[0, :, c * half:(c + 1) * half] = nat(c * half, (c + 1) * half).astype(BF16)
        mg_ref[0, :, c * half:(c + 1) * half] = nat(4 * A_W + c * half, 4 * A_W + (c + 1) * half).astype(BF16)
    vc_ref[0] = nat(4 * A_W + 2 * D_MODEL, 4 * A_W + 2 * D_MODEL + B_KV_W)

    def tr(r0, r1):
        return _dot_nt(wt_ref[r0:r1, :], xn)

    ang = invf_ref[...] * pos_ref[0].astype(F32)
    cos = jnp.cos(ang)
    sin = jnp.sin(ang)

    def rope(xt, n_heads):
        pieces = []
        for h in range(n_heads):
            b0 = h * B_HD
            t1 = xt[b0:b0 + ROT_HALF]
            t2 = xt[b0 + ROT_HALF:b0 + ROT_DIM]
            pieces += [t1 * cos - t2 * sin, t2 * cos + t1 * sin, xt[b0 + ROT_DIM:b0 + B_HD]]
        return jnp.concatenate(pieces, axis=0)

    r_q, r_k, r_v = B_Q_W, B_Q_W + 3 * B_KV_W, B_Q_W + 5 * B_KV_W
    qt_ref[0] = (rope(tr(0, r_q), B_HEADS) * (B_HD ** -0.5)).astype(BF16)
    k3 = rope(tr(r_q, r_k), 3 * B_GROUPS).T
    kc_ref[0] = k3[:, 0:B_KV_W]
    kw_ref[0] = k3[:, 2 * B_KV_W:3 * B_KV_W].astype(BF16)
    tm = k3.shape[0]
    tok = pl.program_id(1) * tm + lax.broadcasted_iota(jnp.int32, (tm, SEL_ONEHOT_W), 0)
    lane = lax.broadcasted_iota(jnp.int32, (tm, SEL_ONEHOT_W), 1)
    ks_ref[0, :, 0:B_KV_W] = k3[:, B_KV_W:2 * B_KV_W].astype(BF16)
    ks_ref[0, :, B_KV_W:B_KV_W + SEL_ONEHOT_W] = jnp.where((tok >> SEL_SHIFT) == lane, 1.0, 0.0).astype(BF16)
    vt_ref[0] = tr(r_k, r_v).astype(BF16)
    gt_ref[0] = jax.nn.sigmoid(tr(r_v, r_v + B_GROUPS * GATE_ROWS))


def _proj(x, pos3, g1, invf, w_nat, w_t):
    bsz, s, _ = x.shape
    tm = min(512, s)
    n_nat = w_nat.shape[1]
    n_t = w_t.shape[0]
    tok = lambda w: pl.BlockSpec((1, tm, w), lambda b, t: (b, t, 0))
    trn = lambda r: pl.BlockSpec((1, r, tm), lambda b, t: (b, 0, t))
    const = lambda shp: pl.BlockSpec(shp, lambda b, t: (0,) * len(shp), pipeline_mode=pl.Buffered(1))
    out_shape = (
        jax.ShapeDtypeStruct((bsz, s, 4 * A_W), BF16),
        jax.ShapeDtypeStruct((bsz, s, 2 * D_MODEL), BF16),
        jax.ShapeDtypeStruct((bsz, s, B_KV_W), F32),
        jax.ShapeDtypeStruct((bsz, s, B_KV_W), F32),
        jax.ShapeDtypeStruct((bsz, s, KS_AUG_W), BF16),
        jax.ShapeDtypeStruct((bsz, s, B_KV_W), BF16),
        jax.ShapeDtypeStruct((bsz, B_Q_W, s), BF16),
        jax.ShapeDtypeStruct((bsz, 2 * B_KV_W, s), BF16),
        jax.ShapeDtypeStruct((bsz, B_GROUPS * GATE_ROWS, s), F32),
    )
    return pl.pallas_call(
        _proj_kernel,
        grid=(bsz, s // tm),
        in_specs=[tok(D_MODEL), trn(1), const((1, D_MODEL)), const((ROT_HALF, 1)),
                  const((D_MODEL, n_nat)), const((n_t, D_MODEL))],
        out_specs=(tok(4 * A_W), tok(2 * D_MODEL), tok(B_KV_W), tok(B_KV_W), tok(KS_AUG_W), tok(B_KV_W),
                   trn(B_Q_W), trn(2 * B_KV_W), trn(B_GROUPS * GATE_ROWS)),
        out_shape=out_shape,
        compiler_params=pltpu.CompilerParams(
            dimension_semantics=("parallel", "parallel"), vmem_limit_bytes=VMEM_LIMIT),
        name="proj",
    )(x, pos3, g1, invf, w_nat, w_t)


def _hgrn_kernel(hg_ref, lbp_ref, ng_ref, o_ref, st_ref, *, n_chunks):
    @pl.when(pl.program_id(1) == 0)
    def _():
        st_ref[...] = jnp.zeros_like(st_ref)

    lbp = lbp_ref[...]
    e = jnp.exp(lbp - jnp.max(lbp, axis=0, keepdims=True))
    lb = e[0:1] / jnp.sum(e, axis=0, keepdims=True)
    ng = ng_ref[...]
    row = lax.broadcasted_iota(jnp.int32, (A_CHUNK, A_CHUNK), 0)
    col = lax.broadcasted_iota(jnp.int32, (A_CHUNK, A_CHUNK), 1)
    causal = row >= col
    tri = causal.astype(F32)

    def chunk(c, carry):
        r0 = pl.multiple_of(c * A_CHUNK, A_CHUNK)
        blk = hg_ref[0, pl.ds(r0, A_CHUNK), :]
        q = blk[:, 0:A_W].astype(F32)
        fp = blk[:, A_W:2 * A_W].astype(F32)
        v = blk[:, 2 * A_W:3 * A_W]
        g = blk[:, 3 * A_W:4 * A_W].astype(F32)
        f = lb + (1.0 - lb) * jax.nn.sigmoid(fp)
        k = 1.0 - f
        b = jnp.dot(tri, jnp.log(f), precision=lax.Precision.HIGHEST,
                    preferred_element_type=F32)
        b_last = b[A_CHUNK - 1:A_CHUNK, :]
        q_dec = (q * jnp.exp(b)).astype(BF16)
        k_dec = (k * jnp.exp(-b)).astype(BF16)
        k_til = (k * jnp.exp(b_last - b)).astype(BF16)
        dec = jnp.exp(b_last)
        for h in range(A_HEADS):
            sl = slice(h * A_DK, (h + 1) * A_DK)
            attn = jnp.where(causal, _dot_nt(q_dec[:, sl], k_dec[:, sl]), 0.0).astype(BF16)
            st_t = st_ref[h]
            o = _dot(attn, v[:, sl]) + _dot_nt(q_dec[:, sl], st_t.astype(BF16))
            st_ref[h] = st_t * dec[:, sl] + _dot_tn(v[:, sl], k_til[:, sl])
            gh = g[:, sl]
            y = _rms(o, ng) * (gh * jax.nn.sigmoid(gh))
            o_ref[0, pl.ds(r0, A_CHUNK), sl] = y.astype(BF16)
        return carry

    lax.fori_loop(0, n_chunks, chunk, 0)


def _hgrn(hg, lb_param, norm_g):
    bsz, s, _ = hg.shape
    tt = min(512, s)
    return pl.pallas_call(
        functools.partial(_hgrn_kernel, n_chunks=tt // A_CHUNK),
        grid=(bsz, s // tt),
        in_specs=[pl.BlockSpec((1, tt, 4 * A_W), lambda b, t: (b, t, 0)),
                  pl.BlockSpec(lb_param.shape, lambda b, t: (0, 0)),
                  pl.BlockSpec((1, A_DV), lambda b, t: (0, 0))],
        out_specs=pl.BlockSpec((1, tt, A_W), lambda b, t: (b, t, 0)),
        out_shape=jax.ShapeDtypeStruct((bsz, s, A_W), BF16),
        scratch_shapes=[pltpu.VMEM((A_HEADS, A_DV, A_DK), F32)],
        compiler_params=pltpu.CompilerParams(
            dimension_semantics=("parallel", "arbitrary"), vmem_limit_bytes=VMEM_LIMIT),
        name="hgrn",
    )(hg, lb_param, norm_g)


def _compress_kernel(kt_ref, vt_ref, pek_ref, pev_ref, w1k_ref, w1v_ref, w2k_ref, w2vt_ref,
                     kc_ref, vct_ref, *, n_blk):
    half = CMP_LEN // 2

    def hidden(t_ref, pe_ref, w1_ref, g):
        acc = [jnp.zeros((n_blk, CMP_HIDDEN), F32), jnp.zeros((n_blk, CMP_HIDDEN), F32)]
        for l in range(CMP_LEN):
            rows = t_ref[0, pl.ds(l % half, n_blk, stride=CMP_STRIDE), :] + pe_ref[g, l:l + 1, :]
            acc[l // half] = acc[l // half] + _dot(rows.astype(BF16), w1_ref[g, l])
        pre = acc[0] + pltpu.roll(acc[1], n_blk - 1, 0)
        return jax.nn.gelu(pre).astype(BF16)

    valid_r = lax.broadcasted_iota(jnp.int32, (n_blk, B_KV_W), 0) < n_blk - 1
    valid_c = lax.broadcasted_iota(jnp.int32, (B_HD, n_blk), 1) < n_blk - 1
    kc = jnp.zeros((n_blk, B_KV_W), F32)
    for g in range(B_GROUPS):
        kc = kc + _dot(hidden(kt_ref, pek_ref, w1k_ref, g), w2k_ref[g])
        vct = _dot_nt(w2vt_ref[...], hidden(vt_ref, pev_ref, w1v_ref, g))
        vct_ref[0, g * B_HD:(g + 1) * B_HD, :] = jnp.where(valid_c, vct, 0.0).astype(BF16)
    kc_ref[0] = jnp.where(valid_r, kc, 0.0).astype(BF16)


def _compress(kc_tok, vc_tok, pek, pev, w1k, w1v, w2k, w2vt):
    bsz, s, _ = kc_tok.shape
    n_blk = s // CMP_STRIDE
    full = lambda a: pl.BlockSpec(a.shape, lambda b: (0,) * a.ndim)
    tokspec = pl.BlockSpec((1, s, B_KV_W), lambda b: (b, 0, 0))
    return pl.pallas_call(
        functools.partial(_compress_kernel, n_blk=n_blk),
        grid=(bsz,),
        in_specs=[tokspec, tokspec, full(pek), full(pev), full(w1k), full(w1v), full(w2k), full(w2vt)],
        out_specs=(pl.BlockSpec((1, n_blk, B_KV_W), lambda b: (b, 0, 0)),
                   pl.BlockSpec((1, B_KV_W, n_blk), lambda b: (b, 0, 0))),
        out_shape=(jax.ShapeDtypeStruct((bsz, n_blk, B_KV_W), BF16),
                   jax.ShapeDtypeStruct((bsz, B_KV_W, n_blk), BF16)),
        compiler_params=pltpu.CompilerParams(
            dimension_semantics=("parallel",), vmem_limit_bytes=VMEM_LIMIT),
        name="compress",
    )(kc_tok, vc_tok, pek, pev, w1k, w1v, w2k, w2vt)


KV_TILE = 512
WIN_KEYS = WINDOW + Q_BLOCK
QW = B_HPG * Q_BLOCK
LW = B_GROUPS * QW


def _block_rank(imp, n_slc):
    sub = lax.broadcasted_iota(jnp.int32, (8, Q_BLOCK), 0)
    rank = jnp.zeros((n_slc, Q_BLOCK), F32)
    for i in range(n_slc):
        ri = imp[i:i + 1, :]
        parts = []
        for r in range(n_slc // 8):
            blk = imp[8 * r:8 * r + 8, :]
            gt = jnp.where(ri > blk, 1.0, 0.0)
            ge = jnp.where(ri >= blk, 1.0, 0.0)
            if 8 * r + 7 <= i:
                parts.append(gt)
            elif 8 * r > i:
                parts.append(ge)
            else:
                parts.append(jnp.where(sub > i - 8 * r, ge, gt))
        rank = rank + jnp.concatenate(parts, axis=0)
    return rank


def _nsa_kernel(qt_ref, gt_ref, kc_ref, vct_ref, ks_ref, kw_ref, vt_ref, ovt_ref,
                o_ref, rhs_ref, *, n_cmp_rows, n_slc):
    qi = pl.program_id(1)
    q0 = qi * Q_BLOCK

    qblk = qt_ref[0]
    zq = jnp.zeros((B_HD, QW), BF16)
    qrows = []
    for g in range(B_GROUPS):
        qg = jnp.concatenate(
            [qblk[(g * B_HPG + h) * B_HD:(g * B_HPG + h + 1) * B_HD, :] for h in range(B_HPG)], axis=1)
        qrows.append(jnp.concatenate([qg if gg == g else zq for gg in range(B_GROUPS)], axis=1))
    qbd = jnp.concatenate(qrows, axis=0)

    def tq(rows):
        return lax.broadcasted_iota(jnp.int32, (rows, LW), 1) & (Q_BLOCK - 1)

    sc = _dot(kc_ref[0], qbd)
    crow = lax.broadcasted_iota(jnp.int32, (n_cmp_rows, LW), 0)
    okc = CMP_STRIDE * crow + (CMP_LEN - 1) <= q0 + tq(n_cmp_rows)
    sc = jnp.where(okc, sc, NEG)
    pc = jnp.where(okc, jnp.exp(sc - jnp.max(sc, axis=0, keepdims=True)), 0.0)
    pc = pc / jnp.maximum(jnp.sum(pc, axis=0, keepdims=True), 1e-30)
    pcb = pc.astype(BF16)
    oc = [_dot(vct_ref[0, g * B_HD:(g + 1) * B_HD, :], pcb[:, g * QW:(g + 1) * QW]) for g in range(B_GROUPS)]

    jrow = lax.broadcasted_iota(jnp.int32, (n_slc, Q_BLOCK), 0)
    cur = (q0 + lax.broadcasted_iota(jnp.int32, (n_slc, Q_BLOCK), 1)) >> SEL_SHIFT
    causal_blk = jrow <= cur
    forced = (jrow == 0) | (jrow == cur) | (jrow == cur - 1)
    biases = []
    for g in range(B_GROUPS):
        psum = pc[:, g * QW:g * QW + Q_BLOCK]
        for h in range(1, B_HPG):
            psum = psum + pc[:, g * QW + h * Q_BLOCK:g * QW + (h + 1) * Q_BLOCK]
        p_hi = psum.astype(BF16)
        p_lo = (psum - p_hi.astype(F32)).astype(BF16)
        imp = _dot(ovt_ref[...], p_hi) + _dot(ovt_ref[...], p_lo)
        imp = jnp.where(forced & causal_blk, FORCE_SCORE, imp)
        imp = jnp.where(causal_blk, imp, NEG)
        sel = (_block_rank(imp, n_slc) < float(min(SEL_TOPK, n_slc))) & causal_blk
        biases += [jnp.where(sel, 0.0, NEG).astype(BF16)] * B_HPG
    bias = jnp.concatenate(biases, axis=1)
    rhs_ref[...] = jnp.concatenate(
        [qbd, bias, jnp.zeros((KS_AUG_W - B_KV_W - n_slc, LW), BF16)], axis=0)

    def sel_tile(kt, carry, diagonal):
        m, l, accs = carry
        k0 = pl.multiple_of(kt * KV_TILE, KV_TILE)
        s = _dot(ks_ref[0, pl.ds(k0, KV_TILE), :], rhs_ref[...])
        if diagonal:
            krow = lax.broadcasted_iota(jnp.int32, (KV_TILE, LW), 0)
            s = jnp.where(k0 + krow <= q0 + tq(KV_TILE), s, NEG)
        m_new = jnp.maximum(m, jnp.max(s, axis=0, keepdims=True))
        alpha = jnp.exp(m - m_new)
        p = jnp.exp(s - m_new)
        l = alpha * l + jnp.sum(p, axis=0, keepdims=True)
        pb = p.astype(BF16)
        accs = tuple(
            alpha[:, g * QW:(g + 1) * QW] * accs[g]
            + _dot(vt_ref[0, g * B_HD:(g + 1) * B_HD, pl.ds(k0, KV_TILE)], pb[:, g * QW:(g + 1) * QW])
            for g in range(B_GROUPS))
        return m_new, l, accs

    init = (jnp.full((1, LW), NEG, F32), jnp.zeros((1, LW), F32),
            tuple(jnp.zeros((B_HD, QW), F32) for _ in range(B_GROUPS)))
    n_full = q0 // KV_TILE
    carry = lax.fori_loop(0, n_full, functools.partial(sel_tile, diagonal=False), init)
    _, l_s, acc_s = sel_tile(n_full, carry, diagonal=True)

    ws = pl.multiple_of(jnp.maximum(q0 - WINDOW, 0), Q_BLOCK)
    sw = _dot(kw_ref[0, pl.ds(ws, WIN_KEYS), :], qbd)
    dist = (q0 - ws) + tq(WIN_KEYS) - lax.broadcasted_iota(jnp.int32, (WIN_KEYS, LW), 0)
    okw = lax.bitcast_convert_type(dist, jnp.uint32) < jnp.uint32(WINDOW)
    sw = jnp.where(okw, sw, NEG)
    pw = jnp.exp(sw - jnp.max(sw, axis=0, keepdims=True))
    l_w = jnp.sum(pw, axis=0, keepdims=True)
    pwb = pw.astype(BF16)

    gts = gt_ref[0]
    for g in range(B_GROUPS):
        vw = vt_ref[0, (B_GROUPS + g) * B_HD:(B_GROUPS + g + 1) * B_HD, pl.ds(ws, WIN_KEYS)]
        o_s = acc_s[g] / l_s[:, g * QW:(g + 1) * QW]
        o_w = _dot(vw, pwb[:, g * QW:(g + 1) * QW]) / l_w[:, g * QW:(g + 1) * QW]
        for h in range(B_HPG):
            sl = slice(h * Q_BLOCK, (h + 1) * Q_BLOCK)
            r = g * GATE_ROWS + 3 * h
            o_h = gts[r:r + 1] * oc[g][:, sl] + gts[r + 1:r + 2] * o_s[:, sl] + gts[r + 2:r + 3] * o_w[:, sl]
            o_ref[0, (g * B_HPG + h) * B_HD:(g * B_HPG + h + 1) * B_HD, :] = o_h.astype(BF16)


def _nsa(qt, gt, kc, vct, ks, kw, vt, ovt):
    bsz, _, s = qt.shape
    n_cmp_rows = kc.shape[1]
    n_slc = s // SEL_LEN
    assert n_slc <= SEL_ONEHOT_W and s % KV_TILE == 0 and s >= WIN_KEYS
    whole = lambda a: pl.BlockSpec((1,) + a.shape[1:], lambda b, i: (b, 0, 0))
    return pl.pallas_call(
        functools.partial(_nsa_kernel, n_cmp_rows=n_cmp_rows, n_slc=n_slc),
        grid=(bsz, s // Q_BLOCK),
        in_specs=[
            pl.BlockSpec((1, B_Q_W, Q_BLOCK), lambda b, i: (b, 0, i)),
            pl.BlockSpec((1, B_GROUPS * GATE_ROWS, Q_BLOCK), lambda b, i: (b, 0, i)),
            whole(kc), whole(vct), whole(ks), whole(kw), whole(vt),
            pl.BlockSpec(ovt.shape, lambda b, i: (0, 0)),
        ],
        out_specs=pl.BlockSpec((1, B_Q_W, Q_BLOCK), lambda b, i: (b, 0, i)),
        out_shape=jax.ShapeDtypeStruct((bsz, B_Q_W, s), BF16),
        scratch_shapes=[pltpu.VMEM((KS_AUG_W, LW), BF16)],
        compiler_params=pltpu.CompilerParams(
            dimension_semantics=("parallel", "arbitrary"), vmem_limit_bytes=VMEM_LIMIT),
        name="nsa",
    )(qt, gt, kc, vct, ks, kw, vt, ovt)


FF_CHUNK = 1024


def _tail_kernel(x_ref, ya_ref, ybt_ref, mg_ref, wa_ref, wb_ref, wo_ref, g2_ref, w1_ref, w2_ref, gf_ref, o_ref):
    pa = _dot(ya_ref[0], wa_ref[...])
    pb = _dot_tn(ybt_ref[0], wb_ref[...])
    mg = mg_ref[0].astype(F32)
    merged = jax.nn.sigmoid(mg[:, 0:D_MODEL]) * pa + jax.nn.sigmoid(mg[:, D_MODEL:2 * D_MODEL]) * pb
    h = x_ref[0] + _dot(merged.astype(BF16), wo_ref[...])
    hn = _rms(h, g2_ref[...]).astype(BF16)
    acc = jnp.zeros_like(h)
    for c in range(D_FF // FF_CHUNK):
        z = _dot(hn, w1_ref[:, c * FF_CHUNK:(c + 1) * FF_CHUNK])
        acc = acc + _dot(jnp.square(jnp.maximum(z, 0.0)).astype(BF16), w2_ref[c * FF_CHUNK:(c + 1) * FF_CHUNK, :])
    o_ref[0] = _rms(h + acc, gf_ref[...])


def _tail(x, ya, ybt, mg, wa, wb, wo, g2, w1, w2, gf):
    bsz, s, _ = x.shape
    tm = min(256, s)
    tok = lambda w: pl.BlockSpec((1, tm, w), lambda b, t: (b, t, 0))
    const = lambda a: pl.BlockSpec(a.shape, lambda b, t: (0,) * a.ndim, pipeline_mode=pl.Buffered(1))
    return pl.pallas_call(
        _tail_kernel,
        grid=(bsz, s // tm),
        in_specs=[tok(D_MODEL), tok(A_W), pl.BlockSpec((1, B_Q_W, tm), lambda b, t: (b, 0, t)), tok(2 * D_MODEL),
                  const(wa), const(wb), const(wo), const(g2), const(w1), const(w2), const(gf)],
        out_specs=tok(D_MODEL),
        out_shape=jax.ShapeDtypeStruct((bsz, s, D_MODEL), F32),
        compiler_params=pltpu.CompilerParams(
            dimension_semantics=("parallel", "parallel"), vmem_limit_bytes=VMEM_LIMIT),
        name="tail",
    )(x, ya, ybt, mg, wa, wb, wo, g2, w1, w2, gf)


def _pad_groups(a, axis):
    outs = []
    for g in range(B_GROUPS):
        pads = [(0, 0)] * a.ndim
        pads[axis] = (g * B_HD, (B_GROUPS - 1 - g) * B_HD)
        outs.append(jnp.pad(a, pads))
    return jnp.stack(outs)


def _overlap_t(s):
    n_cmp = (s - CMP_LEN) // CMP_STRIDE + 1
    n_slc = s // SEL_LEN
    cs = np.arange(n_cmp)[:, None] * CMP_STRIDE
    ss = np.arange(n_slc)[None, :] * SEL_LEN
    ov = np.clip(np.minimum(cs + CMP_LEN, ss + SEL_LEN) - np.maximum(cs, ss), 0, None) / CMP_LEN
    ovt = np.zeros((n_slc, s // CMP_STRIDE), np.float32)
    ovt[:, :n_cmp] = ov.T
    return jnp.asarray(ovt, BF16)


def kernel(x, positions, norm1_g, w_in, lb_param, hgrn_norm_g, cmp_pe_k, cmp_pe_v, cmp_w1_k, cmp_w2_k,
           cmp_w1_v, cmp_w2_v, w_br_a, w_br_b, w_out, norm2_g, w_ff1, w_ff2, final_g):
    bsz, s, _ = x.shape
    assert norm1_g.shape[0] == 1, "single-layer block"

    w = w_in[0]
    o_bq = 4 * A_W
    o_kv = o_bq + B_Q_W
    o_gate = o_kv + 6 * B_KV_W
    o_mg = o_gate + 3 * B_HEADS
    kv = lambda i: w[:, o_kv + i * B_KV_W:o_kv + (i + 1) * B_KV_W]
    w_nat = jnp.concatenate([w[:, 0:o_bq], w[:, o_mg:o_mg + 2 * D_MODEL], kv(1)], axis=1).astype(BF16)
    wg = w[:, o_gate:o_mg].reshape(D_MODEL, B_GROUPS, 3 * B_HPG)
    wg = jnp.pad(wg, ((0, 0), (0, 0), (0, GATE_ROWS - 3 * B_HPG))).reshape(D_MODEL, B_GROUPS * GATE_ROWS)
    w_t = jnp.concatenate([w[:, o_bq:o_kv], kv(0), kv(2), kv(4), kv(3), kv(5), wg], axis=1).T.astype(BF16)
    invf = (ROPE_THETA ** (-jnp.arange(0, ROT_DIM, 2, dtype=F32) / ROT_DIM)).reshape(ROT_HALF, 1)

    hg, mg, vc_tok, kc_tok, ks, kw, qt, vt, gt = _proj(
        x, positions.reshape(bsz, 1, s), norm1_g, invf, w_nat, w_t)

    ya = _hgrn(hg, lb_param, hgrn_norm_g)

    w1k = _pad_groups(cmp_w1_k[0].reshape(CMP_LEN, B_HD, CMP_HIDDEN), 1).astype(BF16)
    w1v = _pad_groups(cmp_w1_v[0].reshape(CMP_LEN, B_HD, CMP_HIDDEN), 1).astype(BF16)
    kc, vct = _compress(
        kc_tok, vc_tok, _pad_groups(cmp_pe_k[0], 1), _pad_groups(cmp_pe_v[0], 1), w1k, w1v,
        _pad_groups(cmp_w2_k[0], 1).astype(BF16), cmp_w2_v[0].T.astype(BF16))

    ybt = _nsa(qt, gt, kc, vct, ks, kw, vt, _overlap_t(s))

    return _tail(x, ya, ybt, mg, w_br_a[0].astype(BF16), w_br_b[0].astype(BF16), w_out[0].astype(BF16),
                 norm2_g, w_ff1[0].astype(BF16), w_ff2[0].astype(BF16), final_g.reshape(1, D_MODEL))
```

```python
import functools

import jax
import jax.numpy as jnp
import numpy as np
from jax import lax
from jax.experimental import pallas as pl
from jax.experimental.pallas import tpu as pltpu

F32 = jnp.float32
BF16 = jnp.bfloat16

D_MODEL = 1024
A_HEADS = 4
A_DK = 128
A_DV = 128
A_CHUNK = 64
A_W = A_HEADS * A_DK
B_HEADS = 8
B_GROUPS = 2
B_HPG = B_HEADS // B_GROUPS
B_HD = 64
B_Q_W = B_HEADS * B_HD
B_KV_W = B_GROUPS * B_HD
CMP_LEN = 32
CMP_STRIDE = 16
CMP_HIDDEN = 256
SEL_LEN = 64
SEL_TOPK = 16
WINDOW = 512
Q_BLOCK = 128
FORCE_SCORE = 1e4
NEG = -1e30
ROPE_THETA = 500000.0
ROT_DIM = B_HD // 4
ROT_HALF = ROT_DIM // 2
D_FF = 4 * D_MODEL
EPS = 1e-6
LOG2_E = 1.4426950408889634
GATE_ROWS = 16
SEL_SHIFT = SEL_LEN.bit_length() - 1
SEL_ONEHOT_W = 128
KS_AUG_W = B_KV_W + SEL_ONEHOT_W

VMEM_LIMIT = 56 * 1024 * 1024

NT = (((1,), (1,)), ((), ()))
TN = (((0,), (0,)), ((), ()))


def _dot(a, b):
    return jnp.dot(a, b, preferred_element_type=F32)


def _dot_nt(a, b):
    return lax.dot_general(a, b, NT, preferred_element_type=F32)


def _dot_tn(a, b):
    return lax.dot_general(a, b, TN, preferred_element_type=F32)


def _rms(x, g):
    return x * lax.rsqrt(jnp.mean(x * x, axis=-1, keepdims=True) + EPS) * g


def _proj_kernel(x_ref, pos_ref, g1_ref, invf_ref, wnat_ref, wt_ref,
                 hg_ref, mg_ref, vc_ref, kc_ref, ks_ref, kw_ref, qt_ref, vt_ref, gt_ref):
    xn = _rms(x_ref[0], g1_ref[...]).astype(BF16)

    def nat(c0, c1):
        return _dot(xn, wnat_ref[:, c0:c1])

    half = 2 * A_W
    for c in range(2):
        hg_ref[0, :, c * half:(c + 1) * half] = nat(c * half, (c + 1) * half).astype(BF16)
        mg_ref[0, :, c * half:(c + 1) * half] = nat(4 * A_W + c * half, 4 * A_W + (c + 1) * half).astype(BF16)
    vc_ref[0] = nat(4 * A_W + 2 * D_MODEL, 4 * A_W + 2 * D_MODEL + B_KV_W)

    def tr(r0, r1):
        return _dot_nt(wt_ref[r0:r1, :], xn)

    ang = invf_ref[...] * pos_ref[0].astype(F32)
    cos = jnp.cos(ang)
    sin = jnp.sin(ang)

    def rope(xt, n_heads):
        pieces = []
        for h in range(n_heads):
            b0 = h * B_HD
            t1 = xt[b0:b0 + ROT_HALF]
            t2 = xt[b0 + ROT_HALF:b0 + ROT_DIM]
            pieces += [t1 * cos - t2 * sin, t2 * cos + t1 * sin, xt[b0 + ROT_DIM:b0 + B_HD]]
        return jnp.concatenate(pieces, axis=0)

    r_q, r_k, r_v = B_Q_W, B_Q_W + 3 * B_KV_W, B_Q_W + 5 * B_KV_W
    qt_ref[0] = (rope(tr(0, r_q), B_HEADS) * (B_HD ** -0.5 * LOG2_E)).astype(BF16)
    k3 = rope(tr(r_q, r_k), 3 * B_GROUPS).T
    kc_ref[0] = k3[:, 0:B_KV_W]
    kw_ref[0] = k3[:, 2 * B_KV_W:3 * B_KV_W].astype(BF16)
    tm = k3.shape[0]
    tok = pl.program_id(1) * tm + lax.broadcasted_iota(jnp.int32, (tm, SEL_ONEHOT_W), 0)
    lane = lax.broadcasted_iota(jnp.int32, (tm, SEL_ONEHOT_W), 1)
    ks_ref[0, :, 0:B_KV_W] = k3[:, B_KV_W:2 * B_KV_W].astype(BF16)
    ks_ref[0, :, B_KV_W:B_KV_W + SEL_ONEHOT_W] = jnp.where((tok >> SEL_SHIFT) == lane, 1.0, 0.0).astype(BF16)
    vt_ref[0] = tr(r_k, r_v).astype(BF16)
    gt_ref[0] = jax.nn.sigmoid(tr(r_v, r_v + B_GROUPS * GATE_ROWS))


def _proj(x, pos3, g1, invf, w_nat, w_t):
    bsz, s, _ = x.shape
    tm = min(512, s)
    n_nat = w_nat.shape[1]
    n_t = w_t.shape[0]
    tok = lambda w: pl.BlockSpec((1, tm, w), lambda b, t: (b, t, 0))
    trn = lambda r: pl.BlockSpec((1, r, tm), lambda b, t: (b, 0, t))
    const = lambda shp: pl.BlockSpec(shp, lambda b, t: (0,) * len(shp), pipeline_mode=pl.Buffered(1))
    out_shape = (
        jax.ShapeDtypeStruct((bsz, s, 4 * A_W), BF16),
        jax.ShapeDtypeStruct((bsz, s, 2 * D_MODEL), BF16),
        jax.ShapeDtypeStruct((bsz, s, B_KV_W), F32),
        jax.ShapeDtypeStruct((bsz, s, B_KV_W), F32),
        jax.ShapeDtypeStruct((bsz, s, KS_AUG_W), BF16),
        jax.ShapeDtypeStruct((bsz, s, B_KV_W), BF16),
        jax.ShapeDtypeStruct((bsz, B_Q_W, s), BF16),
        jax.ShapeDtypeStruct((bsz, 2 * B_KV_W, s), BF16),
        jax.ShapeDtypeStruct((bsz, B_GROUPS * GATE_ROWS, s), F32),
    )
    return pl.pallas_call(
        _proj_kernel,
        grid=(bsz, s // tm),
        in_specs=[tok(D_MODEL), trn(1), const((1, D_MODEL)), const((ROT_HALF, 1)),
                  const((D_MODEL, n_nat)), const((n_t, D_MODEL))],
        out_specs=(tok(4 * A_W), tok(2 * D_MODEL), tok(B_KV_W), tok(B_KV_W), tok(KS_AUG_W), tok(B_KV_W),
                   trn(B_Q_W), trn(2 * B_KV_W), trn(B_GROUPS * GATE_ROWS)),
        out_shape=out_shape,
        compiler_params=pltpu.CompilerParams(
            dimension_semantics=("parallel", "parallel"), vmem_limit_bytes=VMEM_LIMIT),
        name="proj",
    )(x, pos3, g1, invf, w_nat, w_t)


def _hgrn_kernel(hg_ref, lbp_ref, ng_ref, o_ref, st_ref, *, n_chunks):
    @pl.when(pl.program_id(1) == 0)
    def _():
        st_ref[...] = jnp.zeros_like(st_ref)

    lbp = lbp_ref[...]
    e = jnp.exp(lbp - jnp.max(lbp, axis=0, keepdims=True))
    lb = e[0:1] / jnp.sum(e, axis=0, keepdims=True)
    ng = ng_ref[...]
    row = lax.broadcasted_iota(jnp.int32, (A_CHUNK, A_CHUNK), 0)
    col = lax.broadcasted_iota(jnp.int32, (A_CHUNK, A_CHUNK), 1)
    causal = row >= col
    tri = causal.astype(F32)

    def chunk(c, carry):
        r0 = pl.multiple_of(c * A_CHUNK, A_CHUNK)
        blk = hg_ref[0, pl.ds(r0, A_CHUNK), :]
        q = blk[:, 0:A_W].astype(F32)
        fp = blk[:, A_W:2 * A_W].astype(F32)
        v = blk[:, 2 * A_W:3 * A_W]
        g = blk[:, 3 * A_W:4 * A_W].astype(F32)
        f = lb + (1.0 - lb) * jax.nn.sigmoid(fp)
        k = 1.0 - f
        b = jnp.dot(tri, jnp.log(f), precision=lax.Precision.HIGHEST,
                    preferred_element_type=F32)
        b_last = b[A_CHUNK - 1:A_CHUNK, :]
        q_dec = (q * jnp.exp(b)).astype(BF16)
        k_dec = (k * jnp.exp(-b)).astype(BF16)
        k_til = (k * jnp.exp(b_last - b)).astype(BF16)
        dec = jnp.exp(b_last)
        for h in range(A_HEADS):
            sl = slice(h * A_DK, (h + 1) * A_DK)
            attn = jnp.where(causal, _dot_nt(q_dec[:, sl], k_dec[:, sl]), 0.0).astype(BF16)
            st_t = st_ref[h]
            o = _dot(attn, v[:, sl]) + _dot_nt(q_dec[:, sl], st_t.astype(BF16))
            st_ref[h] = st_t * dec[:, sl] + _dot_tn(v[:, sl], k_til[:, sl])
            gh = g[:, sl]
            y = _rms(o, ng) * (gh * jax.nn.sigmoid(gh))
            o_ref[0, pl.ds(r0, A_CHUNK), sl] = y.astype(BF16)
        return carry

    lax.fori_loop(0, n_chunks, chunk, 0)


def _hgrn(hg, lb_param, norm_g):
    bsz, s, _ = hg.shape
    tt = min(512, s)
    return pl.pallas_call(
        functools.partial(_hgrn_kernel, n_chunks=tt // A_CHUNK),
        grid=(bsz, s // tt),
        in_specs=[pl.BlockSpec((1, tt, 4 * A_W), lambda b, t: (b, t, 0)),
                  pl.BlockSpec(lb_param.shape, lambda b, t: (0, 0)),
                  pl.BlockSpec((1, A_DV), lambda b, t: (0, 0))],
        out_specs=pl.BlockSpec((1, tt, A_W), lambda b, t: (b, t, 0)),
        out_shape=jax.ShapeDtypeStruct((bsz, s, A_W), BF16),
        scratch_shapes=[pltpu.VMEM((A_HEADS, A_DV, A_DK), F32)],
        compiler_params=pltpu.CompilerParams(
            dimension_semantics=("parallel", "arbitrary"), vmem_limit_bytes=VMEM_LIMIT),
        name="hgrn",
    )(hg, lb_param, norm_g)


def _compress_kernel(kt_ref, vt_ref, pek_ref, pev_ref, w1k_ref, w1v_ref, w2k_ref, w2vt_ref,
                     kc_ref, vct_ref, *, n_blk):
    half = CMP_LEN // 2

    def hidden(t_ref, pe_ref, w1_ref, g):
        acc = [jnp.zeros((n_blk, CMP_HIDDEN), F32), jnp.zeros((n_blk, CMP_HIDDEN), F32)]
        for l in range(CMP_LEN):
            rows = t_ref[0, pl.ds(l % half, n_blk, stride=CMP_STRIDE), :] + pe_ref[g, l:l + 1, :]
            acc[l // half] = acc[l // half] + _dot(rows.astype(BF16), w1_ref[g, l])
        pre = acc[0] + pltpu.roll(acc[1], n_blk - 1, 0)
        return jax.nn.gelu(pre).astype(BF16)

    valid_r = lax.broadcasted_iota(jnp.int32, (n_blk, B_KV_W), 0) < n_blk - 1
    valid_c = lax.broadcasted_iota(jnp.int32, (B_HD, n_blk), 1) < n_blk - 1
    kc = jnp.zeros((n_blk, B_KV_W), F32)
    for g in range(B_GROUPS):
        kc = kc + _dot(hidden(kt_ref, pek_ref, w1k_ref, g), w2k_ref[g])
        vct = _dot_nt(w2vt_ref[...], hidden(vt_ref, pev_ref, w1v_ref, g))
        vct_ref[0, g * B_HD:(g + 1) * B_HD, :] = jnp.where(valid_c, vct, 0.0).astype(BF16)
    kc_ref[0] = jnp.where(valid_r, kc, 0.0).astype(BF16)


def _compress(kc_tok, vc_tok, pek, pev, w1k, w1v, w2k, w2vt):
    bsz, s, _ = kc_tok.shape
    n_blk = s // CMP_STRIDE
    full = lambda a: pl.BlockSpec(a.shape, lambda b: (0,) * a.ndim)
    tokspec = pl.BlockSpec((1, s, B_KV_W), lambda b: (b, 0, 0))
    return pl.pallas_call(
        functools.partial(_compress_kernel, n_blk=n_blk),
        grid=(bsz,),
        in_specs=[tokspec, tokspec, full(pek), full(pev), full(w1k), full(w1v), full(w2k), full(w2vt)],
        out_specs=(pl.BlockSpec((1, n_blk, B_KV_W), lambda b: (b, 0, 0)),
                   pl.BlockSpec((1, B_KV_W, n_blk), lambda b: (b, 0, 0))),
        out_shape=(jax.ShapeDtypeStruct((bsz, n_blk, B_KV_W), BF16),
                   jax.ShapeDtypeStruct((bsz, B_KV_W, n_blk), BF16)),
        compiler_params=pltpu.CompilerParams(
            dimension_semantics=("parallel",), vmem_limit_bytes=VMEM_LIMIT),
        name="compress",
    )(kc_tok, vc_tok, pek, pev, w1k, w1v, w2k, w2vt)


KV_TILE = 512
WIN_KEYS = WINDOW + Q_BLOCK
QW = B_HPG * Q_BLOCK
LW = B_GROUPS * QW


def _block_rank(imp, n_slc):
    sub = lax.broadcasted_iota(jnp.int32, (8, Q_BLOCK), 0)
    rank = jnp.zeros((n_slc, Q_BLOCK), F32)
    for i in range(n_slc):
        ri = imp[i:i + 1, :]
        parts = []
        for r in range(n_slc // 8):
            blk = imp[8 * r:8 * r + 8, :]
            gt = jnp.where(ri > blk, 1.0, 0.0)
            ge = jnp.where(ri >= blk, 1.0, 0.0)
            if 8 * r + 7 <= i:
                parts.append(gt)
            elif 8 * r > i:
                parts.append(ge)
            else:
                parts.append(jnp.where(sub > i - 8 * r, ge, gt))
        rank = rank + jnp.concatenate(parts, axis=0)
    return rank


def _nsa_kernel(qt_ref, gt_ref, kc_ref, vct_ref, ks_ref, kw_ref, vt_ref, ovt_ref,
                o_ref, rhs_ref, sa_ref, sb_ref, *, n_cmp_rows, n_slc):
    qi = pl.program_id(1)
    q0 = qi * Q_BLOCK

    qblk = qt_ref[0]
    zq = jnp.zeros((B_HD, QW), BF16)
    qrows = []
    for g in range(B_GROUPS):
        qg = jnp.concatenate(
            [qblk[(g * B_HPG + h) * B_HD:(g * B_HPG + h + 1) * B_HD, :] for h in range(B_HPG)], axis=1)
        qrows.append(jnp.concatenate([qg if gg == g else zq for gg in range(B_GROUPS)], axis=1))
    qbd = jnp.concatenate(qrows, axis=0)

    def tq(rows):
        return lax.broadcasted_iota(jnp.int32, (rows, LW), 1) & (Q_BLOCK - 1)

    sc = _dot(kc_ref[0], qbd)
    crow = lax.broadcasted_iota(jnp.int32, (n_cmp_rows, LW), 0)
    okc = CMP_STRIDE * crow + (CMP_LEN - 1) <= q0 + tq(n_cmp_rows)
    sc = jnp.where(okc, sc, NEG)
    pc = jnp.where(okc, jnp.exp2(sc - jnp.max(sc, axis=0, keepdims=True)), 0.0)
    pc = pc / jnp.maximum(jnp.sum(pc, axis=0, keepdims=True), 1e-30)
    pcb = pc.astype(BF16)
    oc = [_dot(vct_ref[0, g * B_HD:(g + 1) * B_HD, :], pcb[:, g * QW:(g + 1) * QW]) for g in range(B_GROUPS)]

    jrow = lax.broadcasted_iota(jnp.int32, (n_slc, Q_BLOCK), 0)
    cur = (q0 + lax.broadcasted_iota(jnp.int32, (n_slc, Q_BLOCK), 1)) >> SEL_SHIFT
    causal_blk = jrow <= cur
    forced = (jrow == 0) | (jrow == cur) | (jrow == cur - 1)
    biases = []
    for g in range(B_GROUPS):
        psum = pc[:, g * QW:g * QW + Q_BLOCK]
        for h in range(1, B_HPG):
            psum = psum + pc[:, g * QW + h * Q_BLOCK:g * QW + (h + 1) * Q_BLOCK]
        p_hi = psum.astype(BF16)
        p_lo = (psum - p_hi.astype(F32)).astype(BF16)
        imp = _dot(ovt_ref[...], p_hi) + _dot(ovt_ref[...], p_lo)
        imp = jnp.where(forced & causal_blk, FORCE_SCORE, imp)
        imp = jnp.where(causal_blk, imp, NEG)
        sel = (_block_rank(imp, n_slc) < float(min(SEL_TOPK, n_slc))) & causal_blk
        sel = sel & (jrow // (Q_BLOCK // SEL_LEN) != qi)
        biases += [jnp.where(sel, 0.0, NEG).astype(BF16)] * B_HPG
    bias = jnp.concatenate(biases, axis=1)
    rhs_ref[...] = jnp.concatenate(
        [qbd, bias, jnp.zeros((KS_AUG_W - B_KV_W - n_slc, LW), BF16)], axis=0)

    def online_update(carry, s, v_of_group):
        m, l, accs = carry
        m_new = jnp.maximum(m, jnp.max(s, axis=0, keepdims=True))
        alpha = jnp.exp2(m - m_new)
        p = jnp.exp2(s - m_new)
        l = alpha * l + jnp.sum(p, axis=0, keepdims=True)
        pb = p.astype(BF16)
        accs = tuple(
            alpha[:, g * QW:(g + 1) * QW] * accs[g] + _dot(v_of_group(g), pb[:, g * QW:(g + 1) * QW])
            for g in range(B_GROUPS))
        return m_new, l, accs

    row_q = lax.broadcasted_iota(jnp.int32, (Q_BLOCK, LW), 0)
    tq_q = tq(Q_BLOCK)
    qs = pl.multiple_of(q0, Q_BLOCK)
    s_d = jnp.where(row_q <= tq_q, _dot(ks_ref[0, pl.ds(qs, Q_BLOCK), 0:B_KV_W], qbd), NEG)
    init = (jnp.full((1, LW), NEG, F32), jnp.zeros((1, LW), F32),
            tuple(jnp.zeros((B_HD, QW), F32) for _ in range(B_GROUPS)))
    carry = online_update(init, s_d, lambda g: vt_ref[0, g * B_HD:(g + 1) * B_HD, pl.ds(qs, Q_BLOCK)])

    def tile_scores(kt):
        k0 = pl.multiple_of(kt * KV_TILE, KV_TILE)
        return _dot(ks_ref[0, pl.ds(k0, KV_TILE), :], rhs_ref[...])

    def tile_update(kt, s, carry):
        k0 = pl.multiple_of(kt * KV_TILE, KV_TILE)
        return online_update(carry, s, lambda g: vt_ref[0, g * B_HD:(g + 1) * B_HD, pl.ds(k0, KV_TILE)])

    last = ks_ref.shape[1] // KV_TILE - 1
    sa_ref[...] = tile_scores(0)

    def tile_pair(i, carry):
        s0 = sa_ref[...]
        s1 = tile_scores(2 * i + 1)
        sa_ref[...] = tile_scores(jnp.minimum(2 * i + 2, last))
        return tile_update(2 * i + 1, s1, tile_update(2 * i, s0, carry))

    n_tiles = (q0 + KV_TILE - 1) // KV_TILE
    carry = lax.fori_loop(0, n_tiles // 2, tile_pair, carry)
    _, l_s, acc_s = lax.cond(
        n_tiles % 2 == 1, lambda c: tile_update(n_tiles - 1, sa_ref[...], c), lambda c: c, carry)

    ws = pl.multiple_of(jnp.maximum(q0 - WINDOW, 0), Q_BLOCK)

    def window(steady):
        sw = _dot(kw_ref[0, pl.ds(ws, WIN_KEYS), :], qbd)
        if steady:
            lo = jnp.where(row_q > tq_q, sw[0:Q_BLOCK], NEG)
            hi = jnp.where(row_q <= tq_q, sw[WINDOW:WIN_KEYS], NEG)
            sw = jnp.concatenate([lo, sw[Q_BLOCK:WINDOW], hi], axis=0)
        else:
            dist = (q0 - ws) + tq(WIN_KEYS) - lax.broadcasted_iota(jnp.int32, (WIN_KEYS, LW), 0)
            sw = jnp.where(lax.bitcast_convert_type(dist, jnp.uint32) < jnp.uint32(WINDOW), sw, NEG)
        pw = jnp.exp2(sw - jnp.max(sw, axis=0, keepdims=True))
        l_w = jnp.sum(pw, axis=0, keepdims=True)
        pwb = pw.astype(BF16)
        return tuple(
            _dot(vt_ref[0, (B_GROUPS + g) * B_HD:(B_GROUPS + g + 1) * B_HD, pl.ds(ws, WIN_KEYS)],
                 pwb[:, g * QW:(g + 1) * QW]) / l_w[:, g * QW:(g + 1) * QW]
            for g in range(B_GROUPS))

    o_w = lax.cond(q0 >= WINDOW, lambda: window(True), lambda: window(False))

    gts = gt_ref[0]
    for g in range(B_GROUPS):
        o_s = acc_s[g] / l_s[:, g * QW:(g + 1) * QW]
        for h in range(B_HPG):
            sl = slice(h * Q_BLOCK, (h + 1) * Q_BLOCK)
            r = g * GATE_ROWS + 3 * h
            o_h = gts[r:r + 1] * oc[g][:, sl] + gts[r + 1:r + 2] * o_s[:, sl] + gts[r + 2:r + 3] * o_w[g][:, sl]
            o_ref[0, (g * B_HPG + h) * B_HD:(g * B_HPG + h + 1) * B_HD, :] = o_h.astype(BF16)


def _nsa(qt, gt, kc, vct, ks, kw, vt, ovt):
    bsz, _, s = qt.shape
    n_cmp_rows = kc.shape[1]
    n_slc = s // SEL_LEN
    assert n_slc <= SEL_ONEHOT_W and s % KV_TILE == 0 and s >= WIN_KEYS
    whole = lambda a: pl.BlockSpec((1,) + a.shape[1:], lambda b, i: (b, 0, 0))
    return pl.pallas_call(
        functools.partial(_nsa_kernel, n_cmp_rows=n_cmp_rows, n_slc=n_slc),
        grid=(bsz, s // Q_BLOCK),
        in_specs=[
            pl.BlockSpec((1, B_Q_W, Q_BLOCK), lambda b, i: (b, 0, i)),
            pl.BlockSpec((1, B_GROUPS * GATE_ROWS, Q_BLOCK), lambda b, i: (b, 0, i)),
            whole(kc), whole(vct), whole(ks), whole(kw), whole(vt),
            pl.BlockSpec(ovt.shape, lambda b, i: (0, 0)),
        ],
        out_specs=pl.BlockSpec((1, B_Q_W, Q_BLOCK), lambda b, i: (b, 0, i)),
        out_shape=jax.ShapeDtypeStruct((bsz, B_Q_W, s), BF16),
        scratch_shapes=[pltpu.VMEM((KS_AUG_W, LW), BF16),
                        pltpu.VMEM((KV_TILE, LW), F32), pltpu.VMEM((KV_TILE, LW), F32)],
        compiler_params=pltpu.CompilerParams(
            dimension_semantics=("parallel", "arbitrary"), vmem_limit_bytes=VMEM_LIMIT),
        name="nsa",
    )(qt, gt, kc, vct, ks, kw, vt, ovt)


FF_CHUNK = 1024


def _tail_kernel(x_ref, ya_ref, ybt_ref, mg_ref, wa_ref, wb_ref, wo_ref, g2_ref, w1_ref, w2_ref, gf_ref, o_ref):
    pa = _dot(ya_ref[0], wa_ref[...])
    pb = _dot_tn(ybt_ref[0], wb_ref[...])
    mg = mg_ref[0].astype(F32)
    merged = jax.nn.sigmoid(mg[:, 0:D_MODEL]) * pa + jax.nn.sigmoid(mg[:, D_MODEL:2 * D_MODEL]) * pb
    h = x_ref[0] + _dot(merged.astype(BF16), wo_ref[...])
    hn = _rms(h, g2_ref[...]).astype(BF16)
    acc = jnp.zeros_like(h)
    for c in range(D_FF // FF_CHUNK):
        z = _dot(hn, w1_ref[:, c * FF_CHUNK:(c + 1) * FF_CHUNK])
        acc = acc + _dot(jnp.square(jnp.maximum(z, 0.0)).astype(BF16), w2_ref[c * FF_CHUNK:(c + 1) * FF_CHUNK, :])
    o_ref[0] = _rms(h + acc, gf_ref[...])


def _tail(x, ya, ybt, mg, wa, wb, wo, g2, w1, w2, gf):
    bsz, s, _ = x.shape
    tm = min(256, s)
    tok = lambda w: pl.BlockSpec((1, tm, w), lambda b, t: (b, t, 0))
    const = lambda a: pl.BlockSpec(a.shape, lambda b, t: (0,) * a.ndim, pipeline_mode=pl.Buffered(1))
    return pl.pallas_call(
        _tail_kernel,
        grid=(bsz, s // tm),
        in_specs=[tok(D_MODEL), tok(A_W), pl.BlockSpec((1, B_Q_W, tm), lambda b, t: (b, 0, t)), tok(2 * D_MODEL),
                  const(wa), const(wb), const(wo), const(g2), const(w1), const(w2), const(gf)],
        out_specs=tok(D_MODEL),
        out_shape=jax.ShapeDtypeStruct((bsz, s, D_MODEL), F32),
        compiler_params=pltpu.CompilerParams(
            dimension_semantics=("parallel", "parallel"), vmem_limit_bytes=VMEM_LIMIT),
        name="tail",
    )(x, ya, ybt, mg, wa, wb, wo, g2, w1, w2, gf)


def _pad_groups(a, axis):
    outs = []
    for g in range(B_GROUPS):
        pads = [(0, 0)] * a.ndim
        pads[axis] = (g * B_HD, (B_GROUPS - 1 - g) * B_HD)
        outs.append(jnp.pad(a, pads))
    return jnp.stack(outs)


def _overlap_t(s):
    n_cmp = (s - CMP_LEN) // CMP_STRIDE + 1
    n_slc = s // SEL_LEN
    cs = np.arange(n_cmp)[:, None] * CMP_STRIDE
    ss = np.arange(n_slc)[None, :] * SEL_LEN
    ov = np.clip(np.minimum(cs + CMP_LEN, ss + SEL_LEN) - np.maximum(cs, ss), 0, None) / CMP_LEN
    ovt = np.zeros((n_slc, s // CMP_STRIDE), np.float32)
    ovt[:, :n_cmp] = ov.T
    return jnp.asarray(ovt, BF16)


def kernel(x, positions, norm1_g, w_in, lb_param, hgrn_norm_g, cmp_pe_k, cmp_pe_v, cmp_w1_k, cmp_w2_k,
           cmp_w1_v, cmp_w2_v, w_br_a, w_br_b, w_out, norm2_g, w_ff1, w_ff2, final_g):
    bsz, s, _ = x.shape
    assert norm1_g.shape[0] == 1, "single-layer block"

    w = w_in[0]
    o_bq = 4 * A_W
    o_kv = o_bq + B_Q_W
    o_gate = o_kv + 6 * B_KV_W
    o_mg = o_gate + 3 * B_HEADS
    kv = lambda i: w[:, o_kv + i * B_KV_W:o_kv + (i + 1) * B_KV_W]
    w_nat = jnp.concatenate([w[:, 0:o_bq], w[:, o_mg:o_mg + 2 * D_MODEL], kv(1)], axis=1).astype(BF16)
    wg = w[:, o_gate:o_mg].reshape(D_MODEL, B_GROUPS, 3 * B_HPG)
    wg = jnp.pad(wg, ((0, 0), (0, 0), (0, GATE_ROWS - 3 * B_HPG))).reshape(D_MODEL, B_GROUPS * GATE_ROWS)
    w_t = jnp.concatenate([w[:, o_bq:o_kv], kv(0), kv(2), kv(4), kv(3), kv(5), wg], axis=1).T.astype(BF16)
    invf = (ROPE_THETA ** (-jnp.arange(0, ROT_DIM, 2, dtype=F32) / ROT_DIM)).reshape(ROT_HALF, 1)

    hg, mg, vc_tok, kc_tok, ks, kw, qt, vt, gt = _proj(
        x, positions.reshape(bsz, 1, s), norm1_g, invf, w_nat, w_t)

    ya = _hgrn(hg, lb_param, hgrn_norm_g)

    w1k = _pad_groups(cmp_w1_k[0].reshape(CMP_LEN, B_HD, CMP_HIDDEN), 1).astype(BF16)
    w1v = _pad_groups(cmp_w1_v[0].reshape(CMP_LEN, B_HD, CMP_HIDDEN), 1).astype(BF16)
    kc, vct = _compress(
        kc_tok, vc_tok, _pad_groups(cmp_pe_k[0], 1), _pad_groups(cmp_pe_v[0], 1), w1k, w1v,
        _pad_groups(cmp_w2_k[0], 1).astype(BF16), cmp_w2_v[0].T.astype(BF16))

    ybt = _nsa(qt, gt, kc, vct, ks, kw, vt, _overlap_t(s))

    return _tail(x, ya, ybt, mg, w_br_a[0].astype(BF16), w_br_b[0].astype(BF16), w_out[0].astype(BF16),
                 norm2_g, w_ff1[0].astype(BF16), w_ff2[0].astype(BF16), final_g.reshape(1, D_MODEL))
```

```python
import functools

import jax
import jax.numpy as jnp
import numpy as np
from jax import lax
from jax.experimental import pallas as pl
from jax.experimental.pallas import tpu as pltpu

F32 = jnp.float32
BF16 = jnp.bfloat16

D_MODEL = 1024
A_HEADS = 4
A_DK = 128
A_DV = 128
A_CHUNK = 64
A_W = A_HEADS * A_DK
B_HEADS = 8
B_GROUPS = 2
B_HPG = B_HEADS // B_GROUPS
B_HD = 64
B_Q_W = B_HEADS * B_HD
B_KV_W = B_GROUPS * B_HD
CMP_LEN = 32
CMP_STRIDE = 16
CMP_HIDDEN = 256
SEL_LEN = 64
SEL_TOPK = 16
WINDOW = 512
Q_BLOCK = 128
FORCE_SCORE = 1e4
NEG = -1e30
ROPE_THETA = 500000.0
ROT_DIM = B_HD // 4
ROT_HALF = ROT_DIM // 2
D_FF = 4 * D_MODEL
EPS = 1e-6
LOG2_E = 1.4426950408889634
GATE_ROWS = 16
SEL_SHIFT = SEL_LEN.bit_length() - 1
SEL_ONEHOT_W = 128
KS_AUG_W = B_KV_W + SEL_ONEHOT_W
V_AUX = 16
V_BLK = B_HD + V_AUX
VT_ROWS = 2 * B_GROUPS * V_BLK

VMEM_LIMIT = 56 * 1024 * 1024

NT = (((1,), (1,)), ((), ()))
TN = (((0,), (0,)), ((), ()))


def _dot(a, b):
    return jnp.dot(a, b, preferred_element_type=F32)


def _dot_nt(a, b):
    return lax.dot_general(a, b, NT, preferred_element_type=F32)


def _dot_tn(a, b):
    return lax.dot_general(a, b, TN, preferred_element_type=F32)


def _rms(x, g):
    return x * lax.rsqrt(jnp.mean(x * x, axis=-1, keepdims=True) + EPS) * g


def _proj_kernel(x_ref, pos_ref, g1_ref, invf_ref, wnat_ref, wt_ref,
                 hg_ref, mg_ref, vc_ref, kc_ref, ks_ref, kw_ref, qt_ref, vt_ref, gt_ref):
    xn = _rms(x_ref[0], g1_ref[...]).astype(BF16)

    def nat(c0, c1):
        return _dot(xn, wnat_ref[:, c0:c1])

    half = 2 * A_W
    for c in range(2):
        hg_ref[0, :, c * half:(c + 1) * half] = nat(c * half, (c + 1) * half).astype(BF16)
        mg_ref[0, :, c * half:(c + 1) * half] = nat(4 * A_W + c * half, 4 * A_W + (c + 1) * half).astype(BF16)
    vc_ref[0] = nat(4 * A_W + 2 * D_MODEL, 4 * A_W + 2 * D_MODEL + B_KV_W)

    def tr(r0, r1):
        return _dot_nt(wt_ref[r0:r1, :], xn)

    ang = invf_ref[...] * pos_ref[0].astype(F32)
    cos = jnp.cos(ang)
    sin = jnp.sin(ang)

    def rope(xt, n_heads):
        pieces = []
        for h in range(n_heads):
            b0 = h * B_HD
            t1 = xt[b0:b0 + ROT_HALF]
            t2 = xt[b0 + ROT_HALF:b0 + ROT_DIM]
            pieces += [t1 * cos - t2 * sin, t2 * cos + t1 * sin, xt[b0 + ROT_DIM:b0 + B_HD]]
        return jnp.concatenate(pieces, axis=0)

    r_q, r_k, r_v = B_Q_W, B_Q_W + 3 * B_KV_W, B_Q_W + 5 * B_KV_W
    qt_ref[0] = (rope(tr(0, r_q), B_HEADS) * (B_HD ** -0.5 * LOG2_E)).astype(BF16)
    k3 = rope(tr(r_q, r_k), 3 * B_GROUPS).T
    kc_ref[0] = k3[:, 0:B_KV_W]
    kw_ref[0] = k3[:, 2 * B_KV_W:3 * B_KV_W].astype(BF16)
    tm = k3.shape[0]
    tok = pl.program_id(1) * tm + lax.broadcasted_iota(jnp.int32, (tm, SEL_ONEHOT_W), 0)
    lane = lax.broadcasted_iota(jnp.int32, (tm, SEL_ONEHOT_W), 1)
    ks_ref[0, :, 0:B_KV_W] = k3[:, B_KV_W:2 * B_KV_W].astype(BF16)
    ks_ref[0, :, B_KV_W:B_KV_W + SEL_ONEHOT_W] = jnp.where((tok >> SEL_SHIFT) == lane, 1.0, 0.0).astype(BF16)
    vt = tr(r_k, r_v)
    aux = jnp.where(lax.broadcasted_iota(jnp.int32, (V_AUX, tm), 0) == 0, 1.0, 0.0)
    vt_ref[0] = jnp.concatenate(
        [piece for j in range(2 * B_GROUPS) for piece in (vt[j * B_HD:(j + 1) * B_HD], aux)], axis=0).astype(BF16)
    gt_ref[0] = jax.nn.sigmoid(tr(r_v, r_v + B_GROUPS * GATE_ROWS))


def _proj(x, pos3, g1, invf, w_nat, w_t):
    bsz, s, _ = x.shape
    tm = min(512, s)
    n_nat = w_nat.shape[1]
    n_t = w_t.shape[0]
    tok = lambda w: pl.BlockSpec((1, tm, w), lambda b, t: (b, t, 0))
    trn = lambda r: pl.BlockSpec((1, r, tm), lambda b, t: (b, 0, t))
    const = lambda shp: pl.BlockSpec(shp, lambda b, t: (0,) * len(shp), pipeline_mode=pl.Buffered(1))
    out_shape = (
        jax.ShapeDtypeStruct((bsz, s, 4 * A_W), BF16),
        jax.ShapeDtypeStruct((bsz, s, 2 * D_MODEL), BF16),
        jax.ShapeDtypeStruct((bsz, s, B_KV_W), F32),
        jax.ShapeDtypeStruct((bsz, s, B_KV_W), F32),
        jax.ShapeDtypeStruct((bsz, s, KS_AUG_W), BF16),
        jax.ShapeDtypeStruct((bsz, s, B_KV_W), BF16),
        jax.ShapeDtypeStruct((bsz, B_Q_W, s), BF16),
        jax.ShapeDtypeStruct((bsz, VT_ROWS, s), BF16),
        jax.ShapeDtypeStruct((bsz, B_GROUPS * GATE_ROWS, s), F32),
    )
    return pl.pallas_call(
        _proj_kernel,
        grid=(bsz, s // tm),
        in_specs=[tok(D_MODEL), trn(1), const((1, D_MODEL)), const((ROT_HALF, 1)),
                  const((D_MODEL, n_nat)), const((n_t, D_MODEL))],
        out_specs=(tok(4 * A_W), tok(2 * D_MODEL), tok(B_KV_W), tok(B_KV_W), tok(KS_AUG_W), tok(B_KV_W),
                   trn(B_Q_W), trn(VT_ROWS), trn(B_GROUPS * GATE_ROWS)),
        out_shape=out_shape,
        compiler_params=pltpu.CompilerParams(
            dimension_semantics=("parallel", "parallel"), vmem_limit_bytes=VMEM_LIMIT),
        name="proj",
    )(x, pos3, g1, invf, w_nat, w_t)


def _hgrn_kernel(hg_ref, lbp_ref, ng_ref, o_ref, st_ref, *, n_chunks):
    @pl.when(pl.program_id(1) == 0)
    def _():
        st_ref[...] = jnp.zeros_like(st_ref)

    lbp = lbp_ref[...]
    e = jnp.exp(lbp - jnp.max(lbp, axis=0, keepdims=True))
    lb = e[0:1] / jnp.sum(e, axis=0, keepdims=True)
    ng = ng_ref[...]
    row = lax.broadcasted_iota(jnp.int32, (A_CHUNK, A_CHUNK), 0)
    col = lax.broadcasted_iota(jnp.int32, (A_CHUNK, A_CHUNK), 1)
    causal = row >= col
    tri = jnp.where(causal, 1.0, 0.0).astype(BF16)

    def chunk(c, carry):
        r0 = pl.multiple_of(c * A_CHUNK, A_CHUNK)
        blk = hg_ref[0, pl.ds(r0, A_CHUNK), :]
        q = blk[:, 0:A_W].astype(F32)
        fp = blk[:, A_W:2 * A_W].astype(F32)
        v = blk[:, 2 * A_W:3 * A_W]
        g = blk[:, 3 * A_W:4 * A_W].astype(F32)
        f = lb + (1.0 - lb) * jax.nn.sigmoid(fp)
        k = 1.0 - f
        lf = jnp.log(f)
        lf_hi = lf.astype(BF16)
        r1 = lf - lf_hi.astype(F32)
        lf_mid = r1.astype(BF16)
        lf_lo = (r1 - lf_mid.astype(F32)).astype(BF16)
        b = _dot(tri, lf_hi) + _dot(tri, lf_mid) + _dot(tri, lf_lo)
        b_last = b[A_CHUNK - 1:A_CHUNK, :]
        q_dec = (q * jnp.exp(b)).astype(BF16)
        k_dec = (k * jnp.exp(-b)).astype(BF16)
        k_til = (k * jnp.exp(b_last - b)).astype(BF16)
        dec = jnp.exp(b_last)
        for h in range(A_HEADS):
            sl = slice(h * A_DK, (h + 1) * A_DK)
            attn = jnp.where(causal, _dot_nt(q_dec[:, sl], k_dec[:, sl]), 0.0).astype(BF16)
            st_t = st_ref[h]
            o = _dot(attn, v[:, sl]) + _dot_nt(q_dec[:, sl], st_t.astype(BF16))
            st_ref[h] = st_t * dec[:, sl] + _dot_tn(v[:, sl], k_til[:, sl])
            gh = g[:, sl]
            y = _rms(o, ng) * (gh * jax.nn.sigmoid(gh))
            o_ref[0, pl.ds(r0, A_CHUNK), sl] = y.astype(BF16)
        return carry

    lax.fori_loop(0, n_chunks, chunk, 0, unroll=True)


def _hgrn(hg, lb_param, norm_g):
    bsz, s, _ = hg.shape
    tt = min(512, s)
    return pl.pallas_call(
        functools.partial(_hgrn_kernel, n_chunks=tt // A_CHUNK),
        grid=(bsz, s // tt),
        in_specs=[pl.BlockSpec((1, tt, 4 * A_W), lambda b, t: (b, t, 0)),
                  pl.BlockSpec(lb_param.shape, lambda b, t: (0, 0)),
                  pl.BlockSpec((1, A_DV), lambda b, t: (0, 0))],
        out_specs=pl.BlockSpec((1, tt, A_W), lambda b, t: (b, t, 0)),
        out_shape=jax.ShapeDtypeStruct((bsz, s, A_W), BF16),
        scratch_shapes=[pltpu.VMEM((A_HEADS, A_DV, A_DK), F32)],
        compiler_params=pltpu.CompilerParams(
            dimension_semantics=("parallel", "arbitrary"), vmem_limit_bytes=VMEM_LIMIT),
        name="hgrn",
    )(hg, lb_param, norm_g)


def _compress_kernel(kt_ref, vt_ref, pek_ref, pev_ref, w1k_ref, w1v_ref, w2k_ref, w2vt_ref,
                     kc_ref, vct_ref, *, n_blk):
    half = CMP_LEN // 2

    def hidden(t_ref, pe_ref, w1_ref, g):
        acc = [jnp.zeros((n_blk, CMP_HIDDEN), F32), jnp.zeros((n_blk, CMP_HIDDEN), F32)]
        for l in range(CMP_LEN):
            rows = t_ref[0, pl.ds(l % half, n_blk, stride=CMP_STRIDE), :] + pe_ref[g, l:l + 1, :]
            acc[l // half] = acc[l // half] + _dot(rows.astype(BF16), w1_ref[g, l])
        pre = acc[0] + pltpu.roll(acc[1], n_blk - 1, 0)
        return jax.nn.gelu(pre).astype(BF16)

    valid_r = lax.broadcasted_iota(jnp.int32, (n_blk, B_KV_W), 0) < n_blk - 1
    valid_c = lax.broadcasted_iota(jnp.int32, (B_HD, n_blk), 1) < n_blk - 1
    kc = jnp.zeros((n_blk, B_KV_W), F32)
    for g in range(B_GROUPS):
        kc = kc + _dot(hidden(kt_ref, pek_ref, w1k_ref, g), w2k_ref[g])
        vct = _dot_nt(w2vt_ref[...], hidden(vt_ref, pev_ref, w1v_ref, g))
        vct_ref[0, g * B_HD:(g + 1) * B_HD, :] = jnp.where(valid_c, vct, 0.0).astype(BF16)
    kc_ref[0] = jnp.where(valid_r, kc, 0.0).astype(BF16)


def _compress(kc_tok, vc_tok, pek, pev, w1k, w1v, w2k, w2vt):
    bsz, s, _ = kc_tok.shape
    n_blk = s // CMP_STRIDE
    full = lambda a: pl.BlockSpec(a.shape, lambda b: (0,) * a.ndim)
    tokspec = pl.BlockSpec((1, s, B_KV_W), lambda b: (b, 0, 0))
    return pl.pallas_call(
        functools.partial(_compress_kernel, n_blk=n_blk),
        grid=(bsz,),
        in_specs=[tokspec, tokspec, full(pek), full(pev), full(w1k), full(w1v), full(w2k), full(w2vt)],
        out_specs=(pl.BlockSpec((1, n_blk, B_KV_W), lambda b: (b, 0, 0)),
                   pl.BlockSpec((1, B_KV_W, n_blk), lambda b: (b, 0, 0))),
        out_shape=(jax.ShapeDtypeStruct((bsz, n_blk, B_KV_W), BF16),
                   jax.ShapeDtypeStruct((bsz, B_KV_W, n_blk), BF16)),
        compiler_params=pltpu.CompilerParams(
            dimension_semantics=("parallel",), vmem_limit_bytes=VMEM_LIMIT),
        name="compress",
    )(kc_tok, vc_tok, pek, pev, w1k, w1v, w2k, w2vt)


KV_TILE = 512
WIN_KEYS = WINDOW + Q_BLOCK
QW = B_HPG * Q_BLOCK
LW = B_GROUPS * QW


def _block_rank(imp, n_slc):
    sub = lax.broadcasted_iota(jnp.int32, (8, Q_BLOCK), 0)
    rank = jnp.zeros((n_slc, Q_BLOCK), F32)
    for i in range(n_slc):
        ri = imp[i:i + 1, :]
        parts = []
        for r in range(n_slc // 8):
            blk = imp[8 * r:8 * r + 8, :]
            gt = jnp.where(ri > blk, 1.0, 0.0)
            ge = jnp.where(ri >= blk, 1.0, 0.0)
            if 8 * r + 7 <= i:
                parts.append(gt)
            elif 8 * r > i:
                parts.append(ge)
            else:
                parts.append(jnp.where(sub > i - 8 * r, ge, gt))
        rank = rank + jnp.concatenate(parts, axis=0)
    return rank


def _nsa_kernel(*refs, n_cmp_rows, n_slc, n_key_tiles):
    qi = pl.program_id(1)
    n_tiles = (qi * Q_BLOCK + KV_TILE - 1) // KV_TILE
    for c in range(n_key_tiles + 1):
        pl.when(n_tiles == c)(functools.partial(
            _nsa_step, c, qi, *refs, n_cmp_rows=n_cmp_rows, n_slc=n_slc))


def _nsa_step(n_tiles, qi, qt_ref, gt_ref, kc_ref, vct_ref, ks_ref, kw_ref, vt_ref, ovt_ref,
              o_ref, rhs_ref, *, n_cmp_rows, n_slc):
    q0 = qi * Q_BLOCK
    n_blk = min(n_slc, 16 * ((KV_TILE // SEL_LEN * n_tiles + Q_BLOCK // SEL_LEN + 15) // 16))
    n_cmp = min(n_cmp_rows, 128 * (((KV_TILE * n_tiles + Q_BLOCK) // CMP_STRIDE + 127) // 128))
    steady = KV_TILE * (n_tiles - 1) >= WINDOW

    qblk = qt_ref[0]
    zq = jnp.zeros((B_HD, QW), BF16)
    qrows = []
    for g in range(B_GROUPS):
        qg = jnp.concatenate(
            [qblk[(g * B_HPG + h) * B_HD:(g * B_HPG + h + 1) * B_HD, :] for h in range(B_HPG)], axis=1)
        qrows.append(jnp.concatenate([qg if gg == g else zq for gg in range(B_GROUPS)], axis=1))
    qbd = jnp.concatenate(qrows, axis=0)

    def tq(rows):
        return lax.broadcasted_iota(jnp.int32, (rows, LW), 1) & (Q_BLOCK - 1)

    qs = pl.multiple_of(q0, Q_BLOCK)
    ws = pl.multiple_of(jnp.maximum(q0 - WINDOW, 0), Q_BLOCK)
    sc = _dot(kc_ref[0, 0:n_cmp, :], qbd)
    sw = _dot(kw_ref[0, pl.ds(ws, WIN_KEYS), :], qbd)
    s_d = _dot(ks_ref[0, pl.ds(qs, Q_BLOCK), 0:B_KV_W], qbd)

    crow = lax.broadcasted_iota(jnp.int32, (n_cmp, LW), 0)
    okc = CMP_STRIDE * crow + (CMP_LEN - 1) <= q0 + tq(n_cmp)
    sc = jnp.where(okc, sc, NEG)
    pc = jnp.where(okc, jnp.exp2(sc - jnp.max(sc, axis=0, keepdims=True)), 0.0)
    pc = pc / jnp.maximum(jnp.sum(pc, axis=0, keepdims=True), 1e-30)
    pcb = pc.astype(BF16)
    oc = [_dot(vct_ref[0, g * B_HD:(g + 1) * B_HD, 0:n_cmp], pcb[:, g * QW:(g + 1) * QW])
          for g in range(B_GROUPS)]

    jrow = lax.broadcasted_iota(jnp.int32, (n_blk, Q_BLOCK), 0)
    cur = (q0 + lax.broadcasted_iota(jnp.int32, (n_blk, Q_BLOCK), 1)) >> SEL_SHIFT
    causal_blk = jrow <= cur
    forced = (jrow == 0) | (jrow == cur) | (jrow == cur - 1)
    ovt = ovt_ref[0:n_blk, 0:n_cmp]
    biases = []
    for g in range(B_GROUPS):
        psum = pc[:, g * QW:g * QW + Q_BLOCK]
        for h in range(1, B_HPG):
            psum = psum + pc[:, g * QW + h * Q_BLOCK:g * QW + (h + 1) * Q_BLOCK]
        p_hi = psum.astype(BF16)
        p_lo = (psum - p_hi.astype(F32)).astype(BF16)
        imp = _dot(ovt, p_hi) + _dot(ovt, p_lo)
        imp = jnp.where(forced & causal_blk, FORCE_SCORE, imp)
        imp = jnp.where(causal_blk, imp, NEG)
        sel = (_block_rank(imp, n_blk) < float(min(SEL_TOPK, n_slc))) & causal_blk
        sel = sel & (jrow // (Q_BLOCK // SEL_LEN) != qi)
        biases += [jnp.where(sel, 0.0, NEG).astype(BF16)] * B_HPG
    bias = jnp.concatenate(biases, axis=1)
    rhs_ref[...] = jnp.concatenate(
        [qbd, bias, jnp.zeros((KS_AUG_W - B_KV_W - n_blk, LW), BF16)], axis=0)

    def online_update(carry, s, v_of_group):
        m, accs = carry
        m_new = jnp.maximum(m, jnp.max(s, axis=0, keepdims=True))
        alpha = jnp.exp2(m - m_new)
        pb = jnp.exp2(s - m_new).astype(BF16)
        accs = tuple(
            alpha[:, g * QW:(g + 1) * QW] * accs[g] + _dot(v_of_group(g), pb[:, g * QW:(g + 1) * QW])
            for g in range(B_GROUPS))
        return m_new, accs

    def v_block(j, start, size):
        return vt_ref[0, j * V_BLK:(j + 1) * V_BLK, pl.ds(start, size)]

    row_q = lax.broadcasted_iota(jnp.int32, (Q_BLOCK, LW), 0)
    tq_q = tq(Q_BLOCK)
    carry = (jnp.full((1, LW), NEG, F32), tuple(jnp.zeros((V_BLK, QW), F32) for _ in range(B_GROUPS)))
    carry = online_update(carry, jnp.where(row_q <= tq_q, s_d, NEG), lambda g: v_block(g, qs, Q_BLOCK))

    def tile_scores(kt):
        return _dot(ks_ref[0, kt * KV_TILE:(kt + 1) * KV_TILE, :], rhs_ref[...])

    nxt = tile_scores(0) if n_tiles else None
    for kt in range(n_tiles):
        cur_s, nxt = nxt, (tile_scores(kt + 1) if kt + 1 < n_tiles else None)
        carry = online_update(carry, cur_s, lambda g, kt=kt: v_block(g, kt * KV_TILE, KV_TILE))
    acc_s = carry[1]

    if steady:
        lo = jnp.where(row_q > tq_q, sw[0:Q_BLOCK], NEG)
        hi = jnp.where(row_q <= tq_q, sw[WINDOW:WIN_KEYS], NEG)
        sw = jnp.concatenate([lo, sw[Q_BLOCK:WINDOW], hi], axis=0)
    else:
        dist = (q0 - ws) + tq(WIN_KEYS) - lax.broadcasted_iota(jnp.int32, (WIN_KEYS, LW), 0)
        sw = jnp.where(lax.bitcast_convert_type(dist, jnp.uint32) < jnp.uint32(WINDOW), sw, NEG)
    pwb = jnp.exp2(sw - jnp.max(sw, axis=0, keepdims=True)).astype(BF16)
    acc_w = [_dot(v_block(B_GROUPS + g, ws, WIN_KEYS), pwb[:, g * QW:(g + 1) * QW]) for g in range(B_GROUPS)]

    gts = gt_ref[0]
    for g in range(B_GROUPS):
        o_s = acc_s[g][0:B_HD] / acc_s[g][B_HD:B_HD + 1]
        o_w = acc_w[g][0:B_HD] / acc_w[g][B_HD:B_HD + 1]
        for h in range(B_HPG):
            sl = slice(h * Q_BLOCK, (h + 1) * Q_BLOCK)
            r = g * GATE_ROWS + 3 * h
            o_h = gts[r:r + 1] * oc[g][:, sl] + gts[r + 1:r + 2] * o_s[:, sl] + gts[r + 2:r + 3] * o_w[:, sl]
            o_ref[0, (g * B_HPG + h) * B_HD:(g * B_HPG + h + 1) * B_HD, :] = o_h.astype(BF16)


def _nsa(qt, gt, kc, vct, ks, kw, vt, ovt):
    bsz, _, s = qt.shape
    n_cmp_rows = kc.shape[1]
    n_slc = s // SEL_LEN
    assert n_slc <= SEL_ONEHOT_W and s % KV_TILE == 0 and s >= WIN_KEYS
    whole = lambda a: pl.BlockSpec((1,) + a.shape[1:], lambda b, i: (b, 0, 0))
    return pl.pallas_call(
        functools.partial(_nsa_kernel, n_cmp_rows=n_cmp_rows, n_slc=n_slc, n_key_tiles=s // KV_TILE),
        grid=(bsz, s // Q_BLOCK),
        in_specs=[
            pl.BlockSpec((1, B_Q_W, Q_BLOCK), lambda b, i: (b, 0, i)),
            pl.BlockSpec((1, B_GROUPS * GATE_ROWS, Q_BLOCK), lambda b, i: (b, 0, i)),
            whole(kc), whole(vct), whole(ks), whole(kw), whole(vt),
            pl.BlockSpec(ovt.shape, lambda b, i: (0, 0)),
        ],
        out_specs=pl.BlockSpec((1, B_Q_W, Q_BLOCK), lambda b, i: (b, 0, i)),
        out_shape=jax.ShapeDtypeStruct((bsz, B_Q_W, s), BF16),
        scratch_shapes=[pltpu.VMEM((KS_AUG_W, LW), BF16)],
        compiler_params=pltpu.CompilerParams(
            dimension_semantics=("parallel", "arbitrary"), vmem_limit_bytes=VMEM_LIMIT),
        name="nsa",
    )(qt, gt, kc, vct, ks, kw, vt, ovt)


FF_CHUNK = 1024


def _tail_kernel(x_ref, ya_ref, ybt_ref, mg_ref, wa_ref, wb_ref, wo_ref, g2_ref, w1_ref, w2_ref, gf_ref, o_ref):
    pa = _dot(ya_ref[0], wa_ref[...])
    pb = _dot_tn(ybt_ref[0], wb_ref[...])
    mg = mg_ref[0].astype(F32)
    merged = jax.nn.sigmoid(mg[:, 0:D_MODEL]) * pa + jax.nn.sigmoid(mg[:, D_MODEL:2 * D_MODEL]) * pb
    h = x_ref[0] + _dot(merged.astype(BF16), wo_ref[...])
    hn = _rms(h, g2_ref[...]).astype(BF16)
    acc = jnp.zeros_like(h)
    for c in range(D_FF // FF_CHUNK):
        z = _dot(hn, w1_ref[:, c * FF_CHUNK:(c + 1) * FF_CHUNK])
        acc = acc + _dot(jnp.square(jnp.maximum(z, 0.0)).astype(BF16), w2_ref[c * FF_CHUNK:(c + 1) * FF_CHUNK, :])
    o_ref[0] = _rms(h + acc, gf_ref[...])


def _tail(x, ya, ybt, mg, wa, wb, wo, g2, w1, w2, gf):
    bsz, s, _ = x.shape
    tm = min(256, s)
    tok = lambda w: pl.BlockSpec((1, tm, w), lambda b, t: (b, t, 0))
    const = lambda a: pl.BlockSpec(a.shape, lambda b, t: (0,) * a.ndim, pipeline_mode=pl.Buffered(1))
    return pl.pallas_call(
        _tail_kernel,
        grid=(bsz, s // tm),
        in_specs=[tok(D_MODEL), tok(A_W), pl.BlockSpec((1, B_Q_W, tm), lambda b, t: (b, 0, t)), tok(2 * D_MODEL),
                  const(wa), const(wb), const(wo), const(g2), const(w1), const(w2), const(gf)],
        out_specs=tok(D_MODEL),
        out_shape=jax.ShapeDtypeStruct((bsz, s, D_MODEL), F32),
        compiler_params=pltpu.CompilerParams(
            dimension_semantics=("parallel", "parallel"), vmem_limit_bytes=VMEM_LIMIT),
        name="tail",
    )(x, ya, ybt, mg, wa, wb, wo, g2, w1, w2, gf)


def _pad_groups(a, axis):
    outs = []
    for g in range(B_GROUPS):
        pads = [(0, 0)] * a.ndim
        pads[axis] = (g * B_HD, (B_GROUPS - 1 - g) * B_HD)
        outs.append(jnp.pad(a, pads))
    return jnp.stack(outs)


def _overlap_t(s):
    n_cmp = (s - CMP_LEN) // CMP_STRIDE + 1
    n_slc = s // SEL_LEN
    cs = np.arange(n_cmp)[:, None] * CMP_STRIDE
    ss = np.arange(n_slc)[None, :] * SEL_LEN
    ov = np.clip(np.minimum(cs + CMP_LEN, ss + SEL_LEN) - np.maximum(cs, ss), 0, None) / CMP_LEN
    ovt = np.zeros((n_slc, s // CMP_STRIDE), np.float32)
    ovt[:, :n_cmp] = ov.T
    return jnp.asarray(ovt, BF16)


def kernel(x, positions, norm1_g, w_in, lb_param, hgrn_norm_g, cmp_pe_k, cmp_pe_v, cmp_w1_k, cmp_w2_k,
           cmp_w1_v, cmp_w2_v, w_br_a, w_br_b, w_out, norm2_g, w_ff1, w_ff2, final_g):
    bsz, s, _ = x.shape
    assert norm1_g.shape[0] == 1, "single-layer block"

    w = w_in[0]
    o_bq = 4 * A_W
    o_kv = o_bq + B_Q_W
    o_gate = o_kv + 6 * B_KV_W
    o_mg = o_gate + 3 * B_HEADS
    kv = lambda i: w[:, o_kv + i * B_KV_W:o_kv + (i + 1) * B_KV_W]
    w_nat = jnp.concatenate([w[:, 0:o_bq], w[:, o_mg:o_mg + 2 * D_MODEL], kv(1)], axis=1).astype(BF16)
    wg = w[:, o_gate:o_mg].reshape(D_MODEL, B_GROUPS, 3 * B_HPG)
    wg = jnp.pad(wg, ((0, 0), (0, 0), (0, GATE_ROWS - 3 * B_HPG))).reshape(D_MODEL, B_GROUPS * GATE_ROWS)
    w_t = jnp.concatenate([w[:, o_bq:o_kv], kv(0), kv(2), kv(4), kv(3), kv(5), wg], axis=1).T.astype(BF16)
    invf = (ROPE_THETA ** (-jnp.arange(0, ROT_DIM, 2, dtype=F32) / ROT_DIM)).reshape(ROT_HALF, 1)

    hg, mg, vc_tok, kc_tok, ks, kw, qt, vt, gt = _proj(
        x, positions.reshape(bsz, 1, s), norm1_g, invf, w_nat, w_t)

    ya = _hgrn(hg, lb_param, hgrn_norm_g)

    w1k = _pad_groups(cmp_w1_k[0].reshape(CMP_LEN, B_HD, CMP_HIDDEN), 1).astype(BF16)
    w1v = _pad_groups(cmp_w1_v[0].reshape(CMP_LEN, B_HD, CMP_HIDDEN), 1).astype(BF16)
    kc, vct = _compress(
        kc_tok, vc_tok, _pad_groups(cmp_pe_k[0], 1), _pad_groups(cmp_pe_v[0], 1), w1k, w1v,
        _pad_groups(cmp_w2_k[0], 1).astype(BF16), cmp_w2_v[0].T.astype(BF16))

    ybt = _nsa(qt, gt, kc, vct, ks, kw, vt, _overlap_t(s))

    return _tail(x, ya, ybt, mg, w_br_a[0].astype(BF16), w_br_b[0].astype(BF16), w_out[0].astype(BF16),
                 norm2_g, w_ff1[0].astype(BF16), w_ff2[0].astype(BF16), final_g.reshape(1, D_MODEL))
```

```python
import functools

import jax
import jax.numpy as jnp
import numpy as np
from jax import lax
from jax.experimental import pallas as pl
from jax.experimental.pallas import tpu as pltpu

F32 = jnp.float32
BF16 = jnp.bfloat16

D_MODEL = 1024
A_HEADS = 4
A_DK = 128
A_DV = 128
A_CHUNK = 64
A_W = A_HEADS * A_DK
B_HEADS = 8
B_GROUPS = 2
B_HPG = B_HEADS // B_GROUPS
B_HD = 64
B_Q_W = B_HEADS * B_HD
B_KV_W = B_GROUPS * B_HD
CMP_LEN = 32
CMP_STRIDE = 16
CMP_HIDDEN = 256
SEL_LEN = 64
SEL_TOPK = 16
WINDOW = 512
Q_BLOCK = 128
FORCE_SCORE = 1e4
NEG = -1e30
ROPE_THETA = 500000.0
ROT_DIM = B_HD // 4
ROT_HALF = ROT_DIM // 2
D_FF = 4 * D_MODEL
EPS = 1e-6
LOG2_E = 1.4426950408889634
GATE_ROWS = 16
SEL_SHIFT = SEL_LEN.bit_length() - 1
SEL_ONEHOT_W = 128
KS_AUG_W = B_KV_W + SEL_ONEHOT_W
V_AUX = 16
V_BLK = B_HD + V_AUX
VT_ROWS = 2 * B_GROUPS * V_BLK

VMEM_LIMIT = 56 * 1024 * 1024

NT = (((1,), (1,)), ((), ()))
TN = (((0,), (0,)), ((), ()))


def _dot(a, b):
    return jnp.dot(a, b, preferred_element_type=F32)


def _dot_nt(a, b):
    return lax.dot_general(a, b, NT, preferred_element_type=F32)


def _dot_tn(a, b):
    return lax.dot_general(a, b, TN, preferred_element_type=F32)


def _rms(x, g):
    return x * lax.rsqrt(jnp.mean(x * x, axis=-1, keepdims=True) + EPS) * g


def _proj_kernel(x_ref, pos_ref, g1_ref, invf_ref, wnat_ref, wt_ref,
                 hg_ref, mg_ref, vc_ref, kc_ref, ks_ref, kw_ref, qt_ref, vt_ref, gt_ref):
    xn = _rms(x_ref[0], g1_ref[...]).astype(BF16)

    def nat(c0, c1):
        return _dot(xn, wnat_ref[:, c0:c1])

    half = 2 * A_W
    for c in range(2):
        hg_ref[0, :, c * half:(c + 1) * half] = nat(c * half, (c + 1) * half).astype(BF16)
        mg_ref[0, :, c * half:(c + 1) * half] = nat(4 * A_W + c * half, 4 * A_W + (c + 1) * half).astype(BF16)
    vc_ref[0] = nat(4 * A_W + 2 * D_MODEL, 4 * A_W + 2 * D_MODEL + B_KV_W)

    def tr(r0, r1):
        return _dot_nt(wt_ref[r0:r1, :], xn)

    ang = invf_ref[...] * pos_ref[0].astype(F32)
    cos = jnp.cos(ang)
    sin = jnp.sin(ang)

    def rope(xt, n_heads):
        pieces = []
        for h in range(n_heads):
            b0 = h * B_HD
            t1 = xt[b0:b0 + ROT_HALF]
            t2 = xt[b0 + ROT_HALF:b0 + ROT_DIM]
            pieces += [t1 * cos - t2 * sin, t2 * cos + t1 * sin, xt[b0 + ROT_DIM:b0 + B_HD]]
        return jnp.concatenate(pieces, axis=0)

    r_q, r_k, r_v = B_Q_W, B_Q_W + 3 * B_KV_W, B_Q_W + 5 * B_KV_W
    qt_ref[0] = (rope(tr(0, r_q), B_HEADS) * (B_HD ** -0.5 * LOG2_E)).astype(BF16)
    k3 = rope(tr(r_q, r_k), 3 * B_GROUPS).T
    kc_ref[0] = k3[:, 0:B_KV_W]
    kw_ref[0] = k3[:, 2 * B_KV_W:3 * B_KV_W].astype(BF16)
    tm = k3.shape[0]
    tok = pl.program_id(1) * tm + lax.broadcasted_iota(jnp.int32, (tm, SEL_ONEHOT_W), 0)
    lane = lax.broadcasted_iota(jnp.int32, (tm, SEL_ONEHOT_W), 1)
    ks_ref[0, :, 0:B_KV_W] = k3[:, B_KV_W:2 * B_KV_W].astype(BF16)
    ks_ref[0, :, B_KV_W:B_KV_W + SEL_ONEHOT_W] = jnp.where((tok >> SEL_SHIFT) == lane, 1.0, 0.0).astype(BF16)
    vt = tr(r_k, r_v)
    aux = jnp.where(lax.broadcasted_iota(jnp.int32, (V_AUX, tm), 0) == 0, 1.0, 0.0)
    vt_ref[0] = jnp.concatenate(
        [piece for j in range(2 * B_GROUPS) for piece in (vt[j * B_HD:(j + 1) * B_HD], aux)], axis=0).astype(BF16)
    gt_ref[0] = jax.nn.sigmoid(tr(r_v, r_v + B_GROUPS * GATE_ROWS))


def _proj(x, pos3, g1, invf, w_nat, w_t):
    bsz, s, _ = x.shape
    tm = min(512, s)
    n_nat = w_nat.shape[1]
    n_t = w_t.shape[0]
    tok = lambda w: pl.BlockSpec((1, tm, w), lambda b, t: (b, t, 0))
    trn = lambda r: pl.BlockSpec((1, r, tm), lambda b, t: (b, 0, t))
    const = lambda shp: pl.BlockSpec(shp, lambda b, t: (0,) * len(shp), pipeline_mode=pl.Buffered(1))
    out_shape = (
        jax.ShapeDtypeStruct((bsz, s, 4 * A_W), BF16),
        jax.ShapeDtypeStruct((bsz, s, 2 * D_MODEL), BF16),
        jax.ShapeDtypeStruct((bsz, s, B_KV_W), F32),
        jax.ShapeDtypeStruct((bsz, s, B_KV_W), F32),
        jax.ShapeDtypeStruct((bsz, s, KS_AUG_W), BF16),
        jax.ShapeDtypeStruct((bsz, s, B_KV_W), BF16),
        jax.ShapeDtypeStruct((bsz, B_Q_W, s), BF16),
        jax.ShapeDtypeStruct((bsz, VT_ROWS, s), BF16),
        jax.ShapeDtypeStruct((bsz, B_GROUPS * GATE_ROWS, s), F32),
    )
    return pl.pallas_call(
        _proj_kernel,
        grid=(bsz, s // tm),
        in_specs=[tok(D_MODEL), trn(1), const((1, D_MODEL)), const((ROT_HALF, 1)),
                  const((D_MODEL, n_nat)), const((n_t, D_MODEL))],
        out_specs=(tok(4 * A_W), tok(2 * D_MODEL), tok(B_KV_W), tok(B_KV_W), tok(KS_AUG_W), tok(B_KV_W),
                   trn(B_Q_W), trn(VT_ROWS), trn(B_GROUPS * GATE_ROWS)),
        out_shape=out_shape,
        compiler_params=pltpu.CompilerParams(
            dimension_semantics=("parallel", "parallel"), vmem_limit_bytes=VMEM_LIMIT),
        name="proj",
    )(x, pos3, g1, invf, w_nat, w_t)


def _hgrn_kernel(hg_ref, lbp_ref, ng_ref, o_ref, st_ref, *, n_chunks):
    @pl.when(pl.program_id(1) == 0)
    def _():
        st_ref[...] = jnp.zeros_like(st_ref)

    lbp = lbp_ref[...]
    e = jnp.exp(lbp - jnp.max(lbp, axis=0, keepdims=True))
    lb = e[0:1] / jnp.sum(e, axis=0, keepdims=True)
    ng = ng_ref[...]
    row = lax.broadcasted_iota(jnp.int32, (A_CHUNK, A_CHUNK), 0)
    col = lax.broadcasted_iota(jnp.int32, (A_CHUNK, A_CHUNK), 1)
    causal = row >= col
    tri = jnp.where(causal, 1.0, 0.0).astype(BF16)

    def chunk(c, carry):
        r0 = pl.multiple_of(c * A_CHUNK, A_CHUNK)
        blk = hg_ref[0, pl.ds(r0, A_CHUNK), :]
        q = blk[:, 0:A_W].astype(F32)
        fp = blk[:, A_W:2 * A_W].astype(F32)
        v = blk[:, 2 * A_W:3 * A_W]
        g = blk[:, 3 * A_W:4 * A_W].astype(F32)
        f = lb + (1.0 - lb) * jax.nn.sigmoid(fp)
        k = 1.0 - f
        lf = jnp.log(f)
        lf_hi = lf.astype(BF16)
        r1 = lf - lf_hi.astype(F32)
        lf_mid = r1.astype(BF16)
        lf_lo = (r1 - lf_mid.astype(F32)).astype(BF16)
        b = _dot(tri, lf_hi) + _dot(tri, lf_mid) + _dot(tri, lf_lo)
        b_last = b[A_CHUNK - 1:A_CHUNK, :]
        q_dec = (q * jnp.exp(b)).astype(BF16)
        k_dec = (k * jnp.exp(-b)).astype(BF16)
        k_til = (k * jnp.exp(b_last - b)).astype(BF16)
        dec = jnp.exp(b_last)
        for h in range(A_HEADS):
            sl = slice(h * A_DK, (h + 1) * A_DK)
            attn = jnp.where(causal, _dot_nt(q_dec[:, sl], k_dec[:, sl]), 0.0).astype(BF16)
            st_t = st_ref[h]
            o = _dot(attn, v[:, sl]) + _dot_nt(q_dec[:, sl], st_t.astype(BF16))
            st_ref[h] = st_t * dec[:, sl] + _dot_tn(v[:, sl], k_til[:, sl])
            gh = g[:, sl]
            y = _rms(o, ng) * (gh * jax.nn.sigmoid(gh))
            o_ref[0, pl.ds(r0, A_CHUNK), sl] = y.astype(BF16)
        return carry

    lax.fori_loop(0, n_chunks, chunk, 0, unroll=True)


def _hgrn(hg, lb_param, norm_g):
    bsz, s, _ = hg.shape
    tt = min(512, s)
    return pl.pallas_call(
        functools.partial(_hgrn_kernel, n_chunks=tt // A_CHUNK),
        grid=(bsz, s // tt),
        in_specs=[pl.BlockSpec((1, tt, 4 * A_W), lambda b, t: (b, t, 0)),
                  pl.BlockSpec(lb_param.shape, lambda b, t: (0, 0)),
                  pl.BlockSpec((1, A_DV), lambda b, t: (0, 0))],
        out_specs=pl.BlockSpec((1, tt, A_W), lambda b, t: (b, t, 0)),
        out_shape=jax.ShapeDtypeStruct((bsz, s, A_W), BF16),
        scratch_shapes=[pltpu.VMEM((A_HEADS, A_DV, A_DK), F32)],
        compiler_params=pltpu.CompilerParams(
            dimension_semantics=("parallel", "arbitrary"), vmem_limit_bytes=VMEM_LIMIT),
        name="hgrn",
    )(hg, lb_param, norm_g)


def _compress_kernel(kt_ref, vt_ref, pek_ref, pev_ref, w1k_ref, w1v_ref, w2k_ref, w2vt_ref,
                     kc_ref, vct_ref, *, n_blk):
    half = CMP_LEN // 2

    def hidden(t_ref, pe_ref, w1_ref, g):
        acc = [jnp.zeros((n_blk, CMP_HIDDEN), F32), jnp.zeros((n_blk, CMP_HIDDEN), F32)]
        for l in range(CMP_LEN):
            rows = t_ref[0, pl.ds(l % half, n_blk, stride=CMP_STRIDE), :] + pe_ref[g, l:l + 1, :]
            acc[l // half] = acc[l // half] + _dot(rows.astype(BF16), w1_ref[g, l])
        pre = acc[0] + pltpu.roll(acc[1], n_blk - 1, 0)
        return jax.nn.gelu(pre).astype(BF16)

    valid_r = lax.broadcasted_iota(jnp.int32, (n_blk, B_KV_W), 0) < n_blk - 1
    valid_c = lax.broadcasted_iota(jnp.int32, (B_HD, n_blk), 1) < n_blk - 1
    kc = jnp.zeros((n_blk, B_KV_W), F32)
    for g in range(B_GROUPS):
        kc = kc + _dot(hidden(kt_ref, pek_ref, w1k_ref, g), w2k_ref[g])
        vct = _dot_nt(w2vt_ref[...], hidden(vt_ref, pev_ref, w1v_ref, g))
        vct_ref[0, g * B_HD:(g + 1) * B_HD, :] = jnp.where(valid_c, vct, 0.0).astype(BF16)
    kc_ref[0] = jnp.where(valid_r, kc, 0.0).astype(BF16)


def _compress(kc_tok, vc_tok, pek, pev, w1k, w1v, w2k, w2vt):
    bsz, s, _ = kc_tok.shape
    n_blk = s // CMP_STRIDE
    full = lambda a: pl.BlockSpec(a.shape, lambda b: (0,) * a.ndim)
    tokspec = pl.BlockSpec((1, s, B_KV_W), lambda b: (b, 0, 0))
    return pl.pallas_call(
        functools.partial(_compress_kernel, n_blk=n_blk),
        grid=(bsz,),
        in_specs=[tokspec, tokspec, full(pek), full(pev), full(w1k), full(w1v), full(w2k), full(w2vt)],
        out_specs=(pl.BlockSpec((1, n_blk, B_KV_W), lambda b: (b, 0, 0)),
                   pl.BlockSpec((1, B_KV_W, n_blk), lambda b: (b, 0, 0))),
        out_shape=(jax.ShapeDtypeStruct((bsz, n_blk, B_KV_W), BF16),
                   jax.ShapeDtypeStruct((bsz, B_KV_W, n_blk), BF16)),
        compiler_params=pltpu.CompilerParams(
            dimension_semantics=("parallel",), vmem_limit_bytes=VMEM_LIMIT),
        name="compress",
    )(kc_tok, vc_tok, pek, pev, w1k, w1v, w2k, w2vt)


KV_TILE = 512
WIN_KEYS = WINDOW + Q_BLOCK
QW = B_HPG * Q_BLOCK
LW = B_GROUPS * QW
STRIP = 256
LOOKAHEAD = 4


def _block_rank(imp, n_slc, i_range):
    sub = lax.broadcasted_iota(jnp.int32, (8, Q_BLOCK), 0)
    rank = jnp.zeros((n_slc, Q_BLOCK), F32)
    for i in i_range:
        ri = imp[i:i + 1, :]
        parts = []
        for r in range(n_slc // 8):
            blk = imp[8 * r:8 * r + 8, :]
            gt = jnp.where(ri > blk, 1.0, 0.0)
            ge = jnp.where(ri >= blk, 1.0, 0.0)
            if 8 * r + 7 <= i:
                parts.append(gt)
            elif 8 * r > i:
                parts.append(ge)
            else:
                parts.append(jnp.where(sub > i - 8 * r, ge, gt))
        rank = rank + jnp.concatenate(parts, axis=0)
    return rank


def _nsa_kernel(*refs, n_cmp_rows, n_slc, n_key_tiles):
    qi = pl.program_id(1)
    n_tiles = (qi * Q_BLOCK + KV_TILE - 1) // KV_TILE
    for c in range(n_key_tiles + 1):
        pl.when(n_tiles == c)(functools.partial(
            _nsa_step, c, qi, *refs, n_cmp_rows=n_cmp_rows, n_slc=n_slc))


def _nsa_step(n_tiles, qi, qt_ref, gt_ref, kc_ref, vct_ref, ks_ref, kw_ref, vt_ref, ovt_ref,
              o_ref, rhs_ref, *, n_cmp_rows, n_slc):
    q0 = qi * Q_BLOCK
    n_blk = min(n_slc, 16 * ((KV_TILE // SEL_LEN * n_tiles + Q_BLOCK // SEL_LEN + 15) // 16))
    n_cmp = min(n_cmp_rows, 128 * (((KV_TILE * n_tiles + Q_BLOCK) // CMP_STRIDE + 127) // 128))
    steady = KV_TILE * (n_tiles - 1) >= WINDOW

    qblk = qt_ref[0]
    zq = jnp.zeros((B_HD, QW), BF16)
    qrows = []
    for g in range(B_GROUPS):
        qg = jnp.concatenate(
            [qblk[(g * B_HPG + h) * B_HD:(g * B_HPG + h + 1) * B_HD, :] for h in range(B_HPG)], axis=1)
        qrows.append(jnp.concatenate([qg if gg == g else zq for gg in range(B_GROUPS)], axis=1))
    qbd = jnp.concatenate(qrows, axis=0)

    def tq(rows):
        return lax.broadcasted_iota(jnp.int32, (rows, LW), 1) & (Q_BLOCK - 1)

    qs = pl.multiple_of(q0, Q_BLOCK)
    ws = pl.multiple_of(jnp.maximum(q0 - WINDOW, 0), Q_BLOCK)
    sc = _dot(kc_ref[0, 0:n_cmp, :], qbd)
    sw = _dot(kw_ref[0, pl.ds(ws, WIN_KEYS), :], qbd)
    s_d = _dot(ks_ref[0, pl.ds(qs, Q_BLOCK), 0:B_KV_W], qbd)

    crow = lax.broadcasted_iota(jnp.int32, (n_cmp, LW), 0)
    okc = CMP_STRIDE * crow + (CMP_LEN - 1) <= q0 + tq(n_cmp)
    sc = jnp.where(okc, sc, NEG)
    pc = jnp.where(okc, jnp.exp2(sc - jnp.max(sc, axis=0, keepdims=True)), 0.0)
    pc = pc / jnp.maximum(jnp.sum(pc, axis=0, keepdims=True), 1e-30)
    pcb = pc.astype(BF16)
    oc = [_dot(vct_ref[0, g * B_HD:(g + 1) * B_HD, 0:n_cmp], pcb[:, g * QW:(g + 1) * QW])
          for g in range(B_GROUPS)]

    def strip_update(state, s, v):
        m, acc = state
        m_new = jnp.maximum(m, jnp.max(s, axis=0, keepdims=True))
        pb = jnp.exp2(s - m_new).astype(BF16)
        return m_new, jnp.exp2(m - m_new) * acc + _dot(v, pb)

    def v_block(j, start, size):
        return vt_ref[0, j * V_BLK:(j + 1) * V_BLK, pl.ds(start, size)]

    n_str = LW // STRIP
    grp = lambda j: j * STRIP // QW
    lanes = lambda j: slice(j * STRIP, (j + 1) * STRIP)
    fresh = lambda: (jnp.full((1, STRIP), NEG, F32), jnp.zeros((V_BLK, STRIP), F32))
    row_q = lax.broadcasted_iota(jnp.int32, (Q_BLOCK, STRIP), 0)
    tq_q = lax.broadcasted_iota(jnp.int32, (Q_BLOCK, STRIP), 1) & (Q_BLOCK - 1)

    jrow = lax.broadcasted_iota(jnp.int32, (n_blk, Q_BLOCK), 0)
    cur = (q0 + lax.broadcasted_iota(jnp.int32, (n_blk, Q_BLOCK), 1)) >> SEL_SHIFT
    causal_blk = jrow <= cur
    forced = (jrow == 0) | (jrow == cur) | (jrow == cur - 1)
    ovt = ovt_ref[0:n_blk, 0:n_cmp]
    imps = []
    for g in range(B_GROUPS):
        psum = pc[:, g * QW:g * QW + Q_BLOCK]
        for h in range(1, B_HPG):
            psum = psum + pc[:, g * QW + h * Q_BLOCK:g * QW + (h + 1) * Q_BLOCK]
        p_hi = psum.astype(BF16)
        p_lo = (psum - p_hi.astype(F32)).astype(BF16)
        imp = _dot(ovt, p_hi) + _dot(ovt, p_lo)
        imp = jnp.where(forced & causal_blk, FORCE_SCORE, imp)
        imps.append(jnp.where(causal_blk, imp, NEG))

    biases = []
    for g in range(B_GROUPS):
        sel = (_block_rank(imps[g], n_blk, range(n_blk)) < float(min(SEL_TOPK, n_slc))) & causal_blk
        sel = sel & (jrow // (Q_BLOCK // SEL_LEN) != qi)
        biases += [jnp.where(sel, 0.0, NEG).astype(BF16)] * B_HPG
    bias = jnp.concatenate(biases, axis=1)
    rhs_ref[...] = jnp.concatenate(
        [qbd, bias, jnp.zeros((KS_AUG_W - B_KV_W - n_blk, LW), BF16)], axis=0)

    states = [
        strip_update(fresh(), jnp.where(row_q <= tq_q, s_d[:, lanes(j)], NEG), v_block(grp(j), qs, Q_BLOCK))
        for j in range(n_str)]

    order = [(kt, j) for kt in range(n_tiles) for j in range(n_str)]

    def strip_scores(kt, j):
        return _dot(ks_ref[0, kt * KV_TILE:(kt + 1) * KV_TILE, :], rhs_ref[:, lanes(j)])

    pending = [strip_scores(*order[i]) for i in range(min(LOOKAHEAD, len(order)))]
    for i, (kt, j) in enumerate(order):
        s = pending.pop(0)
        if i + LOOKAHEAD < len(order):
            pending.append(strip_scores(*order[i + LOOKAHEAD]))
        states[j] = strip_update(states[j], s, v_block(grp(j), kt * KV_TILE, KV_TILE))

    acc_w = []
    for j in range(n_str):
        swj = sw[:, lanes(j)]
        if steady:
            lo = jnp.where(row_q > tq_q, swj[0:Q_BLOCK], NEG)
            hi = jnp.where(row_q <= tq_q, swj[WINDOW:WIN_KEYS], NEG)
            swj = jnp.concatenate([lo, swj[Q_BLOCK:WINDOW], hi], axis=0)
        else:
            krow = lax.broadcasted_iota(jnp.int32, (WIN_KEYS, STRIP), 0)
            tqw = lax.broadcasted_iota(jnp.int32, (WIN_KEYS, STRIP), 1) & (Q_BLOCK - 1)
            dist = (q0 - ws) + tqw - krow
            swj = jnp.where(lax.bitcast_convert_type(dist, jnp.uint32) < jnp.uint32(WINDOW), swj, NEG)
        acc_w.append(strip_update(fresh(), swj, v_block(B_GROUPS + grp(j), ws, WIN_KEYS))[1])

    per_g = n_str // B_GROUPS
    acc_s = [jnp.concatenate([states[g * per_g + u][1] for u in range(per_g)], axis=1) for g in range(B_GROUPS)]
    acc_w = [jnp.concatenate(acc_w[g * per_g:(g + 1) * per_g], axis=1) for g in range(B_GROUPS)]

    gts = gt_ref[0]
    for g in range(B_GROUPS):
        o_s = acc_s[g][0:B_HD] / acc_s[g][B_HD:B_HD + 1]
        o_w = acc_w[g][0:B_HD] / acc_w[g][B_HD:B_HD + 1]
        for h in range(B_HPG):
            sl = slice(h * Q_BLOCK, (h + 1) * Q_BLOCK)
            r = g * GATE_ROWS + 3 * h
            o_h = gts[r:r + 1] * oc[g][:, sl] + gts[r + 1:r + 2] * o_s[:, sl] + gts[r + 2:r + 3] * o_w[:, sl]
            o_ref[0, (g * B_HPG + h) * B_HD:(g * B_HPG + h + 1) * B_HD, :] = o_h.astype(BF16)


def _nsa(qt, gt, kc, vct, ks, kw, vt, ovt):
    bsz, _, s = qt.shape
    n_cmp_rows = kc.shape[1]
    n_slc = s // SEL_LEN
    assert n_slc <= SEL_ONEHOT_W and s % KV_TILE == 0 and s >= WIN_KEYS
    whole = lambda a: pl.BlockSpec((1,) + a.shape[1:], lambda b, i: (b, 0, 0))
    return pl.pallas_call(
        functools.partial(_nsa_kernel, n_cmp_rows=n_cmp_rows, n_slc=n_slc, n_key_tiles=s // KV_TILE),
        grid=(bsz, s // Q_BLOCK),
        in_specs=[
            pl.BlockSpec((1, B_Q_W, Q_BLOCK), lambda b, i: (b, 0, i)),
            pl.BlockSpec((1, B_GROUPS * GATE_ROWS, Q_BLOCK), lambda b, i: (b, 0, i)),
            whole(kc), whole(vct), whole(ks), whole(kw), whole(vt),
            pl.BlockSpec(ovt.shape, lambda b, i: (0, 0)),
        ],
        out_specs=pl.BlockSpec((1, B_Q_W, Q_BLOCK), lambda b, i: (b, 0, i)),
        out_shape=jax.ShapeDtypeStruct((bsz, B_Q_W, s), BF16),
        scratch_shapes=[pltpu.VMEM((KS_AUG_W, LW), BF16)],
        compiler_params=pltpu.CompilerParams(
            dimension_semantics=("parallel", "arbitrary"), vmem_limit_bytes=VMEM_LIMIT),
        name="nsa",
    )(qt, gt, kc, vct, ks, kw, vt, ovt)


FF_CHUNK = 1024


def _tail_kernel(x_ref, ya_ref, ybt_ref, mg_ref, wa_ref, wb_ref, wo_ref, g2_ref, w1_ref, w2_ref, gf_ref, o_ref):
    pa = _dot(ya_ref[0], wa_ref[...])
    pb = _dot_tn(ybt_ref[0], wb_ref[...])
    mg = mg_ref[0].astype(F32)
    merged = jax.nn.sigmoid(mg[:, 0:D_MODEL]) * pa + jax.nn.sigmoid(mg[:, D_MODEL:2 * D_MODEL]) * pb
    h = x_ref[0] + _dot(merged.astype(BF16), wo_ref[...])
    hn = _rms(h, g2_ref[...]).astype(BF16)
    acc = jnp.zeros_like(h)
    for c in range(D_FF // FF_CHUNK):
        z = _dot(hn, w1_ref[:, c * FF_CHUNK:(c + 1) * FF_CHUNK])
        acc = acc + _dot(jnp.square(jnp.maximum(z, 0.0)).astype(BF16), w2_ref[c * FF_CHUNK:(c + 1) * FF_CHUNK, :])
    o_ref[0] = _rms(h + acc, gf_ref[...])


def _tail(x, ya, ybt, mg, wa, wb, wo, g2, w1, w2, gf):
    bsz, s, _ = x.shape
    tm = min(512, s)
    tok = lambda w: pl.BlockSpec((1, tm, w), lambda b, t: (b, t, 0))
    const = lambda a: pl.BlockSpec(a.shape, lambda b, t: (0,) * a.ndim, pipeline_mode=pl.Buffered(1))
    return pl.pallas_call(
        _tail_kernel,
        grid=(bsz, s // tm),
        in_specs=[tok(D_MODEL), tok(A_W), pl.BlockSpec((1, B_Q_W, tm), lambda b, t: (b, 0, t)), tok(2 * D_MODEL),
                  const(wa), const(wb), const(wo), const(g2), const(w1), const(w2), const(gf)],
        out_specs=tok(D_MODEL),
        out_shape=jax.ShapeDtypeStruct((bsz, s, D_MODEL), F32),
        compiler_params=pltpu.CompilerParams(
            dimension_semantics=("parallel", "parallel"), vmem_limit_bytes=VMEM_LIMIT),
        name="tail",
    )(x, ya, ybt, mg, wa, wb, wo, g2, w1, w2, gf)


def _pad_groups(a, axis):
    outs = []
    for g in range(B_GROUPS):
        pads = [(0, 0)] * a.ndim
        pads[axis] = (g * B_HD, (B_GROUPS - 1 - g) * B_HD)
        outs.append(jnp.pad(a, pads))
    return jnp.stack(outs)


def _overlap_t(s):
    n_cmp = (s - CMP_LEN) // CMP_STRIDE + 1
    n_slc = s // SEL_LEN
    cs = np.arange(n_cmp)[:, None] * CMP_STRIDE
    ss = np.arange(n_slc)[None, :] * SEL_LEN
    ov = np.clip(np.minimum(cs + CMP_LEN, ss + SEL_LEN) - np.maximum(cs, ss), 0, None) / CMP_LEN
    ovt = np.zeros((n_slc, s // CMP_STRIDE), np.float32)
    ovt[:, :n_cmp] = ov.T
    return jnp.asarray(ovt, BF16)


def kernel(x, positions, norm1_g, w_in, lb_param, hgrn_norm_g, cmp_pe_k, cmp_pe_v, cmp_w1_k, cmp_w2_k,
           cmp_w1_v, cmp_w2_v, w_br_a, w_br_b, w_out, norm2_g, w_ff1, w_ff2, final_g):
    bsz, s, _ = x.shape
    assert norm1_g.shape[0] == 1, "single-layer block"

    w = w_in[0]
    o_bq = 4 * A_W
    o_kv = o_bq + B_Q_W
    o_gate = o_kv + 6 * B_KV_W
    o_mg = o_gate + 3 * B_HEADS
    kv = lambda i: w[:, o_kv + i * B_KV_W:o_kv + (i + 1) * B_KV_W]
    w_nat = jnp.concatenate([w[:, 0:o_bq], w[:, o_mg:o_mg + 2 * D_MODEL], kv(1)], axis=1).astype(BF16)
    wg = w[:, o_gate:o_mg].reshape(D_MODEL, B_GROUPS, 3 * B_HPG)
    wg = jnp.pad(wg, ((0, 0), (0, 0), (0, GATE_ROWS - 3 * B_HPG))).reshape(D_MODEL, B_GROUPS * GATE_ROWS)
    w_t = jnp.concatenate([w[:, o_bq:o_kv], kv(0), kv(2), kv(4), kv(3), kv(5), wg], axis=1).T.astype(BF16)
    invf = (ROPE_THETA ** (-jnp.arange(0, ROT_DIM, 2, dtype=F32) / ROT_DIM)).reshape(ROT_HALF, 1)

    hg, mg, vc_tok, kc_tok, ks, kw, qt, vt, gt = _proj(
        x, positions.reshape(bsz, 1, s), norm1_g, invf, w_nat, w_t)

    ya = _hgrn(hg, lb_param, hgrn_norm_g)

    w1k = _pad_groups(cmp_w1_k[0].reshape(CMP_LEN, B_HD, CMP_HIDDEN), 1).astype(BF16)
    w1v = _pad_groups(cmp_w1_v[0].reshape(CMP_LEN, B_HD, CMP_HIDDEN), 1).astype(BF16)
    kc, vct = _compress(
        kc_tok, vc_tok, _pad_groups(cmp_pe_k[0], 1), _pad_groups(cmp_pe_v[0], 1), w1k, w1v,
        _pad_groups(cmp_w2_k[0], 1).astype(BF16), cmp_w2_v[0].T.astype(BF16))

    ybt = _nsa(qt, gt, kc, vct, ks, kw, vt, _overlap_t(s))

    return _tail(x, ya, ybt, mg, w_br_a[0].astype(BF16), w_br_b[0].astype(BF16), w_out[0].astype(BF16),
                 norm2_g, w_ff1[0].astype(BF16), w_ff2[0].astype(BF16), final_g.reshape(1, D_MODEL))
```

```python
import functools

import jax
import jax.numpy as jnp
import numpy as np
from jax import lax
from jax.experimental import pallas as pl
from jax.experimental.pallas import tpu as pltpu

F32 = jnp.float32
BF16 = jnp.bfloat16

D_MODEL = 1024
A_HEADS = 4
A_DK = 128
A_DV = 128
A_CHUNK = 64
A_W = A_HEADS * A_DK
B_HEADS = 8
B_GROUPS = 2
B_HPG = B_HEADS // B_GROUPS
B_HD = 64
B_Q_W = B_HEADS * B_HD
B_KV_W = B_GROUPS * B_HD
CMP_LEN = 32
CMP_STRIDE = 16
CMP_HIDDEN = 256
SEL_LEN = 64
SEL_TOPK = 16
WINDOW = 512
Q_BLOCK = 128
FORCE_SCORE = 1e4
NEG = -1e30
ROPE_THETA = 500000.0
ROT_DIM = B_HD // 4
ROT_HALF = ROT_DIM // 2
D_FF = 4 * D_MODEL
EPS = 1e-6
LOG2_E = 1.4426950408889634
GATE_ROWS = 16
SEL_SHIFT = SEL_LEN.bit_length() - 1
SEL_ONEHOT_W = 128
KS_AUG_W = B_KV_W + SEL_ONEHOT_W
V_AUX = 16
V_BLK = B_HD + V_AUX
VT_ROWS = 2 * B_GROUPS * V_BLK

VMEM_LIMIT = 56 * 1024 * 1024

NT = (((1,), (1,)), ((), ()))
TN = (((0,), (0,)), ((), ()))


def _dot(a, b):
    return jnp.dot(a, b, preferred_element_type=F32)


def _dot_nt(a, b):
    return lax.dot_general(a, b, NT, preferred_element_type=F32)


def _dot_tn(a, b):
    return lax.dot_general(a, b, TN, preferred_element_type=F32)


def _rms(x, g):
    return x * lax.rsqrt(jnp.mean(x * x, axis=-1, keepdims=True) + EPS) * g


def _proj_kernel(x_ref, pos_ref, g1_ref, invf_ref, wnat_ref, wt_ref,
                 hg_ref, mg_ref, vc_ref, kc_ref, ks_ref, kw_ref, qt_ref, vt_ref, gt_ref):
    xn = _rms(x_ref[0], g1_ref[...]).astype(BF16)

    def nat(c0, c1):
        return _dot(xn, wnat_ref[:, c0:c1])

    half = 2 * A_W
    for c in range(2):
        hg_ref[0, :, c * half:(c + 1) * half] = nat(c * half, (c + 1) * half).astype(BF16)
        mg_ref[0, :, c * half:(c + 1) * half] = nat(4 * A_W + c * half, 4 * A_W + (c + 1) * half).astype(BF16)
    vc_ref[0] = nat(4 * A_W + 2 * D_MODEL, 4 * A_W + 2 * D_MODEL + B_KV_W)

    def tr(r0, r1):
        return _dot_nt(wt_ref[r0:r1, :], xn)

    ang = invf_ref[...] * pos_ref[0].astype(F32)
    cos = jnp.cos(ang)
    sin = jnp.sin(ang)

    def rope(xt, n_heads):
        pieces = []
        for h in range(n_heads):
            b0 = h * B_HD
            t1 = xt[b0:b0 + ROT_HALF]
            t2 = xt[b0 + ROT_HALF:b0 + ROT_DIM]
            pieces += [t1 * cos - t2 * sin, t2 * cos + t1 * sin, xt[b0 + ROT_DIM:b0 + B_HD]]
        return jnp.concatenate(pieces, axis=0)

    r_q, r_k, r_v = B_Q_W, B_Q_W + 3 * B_KV_W, B_Q_W + 5 * B_KV_W
    qt_ref[0] = (rope(tr(0, r_q), B_HEADS) * (B_HD ** -0.5 * LOG2_E)).astype(BF16)
    k3 = rope(tr(r_q, r_k), 3 * B_GROUPS).T
    kc_ref[0] = k3[:, 0:B_KV_W]
    kw_ref[0] = k3[:, 2 * B_KV_W:3 * B_KV_W].astype(BF16)
    tm = k3.shape[0]
    tok = pl.program_id(1) * tm + lax.broadcasted_iota(jnp.int32, (tm, SEL_ONEHOT_W), 0)
    lane = lax.broadcasted_iota(jnp.int32, (tm, SEL_ONEHOT_W), 1)
    ks_ref[0, :, 0:B_KV_W] = k3[:, B_KV_W:2 * B_KV_W].astype(BF16)
    ks_ref[0, :, B_KV_W:B_KV_W + SEL_ONEHOT_W] = jnp.where((tok >> SEL_SHIFT) == lane, 1.0, 0.0).astype(BF16)
    vt = tr(r_k, r_v)
    aux = jnp.where(lax.broadcasted_iota(jnp.int32, (V_AUX, tm), 0) == 0, 1.0, 0.0)
    vt_ref[0] = jnp.concatenate(
        [piece for j in range(2 * B_GROUPS) for piece in (vt[j * B_HD:(j + 1) * B_HD], aux)], axis=0).astype(BF16)
    gt_ref[0] = jax.nn.sigmoid(tr(r_v, r_v + B_GROUPS * GATE_ROWS))


def _proj(x, pos3, g1, invf, w_nat, w_t):
    bsz, s, _ = x.shape
    tm = min(512, s)
    n_nat = w_nat.shape[1]
    n_t = w_t.shape[0]
    tok = lambda w: pl.BlockSpec((1, tm, w), lambda b, t: (b, t, 0))
    trn = lambda r: pl.BlockSpec((1, r, tm), lambda b, t: (b, 0, t))
    const = lambda shp: pl.BlockSpec(shp, lambda b, t: (0,) * len(shp), pipeline_mode=pl.Buffered(1))
    out_shape = (
        jax.ShapeDtypeStruct((bsz, s, 4 * A_W), BF16),
        jax.ShapeDtypeStruct((bsz, s, 2 * D_MODEL), BF16),
        jax.ShapeDtypeStruct((bsz, s, B_KV_W), F32),
        jax.ShapeDtypeStruct((bsz, s, B_KV_W), F32),
        jax.ShapeDtypeStruct((bsz, s, KS_AUG_W), BF16),
        jax.ShapeDtypeStruct((bsz, s, B_KV_W), BF16),
        jax.ShapeDtypeStruct((bsz, B_Q_W, s), BF16),
        jax.ShapeDtypeStruct((bsz, VT_ROWS, s), BF16),
        jax.ShapeDtypeStruct((bsz, B_GROUPS * GATE_ROWS, s), F32),
    )
    return pl.pallas_call(
        _proj_kernel,
        grid=(bsz, s // tm),
        in_specs=[tok(D_MODEL), trn(1), const((1, D_MODEL)), const((ROT_HALF, 1)),
                  const((D_MODEL, n_nat)), const((n_t, D_MODEL))],
        out_specs=(tok(4 * A_W), tok(2 * D_MODEL), tok(B_KV_W), tok(B_KV_W), tok(KS_AUG_W), tok(B_KV_W),
                   trn(B_Q_W), trn(VT_ROWS), trn(B_GROUPS * GATE_ROWS)),
        out_shape=out_shape,
        compiler_params=pltpu.CompilerParams(
            dimension_semantics=("parallel", "parallel"), vmem_limit_bytes=VMEM_LIMIT),
        name="proj",
    )(x, pos3, g1, invf, w_nat, w_t)


def _hgrn_kernel(hg_ref, lbp_ref, ng_ref, o_ref, st_ref, *, n_chunks):
    @pl.when(pl.program_id(1) == 0)
    def _():
        st_ref[...] = jnp.zeros_like(st_ref)

    lbp = lbp_ref[...]
    e = jnp.exp(lbp - jnp.max(lbp, axis=0, keepdims=True))
    lb = e[0:1] / jnp.sum(e, axis=0, keepdims=True)
    ng = ng_ref[...]
    row = lax.broadcasted_iota(jnp.int32, (A_CHUNK, A_CHUNK), 0)
    col = lax.broadcasted_iota(jnp.int32, (A_CHUNK, A_CHUNK), 1)
    causal = row >= col
    tri = jnp.where(causal, 1.0, 0.0).astype(BF16)

    heads = [slice(h * A_DK, (h + 1) * A_DK) for h in range(A_HEADS)]

    def stage1(c):
        blk = hg_ref[0, c * A_CHUNK:(c + 1) * A_CHUNK, :]
        f = lb + (1.0 - lb) * jax.nn.sigmoid(blk[:, A_W:2 * A_W].astype(F32))
        lf = jnp.log(f)
        lf_hi = lf.astype(BF16)
        r1 = lf - lf_hi.astype(F32)
        lf_mid = r1.astype(BF16)
        lf_lo = (r1 - lf_mid.astype(F32)).astype(BF16)
        b = _dot(tri, lf_hi) + _dot(tri, lf_mid) + _dot(tri, lf_lo)
        return dict(c=c, k=1.0 - f, b=b)

    def stage2(s):
        c, k, b = s["c"], s["k"], s["b"]
        blk = hg_ref[0, c * A_CHUNK:(c + 1) * A_CHUNK, :]
        b_last = b[A_CHUNK - 1:A_CHUNK, :]
        return dict(c=c, v=blk[:, 2 * A_W:3 * A_W],
                    q_dec=(blk[:, 0:A_W].astype(F32) * jnp.exp(b)).astype(BF16),
                    k_dec=(k * jnp.exp(-b)).astype(BF16),
                    k_til=(k * jnp.exp(b_last - b)).astype(BF16),
                    dec=jnp.exp(b_last))

    def stage3(s):
        s = dict(s)
        attn = [jnp.where(causal, _dot_nt(s["q_dec"][:, sl], s["k_dec"][:, sl]), 0.0).astype(BF16)
                for sl in heads]
        s["o_intra"] = [_dot(attn[h], s["v"][:, sl]) for h, sl in enumerate(heads)]
        s["kv_t"] = [_dot_tn(s["v"][:, sl], s["k_til"][:, sl]) for sl in heads]
        return s

    def stage4(s, state):
        c = s["c"]
        g = hg_ref[0, c * A_CHUNK:(c + 1) * A_CHUNK, 3 * A_W:4 * A_W].astype(F32)
        new_state = []
        for h, sl in enumerate(heads):
            o = s["o_intra"][h] + _dot_nt(s["q_dec"][:, sl], state[h].astype(BF16))
            new_state.append(state[h] * s["dec"][:, sl] + s["kv_t"][h])
            gh = g[:, sl]
            y = _rms(o, ng) * (gh * jax.nn.sigmoid(gh))
            o_ref[0, c * A_CHUNK:(c + 1) * A_CHUNK, sl] = y.astype(BF16)
        return new_state

    state = [st_ref[h] for h in range(A_HEADS)]
    pipe = [None, None, None]
    for t in range(n_chunks + 3):
        if pipe[2] is not None:
            state = stage4(pipe[2], state)
        pipe[2] = stage3(pipe[1]) if pipe[1] is not None else None
        pipe[1] = stage2(pipe[0]) if pipe[0] is not None else None
        pipe[0] = stage1(t) if t < n_chunks else None
    for h in range(A_HEADS):
        st_ref[h] = state[h]


def _hgrn(hg, lb_param, norm_g):
    bsz, s, _ = hg.shape
    tt = min(512, s)
    return pl.pallas_call(
        functools.partial(_hgrn_kernel, n_chunks=tt // A_CHUNK),
        grid=(bsz, s // tt),
        in_specs=[pl.BlockSpec((1, tt, 4 * A_W), lambda b, t: (b, t, 0)),
                  pl.BlockSpec(lb_param.shape, lambda b, t: (0, 0)),
                  pl.BlockSpec((1, A_DV), lambda b, t: (0, 0))],
        out_specs=pl.BlockSpec((1, tt, A_W), lambda b, t: (b, t, 0)),
        out_shape=jax.ShapeDtypeStruct((bsz, s, A_W), BF16),
        scratch_shapes=[pltpu.VMEM((A_HEADS, A_DV, A_DK), F32)],
        compiler_params=pltpu.CompilerParams(
            dimension_semantics=("parallel", "arbitrary"), vmem_limit_bytes=VMEM_LIMIT),
        name="hgrn",
    )(hg, lb_param, norm_g)


def _compress_kernel(kt_ref, vt_ref, pek_ref, pev_ref, w1k_ref, w1v_ref, w2k_ref, w2vt_ref,
                     kc_ref, vct_ref, *, n_blk):
    half = CMP_LEN // 2

    def hidden(t_ref, pe_ref, w1_ref, g):
        acc = [jnp.zeros((n_blk, CMP_HIDDEN), F32), jnp.zeros((n_blk, CMP_HIDDEN), F32)]
        for l in range(CMP_LEN):
            rows = t_ref[0, pl.ds(l % half, n_blk, stride=CMP_STRIDE), :] + pe_ref[g, l:l + 1, :]
            acc[l // half] = acc[l // half] + _dot(rows.astype(BF16), w1_ref[g, l])
        pre = acc[0] + pltpu.roll(acc[1], n_blk - 1, 0)
        return jax.nn.gelu(pre).astype(BF16)

    valid_r = lax.broadcasted_iota(jnp.int32, (n_blk, B_KV_W), 0) < n_blk - 1
    valid_c = lax.broadcasted_iota(jnp.int32, (B_HD, n_blk), 1) < n_blk - 1
    kc = jnp.zeros((n_blk, B_KV_W), F32)
    for g in range(B_GROUPS):
        kc = kc + _dot(hidden(kt_ref, pek_ref, w1k_ref, g), w2k_ref[g])
        vct = _dot_nt(w2vt_ref[...], hidden(vt_ref, pev_ref, w1v_ref, g))
        vct_ref[0, g * B_HD:(g + 1) * B_HD, :] = jnp.where(valid_c, vct, 0.0).astype(BF16)
    kc_ref[0] = jnp.where(valid_r, kc, 0.0).astype(BF16)


def _compress(kc_tok, vc_tok, pek, pev, w1k, w1v, w2k, w2vt):
    bsz, s, _ = kc_tok.shape
    n_blk = s // CMP_STRIDE
    full = lambda a: pl.BlockSpec(a.shape, lambda b: (0,) * a.ndim)
    tokspec = pl.BlockSpec((1, s, B_KV_W), lambda b: (b, 0, 0))
    return pl.pallas_call(
        functools.partial(_compress_kernel, n_blk=n_blk),
        grid=(bsz,),
        in_specs=[tokspec, tokspec, full(pek), full(pev), full(w1k), full(w1v), full(w2k), full(w2vt)],
        out_specs=(pl.BlockSpec((1, n_blk, B_KV_W), lambda b: (b, 0, 0)),
                   pl.BlockSpec((1, B_KV_W, n_blk), lambda b: (b, 0, 0))),
        out_shape=(jax.ShapeDtypeStruct((bsz, n_blk, B_KV_W), BF16),
                   jax.ShapeDtypeStruct((bsz, B_KV_W, n_blk), BF16)),
        compiler_params=pltpu.CompilerParams(
            dimension_semantics=("parallel",), vmem_limit_bytes=VMEM_LIMIT),
        name="compress",
    )(kc_tok, vc_tok, pek, pev, w1k, w1v, w2k, w2vt)


KV_TILE = 512
WIN_KEYS = WINDOW + Q_BLOCK
QW = B_HPG * Q_BLOCK
LW = B_GROUPS * QW
STRIP = 256
LOOKAHEAD = 4


def _block_rank(imp, n_slc, i_range):
    sub = lax.broadcasted_iota(jnp.int32, (8, Q_BLOCK), 0)
    rank = jnp.zeros((n_slc, Q_BLOCK), F32)
    for i in i_range:
        ri = imp[i:i + 1, :]
        parts = []
        for r in range(n_slc // 8):
            blk = imp[8 * r:8 * r + 8, :]
            gt = jnp.where(ri > blk, 1.0, 0.0)
            ge = jnp.where(ri >= blk, 1.0, 0.0)
            if 8 * r + 7 <= i:
                parts.append(gt)
            elif 8 * r > i:
                parts.append(ge)
            else:
                parts.append(jnp.where(sub > i - 8 * r, ge, gt))
        rank = rank + jnp.concatenate(parts, axis=0)
    return rank


def _nsa_kernel(*refs, n_cmp_rows, n_slc, n_key_tiles):
    qi = pl.program_id(1)
    n_tiles = (qi * Q_BLOCK + KV_TILE - 1) // KV_TILE
    for c in range(n_key_tiles + 1):
        pl.when(n_tiles == c)(functools.partial(
            _nsa_step, c, qi, *refs, n_cmp_rows=n_cmp_rows, n_slc=n_slc))


def _nsa_step(n_tiles, qi, qt_ref, gt_ref, kc_ref, vct_ref, ks_ref, kw_ref, vt_ref, ovt_ref,
              o_ref, rhs_ref, *, n_cmp_rows, n_slc):
    q0 = qi * Q_BLOCK
    n_blk = min(n_slc, 16 * ((KV_TILE // SEL_LEN * n_tiles + Q_BLOCK // SEL_LEN + 15) // 16))
    n_cmp = min(n_cmp_rows, 128 * (((KV_TILE * n_tiles + Q_BLOCK) // CMP_STRIDE + 127) // 128))
    steady = KV_TILE * (n_tiles - 1) >= WINDOW

    qblk = qt_ref[0]
    zq = jnp.zeros((B_HD, QW), BF16)
    qrows = []
    for g in range(B_GROUPS):
        qg = jnp.concatenate(
            [qblk[(g * B_HPG + h) * B_HD:(g * B_HPG + h + 1) * B_HD, :] for h in range(B_HPG)], axis=1)
        qrows.append(jnp.concatenate([qg if gg == g else zq for gg in range(B_GROUPS)], axis=1))
    qbd = jnp.concatenate(qrows, axis=0)

    def tq(rows):
        return lax.broadcasted_iota(jnp.int32, (rows, LW), 1) & (Q_BLOCK - 1)

    qs = pl.multiple_of(q0, Q_BLOCK)
    ws = pl.multiple_of(jnp.maximum(q0 - WINDOW, 0), Q_BLOCK)
    sc = _dot(kc_ref[0, 0:n_cmp, :], qbd)
    sw = _dot(kw_ref[0, pl.ds(ws, WIN_KEYS), :], qbd)
    s_d = _dot(ks_ref[0, pl.ds(qs, Q_BLOCK), 0:B_KV_W], qbd)

    crow = lax.broadcasted_iota(jnp.int32, (n_cmp, LW), 0)
    okc = CMP_STRIDE * crow + (CMP_LEN - 1) <= q0 + tq(n_cmp)
    sc = jnp.where(okc, sc, NEG)
    pc = jnp.where(okc, jnp.exp2(sc - jnp.max(sc, axis=0, keepdims=True)), 0.0)
    pc = pc / jnp.maximum(jnp.sum(pc, axis=0, keepdims=True), 1e-30)
    pcb = pc.astype(BF16)
    oc = [_dot(vct_ref[0, g * B_HD:(g + 1) * B_HD, 0:n_cmp], pcb[:, g * QW:(g + 1) * QW])
          for g in range(B_GROUPS)]

    def strip_update(state, s, v):
        m, acc = state
        m_new = jnp.maximum(m, jnp.max(s, axis=0, keepdims=True))
        pb = jnp.exp2(s - m_new).astype(BF16)
        return m_new, jnp.exp2(m - m_new) * acc + _dot(v, pb)

    def v_block(j, start, size):
        return vt_ref[0, j * V_BLK:(j + 1) * V_BLK, pl.ds(start, size)]

    n_str = LW // STRIP
    grp = lambda j: j * STRIP // QW
    lanes = lambda j: slice(j * STRIP, (j + 1) * STRIP)
    fresh = lambda: (jnp.full((1, STRIP), NEG, F32), jnp.zeros((V_BLK, STRIP), F32))
    row_q = lax.broadcasted_iota(jnp.int32, (Q_BLOCK, STRIP), 0)
    tq_q = lax.broadcasted_iota(jnp.int32, (Q_BLOCK, STRIP), 1) & (Q_BLOCK - 1)

    jrow = lax.broadcasted_iota(jnp.int32, (n_blk, Q_BLOCK), 0)
    cur = (q0 + lax.broadcasted_iota(jnp.int32, (n_blk, Q_BLOCK), 1)) >> SEL_SHIFT
    causal_blk = jrow <= cur
    forced = (jrow == 0) | (jrow == cur) | (jrow == cur - 1)
    ovt = ovt_ref[0:n_blk, 0:n_cmp]
    imps = []
    for g in range(B_GROUPS):
        psum = pc[:, g * QW:g * QW + Q_BLOCK]
        for h in range(1, B_HPG):
            psum = psum + pc[:, g * QW + h * Q_BLOCK:g * QW + (h + 1) * Q_BLOCK]
        p_hi = psum.astype(BF16)
        p_lo = (psum - p_hi.astype(F32)).astype(BF16)
        imp = _dot(ovt, p_hi) + _dot(ovt, p_lo)
        imp = jnp.where(forced & causal_blk, FORCE_SCORE, imp)
        imps.append(jnp.where(causal_blk, imp, NEG))

    biases = []
    for g in range(B_GROUPS):
        sel = (_block_rank(imps[g], n_blk, range(n_blk)) < float(min(SEL_TOPK, n_slc))) & causal_blk
        sel = sel & (jrow // (Q_BLOCK // SEL_LEN) != qi)
        biases += [jnp.where(sel, 0.0, NEG).astype(BF16)] * B_HPG
    bias = jnp.concatenate(biases, axis=1)
    rhs_ref[...] = jnp.concatenate(
        [qbd, bias, jnp.zeros((KS_AUG_W - B_KV_W - n_blk, LW), BF16)], axis=0)

    states = [
        strip_update(fresh(), jnp.where(row_q <= tq_q, s_d[:, lanes(j)], NEG), v_block(grp(j), qs, Q_BLOCK))
        for j in range(n_str)]

    order = [(kt, j) for kt in range(n_tiles) for j in range(n_str)]

    def strip_scores(kt, j):
        return _dot(ks_ref[0, kt * KV_TILE:(kt + 1) * KV_TILE, :], rhs_ref[:, lanes(j)])

    pending = [strip_scores(*order[i]) for i in range(min(LOOKAHEAD, len(order)))]
    for i, (kt, j) in enumerate(order):
        s = pending.pop(0)
        if i + LOOKAHEAD < len(order):
            pending.append(strip_scores(*order[i + LOOKAHEAD]))
        states[j] = strip_update(states[j], s, v_block(grp(j), kt * KV_TILE, KV_TILE))

    acc_w = []
    for j in range(n_str):
        swj = sw[:, lanes(j)]
        if steady:
            lo = jnp.where(row_q > tq_q, swj[0:Q_BLOCK], NEG)
            hi = jnp.where(row_q <= tq_q, swj[WINDOW:WIN_KEYS], NEG)
            swj = jnp.concatenate([lo, swj[Q_BLOCK:WINDOW], hi], axis=0)
        else:
            krow = lax.broadcasted_iota(jnp.int32, (WIN_KEYS, STRIP), 0)
            tqw = lax.broadcasted_iota(jnp.int32, (WIN_KEYS, STRIP), 1) & (Q_BLOCK - 1)
            dist = (q0 - ws) + tqw - krow
            swj = jnp.where(lax.bitcast_convert_type(dist, jnp.uint32) < jnp.uint32(WINDOW), swj, NEG)
        acc_w.append(strip_update(fresh(), swj, v_block(B_GROUPS + grp(j), ws, WIN_KEYS))[1])

    per_g = n_str // B_GROUPS
    acc_s = [jnp.concatenate([states[g * per_g + u][1] for u in range(per_g)], axis=1) for g in range(B_GROUPS)]
    acc_w = [jnp.concatenate(acc_w[g * per_g:(g + 1) * per_g], axis=1) for g in range(B_GROUPS)]

    gts = gt_ref[0]
    for g in range(B_GROUPS):
        o_s = acc_s[g][0:B_HD] / acc_s[g][B_HD:B_HD + 1]
        o_w = acc_w[g][0:B_HD] / acc_w[g][B_HD:B_HD + 1]
        for h in range(B_HPG):
            sl = slice(h * Q_BLOCK, (h + 1) * Q_BLOCK)
            r = g * GATE_ROWS + 3 * h
            o_h = gts[r:r + 1] * oc[g][:, sl] + gts[r + 1:r + 2] * o_s[:, sl] + gts[r + 2:r + 3] * o_w[:, sl]
            o_ref[0, (g * B_HPG + h) * B_HD:(g * B_HPG + h + 1) * B_HD, :] = o_h.astype(BF16)


def _nsa(qt, gt, kc, vct, ks, kw, vt, ovt):
    bsz, _, s = qt.shape
    n_cmp_rows = kc.shape[1]
    n_slc = s // SEL_LEN
    assert n_slc <= SEL_ONEHOT_W and s % KV_TILE == 0 and s >= WIN_KEYS
    whole = lambda a: pl.BlockSpec((1,) + a.shape[1:], lambda b, i: (b, 0, 0))
    return pl.pallas_call(
        functools.partial(_nsa_kernel, n_cmp_rows=n_cmp_rows, n_slc=n_slc, n_key_tiles=s // KV_TILE),
        grid=(bsz, s // Q_BLOCK),
        in_specs=[
            pl.BlockSpec((1, B_Q_W, Q_BLOCK), lambda b, i: (b, 0, i)),
            pl.BlockSpec((1, B_GROUPS * GATE_ROWS, Q_BLOCK), lambda b, i: (b, 0, i)),
            whole(kc), whole(vct), whole(ks), whole(kw), whole(vt),
            pl.BlockSpec(ovt.shape, lambda b, i: (0, 0)),
        ],
        out_specs=pl.BlockSpec((1, B_Q_W, Q_BLOCK), lambda b, i: (b, 0, i)),
        out_shape=jax.ShapeDtypeStruct((bsz, B_Q_W, s), BF16),
        scratch_shapes=[pltpu.VMEM((KS_AUG_W, LW), BF16)],
        compiler_params=pltpu.CompilerParams(
            dimension_semantics=("parallel", "arbitrary"), vmem_limit_bytes=VMEM_LIMIT),
        name="nsa",
    )(qt, gt, kc, vct, ks, kw, vt, ovt)


FF_CHUNK = 1024


def _tail_kernel(x_ref, ya_ref, ybt_ref, mg_ref, wa_ref, wb_ref, wo_ref, g2_ref, w1_ref, w2_ref, gf_ref, o_ref):
    pa = _dot(ya_ref[0], wa_ref[...])
    pb = _dot_tn(ybt_ref[0], wb_ref[...])
    mg = mg_ref[0].astype(F32)
    merged = jax.nn.sigmoid(mg[:, 0:D_MODEL]) * pa + jax.nn.sigmoid(mg[:, D_MODEL:2 * D_MODEL]) * pb
    h = x_ref[0] + _dot(merged.astype(BF16), wo_ref[...])
    hn = _rms(h, g2_ref[...]).astype(BF16)
    acc = jnp.zeros_like(h)
    for c in range(D_FF // FF_CHUNK):
        z = _dot(hn, w1_ref[:, c * FF_CHUNK:(c + 1) * FF_CHUNK])
        acc = acc + _dot(jnp.square(jnp.maximum(z, 0.0)).astype(BF16), w2_ref[c * FF_CHUNK:(c + 1) * FF_CHUNK, :])
    o_ref[0] = _rms(h + acc, gf_ref[...])


def _tail(x, ya, ybt, mg, wa, wb, wo, g2, w1, w2, gf):
    bsz, s, _ = x.shape
    tm = min(512, s)
    tok = lambda w: pl.BlockSpec((1, tm, w), lambda b, t: (b, t, 0))
    const = lambda a: pl.BlockSpec(a.shape, lambda b, t: (0,) * a.ndim, pipeline_mode=pl.Buffered(1))
    return pl.pallas_call(
        _tail_kernel,
        grid=(bsz, s // tm),
        in_specs=[tok(D_MODEL), tok(A_W), pl.BlockSpec((1, B_Q_W, tm), lambda b, t: (b, 0, t)), tok(2 * D_MODEL),
                  const(wa), const(wb), const(wo), const(g2), const(w1), const(w2), const(gf)],
        out_specs=tok(D_MODEL),
        out_shape=jax.ShapeDtypeStruct((bsz, s, D_MODEL), F32),
        compiler_params=pltpu.CompilerParams(
            dimension_semantics=("parallel", "parallel"), vmem_limit_bytes=VMEM_LIMIT),
        name="tail",
    )(x, ya, ybt, mg, wa, wb, wo, g2, w1, w2, gf)


def _pad_groups(a, axis):
    outs = []
    for g in range(B_GROUPS):
        pads = [(0, 0)] * a.ndim
        pads[axis] = (g * B_HD, (B_GROUPS - 1 - g) * B_HD)
        outs.append(jnp.pad(a, pads))
    return jnp.stack(outs)


def _overlap_t(s):
    n_cmp = (s - CMP_LEN) // CMP_STRIDE + 1
    n_slc = s // SEL_LEN
    cs = np.arange(n_cmp)[:, None] * CMP_STRIDE
    ss = np.arange(n_slc)[None, :] * SEL_LEN
    ov = np.clip(np.minimum(cs + CMP_LEN, ss + SEL_LEN) - np.maximum(cs, ss), 0, None) / CMP_LEN
    ovt = np.zeros((n_slc, s // CMP_STRIDE), np.float32)
    ovt[:, :n_cmp] = ov.T
    return jnp.asarray(ovt, BF16)


def kernel(x, positions, norm1_g, w_in, lb_param, hgrn_norm_g, cmp_pe_k, cmp_pe_v, cmp_w1_k, cmp_w2_k,
           cmp_w1_v, cmp_w2_v, w_br_a, w_br_b, w_out, norm2_g, w_ff1, w_ff2, final_g):
    bsz, s, _ = x.shape
    assert norm1_g.shape[0] == 1, "single-layer block"

    w = w_in[0]
    o_bq = 4 * A_W
    o_kv = o_bq + B_Q_W
    o_gate = o_kv + 6 * B_KV_W
    o_mg = o_gate + 3 * B_HEADS
    kv = lambda i: w[:, o_kv + i * B_KV_W:o_kv + (i + 1) * B_KV_W]
    w_nat = jnp.concatenate([w[:, 0:o_bq], w[:, o_mg:o_mg + 2 * D_MODEL], kv(1)], axis=1).astype(BF16)
    wg = w[:, o_gate:o_mg].reshape(D_MODEL, B_GROUPS, 3 * B_HPG)
    wg = jnp.pad(wg, ((0, 0), (0, 0), (0, GATE_ROWS - 3 * B_HPG))).reshape(D_MODEL, B_GROUPS * GATE_ROWS)
    w_t = jnp.concatenate([w[:, o_bq:o_kv], kv(0), kv(2), kv(4), kv(3), kv(5), wg], axis=1).T.astype(BF16)
    invf = (ROPE_THETA ** (-jnp.arange(0, ROT_DIM, 2, dtype=F32) / ROT_DIM)).reshape(ROT_HALF, 1)

    hg, mg, vc_tok, kc_tok, ks, kw, qt, vt, gt = _proj(
        x, positions.reshape(bsz, 1, s), norm1_g, invf, w_nat, w_t)

    ya = _hgrn(hg, lb_param, hgrn_norm_g)

    w1k = _pad_groups(cmp_w1_k[0].reshape(CMP_LEN, B_HD, CMP_HIDDEN), 1).astype(BF16)
    w1v = _pad_groups(cmp_w1_v[0].reshape(CMP_LEN, B_HD, CMP_HIDDEN), 1).astype(BF16)
    kc, vct = _compress(
        kc_tok, vc_tok, _pad_groups(cmp_pe_k[0], 1), _pad_groups(cmp_pe_v[0], 1), w1k, w1v,
        _pad_groups(cmp_w2_k[0], 1).astype(BF16), cmp_w2_v[0].T.astype(BF16))

    ybt = _nsa(qt, gt, kc, vct, ks, kw, vt, _overlap_t(s))

    return _tail(x, ya, ybt, mg, w_br_a[0].astype(BF16), w_br_b[0].astype(BF16), w_out[0].astype(BF16),
                 norm2_g, w_ff1[0].astype(BF16), w_ff2[0].astype(BF16), final_g.reshape(1, D_MODEL))
```

```python
import functools

import jax
import jax.numpy as jnp
import numpy as np
from jax import lax
from jax.experimental import pallas as pl
from jax.experimental.pallas import tpu as pltpu

F32 = jnp.float32
BF16 = jnp.bfloat16

D_MODEL = 1024
A_HEADS = 4
A_DK = 128
A_DV = 128
A_CHUNK = 64
A_W = A_HEADS * A_DK
B_HEADS = 8
B_GROUPS = 2
B_HPG = B_HEADS // B_GROUPS
B_HD = 64
B_Q_W = B_HEADS * B_HD
B_KV_W = B_GROUPS * B_HD
CMP_LEN = 32
CMP_STRIDE = 16
CMP_HIDDEN = 256
SEL_LEN = 64
SEL_TOPK = 16
WINDOW = 512
Q_BLOCK = 128
FORCE_SCORE = 1e4
NEG = -1e30
ROPE_THETA = 500000.0
ROT_DIM = B_HD // 4
ROT_HALF = ROT_DIM // 2
D_FF = 4 * D_MODEL
EPS = 1e-6
LOG2_E = 1.4426950408889634
GATE_ROWS = 16
SEL_SHIFT = SEL_LEN.bit_length() - 1
SEL_ONEHOT_W = 128
KS_AUG_W = B_KV_W + SEL_ONEHOT_W
V_AUX = 16
V_BLK = B_HD + V_AUX
VT_ROWS = 2 * B_GROUPS * V_BLK

VMEM_LIMIT = 56 * 1024 * 1024

NT = (((1,), (1,)), ((), ()))
TN = (((0,), (0,)), ((), ()))


def _dot(a, b):
    return jnp.dot(a, b, preferred_element_type=F32)


def _dot_nt(a, b):
    return lax.dot_general(a, b, NT, preferred_element_type=F32)


def _dot_tn(a, b):
    return lax.dot_general(a, b, TN, preferred_element_type=F32)


def _rms(x, g):
    return x * lax.rsqrt(jnp.mean(x * x, axis=-1, keepdims=True) + EPS) * g


def _proj_kernel(x_ref, pos_ref, g1_ref, invf_ref, wnat_ref, wt_ref,
                 hg_ref, mg_ref, vc_ref, kc_ref, ks_ref, kw_ref, qt_ref, vt_ref, gt_ref):
    xn = _rms(x_ref[0], g1_ref[...]).astype(BF16)

    def nat(c0, c1):
        return _dot(xn, wnat_ref[:, c0:c1])

    half = 2 * A_W
    for c in range(2):
        hg_ref[0, :, c * half:(c + 1) * half] = nat(c * half, (c + 1) * half).astype(BF16)
        mg_ref[0, :, c * half:(c + 1) * half] = nat(4 * A_W + c * half, 4 * A_W + (c + 1) * half).astype(BF16)
    vc_ref[0] = nat(4 * A_W + 2 * D_MODEL, 4 * A_W + 2 * D_MODEL + B_KV_W)

    def tr(r0, r1):
        return _dot_nt(wt_ref[r0:r1, :], xn)

    ang = invf_ref[...] * pos_ref[0].astype(F32)
    cos = jnp.cos(ang)
    sin = jnp.sin(ang)

    def rope(xt, n_heads):
        pieces = []
        for h in range(n_heads):
            b0 = h * B_HD
            t1 = xt[b0:b0 + ROT_HALF]
            t2 = xt[b0 + ROT_HALF:b0 + ROT_DIM]
            pieces += [t1 * cos - t2 * sin, t2 * cos + t1 * sin, xt[b0 + ROT_DIM:b0 + B_HD]]
        return jnp.concatenate(pieces, axis=0)

    r_q, r_k, r_v = B_Q_W, B_Q_W + 3 * B_KV_W, B_Q_W + 5 * B_KV_W
    qt_ref[0] = (rope(tr(0, r_q), B_HEADS) * (B_HD ** -0.5 * LOG2_E)).astype(BF16)
    k3 = rope(tr(r_q, r_k), 3 * B_GROUPS).T
    kc_ref[0] = k3[:, 0:B_KV_W]
    kw_ref[0] = k3[:, 2 * B_KV_W:3 * B_KV_W].astype(BF16)
    tm = k3.shape[0]
    tok = pl.program_id(1) * tm + lax.broadcasted_iota(jnp.int32, (tm, SEL_ONEHOT_W), 0)
    lane = lax.broadcasted_iota(jnp.int32, (tm, SEL_ONEHOT_W), 1)
    ks_ref[0, :, 0:B_KV_W] = k3[:, B_KV_W:2 * B_KV_W].astype(BF16)
    ks_ref[0, :, B_KV_W:B_KV_W + SEL_ONEHOT_W] = jnp.where((tok >> SEL_SHIFT) == lane, 1.0, 0.0).astype(BF16)
    vt = tr(r_k, r_v)
    aux = jnp.where(lax.broadcasted_iota(jnp.int32, (V_AUX, tm), 0) == 0, 1.0, 0.0)
    vt_ref[0] = jnp.concatenate(
        [piece for j in range(2 * B_GROUPS) for piece in (vt[j * B_HD:(j + 1) * B_HD], aux)], axis=0).astype(BF16)
    gt_ref[0] = jax.nn.sigmoid(tr(r_v, r_v + B_GROUPS * GATE_ROWS))


def _proj(x, pos3, g1, invf, w_nat, w_t):
    bsz, s, _ = x.shape
    tm = min(512, s)
    n_nat = w_nat.shape[1]
    n_t = w_t.shape[0]
    tok = lambda w: pl.BlockSpec((1, tm, w), lambda b, t: (b, t, 0))
    trn = lambda r: pl.BlockSpec((1, r, tm), lambda b, t: (b, 0, t))
    const = lambda shp: pl.BlockSpec(shp, lambda b, t: (0,) * len(shp), pipeline_mode=pl.Buffered(1))
    out_shape = (
        jax.ShapeDtypeStruct((bsz, s, 4 * A_W), BF16),
        jax.ShapeDtypeStruct((bsz, s, 2 * D_MODEL), BF16),
        jax.ShapeDtypeStruct((bsz, s, B_KV_W), F32),
        jax.ShapeDtypeStruct((bsz, s, B_KV_W), F32),
        jax.ShapeDtypeStruct((bsz, s, KS_AUG_W), BF16),
        jax.ShapeDtypeStruct((bsz, s, B_KV_W), BF16),
        jax.ShapeDtypeStruct((bsz, B_Q_W, s), BF16),
        jax.ShapeDtypeStruct((bsz, VT_ROWS, s), BF16),
        jax.ShapeDtypeStruct((bsz, B_GROUPS * GATE_ROWS, s), F32),
    )
    return pl.pallas_call(
        _proj_kernel,
        grid=(bsz, s // tm),
        in_specs=[tok(D_MODEL), trn(1), const((1, D_MODEL)), const((ROT_HALF, 1)),
                  const((D_MODEL, n_nat)), const((n_t, D_MODEL))],
        out_specs=(tok(4 * A_W), tok(2 * D_MODEL), tok(B_KV_W), tok(B_KV_W), tok(KS_AUG_W), tok(B_KV_W),
                   trn(B_Q_W), trn(VT_ROWS), trn(B_GROUPS * GATE_ROWS)),
        out_shape=out_shape,
        compiler_params=pltpu.CompilerParams(
            dimension_semantics=("parallel", "parallel"), vmem_limit_bytes=VMEM_LIMIT),
        name="proj",
    )(x, pos3, g1, invf, w_nat, w_t)


def _hgrn_kernel(hg_ref, lbp_ref, ng_ref, o_ref, st_ref, *, n_chunks):
    @pl.when(pl.program_id(1) == 0)
    def _():
        st_ref[...] = jnp.zeros_like(st_ref)

    lbp = lbp_ref[...]
    e = jnp.exp(lbp - jnp.max(lbp, axis=0, keepdims=True))
    lb = e[0:1] / jnp.sum(e, axis=0, keepdims=True)
    ng = ng_ref[...]
    row = lax.broadcasted_iota(jnp.int32, (A_CHUNK, A_CHUNK), 0)
    col = lax.broadcasted_iota(jnp.int32, (A_CHUNK, A_CHUNK), 1)
    causal = row >= col
    tri = jnp.where(causal, 1.0, 0.0).astype(BF16)

    heads = [slice(h * A_DK, (h + 1) * A_DK) for h in range(A_HEADS)]

    def stage1(c):
        blk = hg_ref[0, c * A_CHUNK:(c + 1) * A_CHUNK, :]
        f = lb + (1.0 - lb) * jax.nn.sigmoid(blk[:, A_W:2 * A_W].astype(F32))
        lf = jnp.log(f)
        lf_hi = lf.astype(BF16)
        r1 = lf - lf_hi.astype(F32)
        lf_mid = r1.astype(BF16)
        lf_lo = (r1 - lf_mid.astype(F32)).astype(BF16)
        b = _dot(tri, lf_hi) + _dot(tri, lf_mid) + _dot(tri, lf_lo)
        return dict(c=c, k=1.0 - f, b=b)

    def stage2(s):
        c, k, b = s["c"], s["k"], s["b"]
        blk = hg_ref[0, c * A_CHUNK:(c + 1) * A_CHUNK, :]
        dec = jnp.exp(b[A_CHUNK - 1:A_CHUNK, :])
        k_dec = k * jnp.exp(-b)
        return dict(c=c, v=blk[:, 2 * A_W:3 * A_W],
                    q_dec=(blk[:, 0:A_W].astype(F32) * jnp.exp(b)).astype(BF16),
                    k_dec=k_dec.astype(BF16), k_til=(k_dec * dec).astype(BF16), dec=dec)

    def stage3(s):
        s = dict(s)
        attn = [jnp.where(causal, _dot_nt(s["q_dec"][:, sl], s["k_dec"][:, sl]), 0.0).astype(BF16)
                for sl in heads]
        s["o_intra"] = [_dot(attn[h], s["v"][:, sl]) for h, sl in enumerate(heads)]
        s["kv_t"] = [_dot_tn(s["v"][:, sl], s["k_til"][:, sl]) for sl in heads]
        return s

    def stage4(s, state):
        c = s["c"]
        g = hg_ref[0, c * A_CHUNK:(c + 1) * A_CHUNK, 3 * A_W:4 * A_W].astype(F32)
        new_state = []
        for h, sl in enumerate(heads):
            o = s["o_intra"][h] + _dot_nt(s["q_dec"][:, sl], state[h].astype(BF16))
            new_state.append(state[h] * s["dec"][:, sl] + s["kv_t"][h])
            gh = g[:, sl]
            y = _rms(o, ng) * (gh * jax.nn.sigmoid(gh))
            o_ref[0, c * A_CHUNK:(c + 1) * A_CHUNK, sl] = y.astype(BF16)
        return new_state

    state = [st_ref[h] for h in range(A_HEADS)]
    pipe = [None, None, None]
    for t in range(n_chunks + 3):
        if pipe[2] is not None:
            state = stage4(pipe[2], state)
        pipe[2] = stage3(pipe[1]) if pipe[1] is not None else None
        pipe[1] = stage2(pipe[0]) if pipe[0] is not None else None
        pipe[0] = stage1(t) if t < n_chunks else None
    for h in range(A_HEADS):
        st_ref[h] = state[h]


def _hgrn(hg, lb_param, norm_g):
    bsz, s, _ = hg.shape
    tt = min(512, s)
    return pl.pallas_call(
        functools.partial(_hgrn_kernel, n_chunks=tt // A_CHUNK),
        grid=(bsz, s // tt),
        in_specs=[pl.BlockSpec((1, tt, 4 * A_W), lambda b, t: (b, t, 0)),
                  pl.BlockSpec(lb_param.shape, lambda b, t: (0, 0)),
                  pl.BlockSpec((1, A_DV), lambda b, t: (0, 0))],
        out_specs=pl.BlockSpec((1, tt, A_W), lambda b, t: (b, t, 0)),
        out_shape=jax.ShapeDtypeStruct((bsz, s, A_W), BF16),
        scratch_shapes=[pltpu.VMEM((A_HEADS, A_DV, A_DK), F32)],
        compiler_params=pltpu.CompilerParams(
            dimension_semantics=("parallel", "arbitrary"), vmem_limit_bytes=VMEM_LIMIT),
        name="hgrn",
    )(hg, lb_param, norm_g)


def _compress_kernel(kt_ref, vt_ref, pek_ref, pev_ref, w1k_ref, w1v_ref, w2k_ref, w2vt_ref,
                     kc_ref, vct_ref, *, n_blk):
    def hidden(t_ref, pe_ref, w1_ref):
        halves = [(t_ref[0] + pe_ref[a:a + 1, :]).astype(BF16) for a in range(2)]
        out = []
        for g in range(B_GROUPS):
            pre = _dot(halves[0], w1_ref[g, 0]) + pltpu.roll(_dot(halves[1], w1_ref[g, 1]), n_blk - 1, 0)
            out.append(jax.nn.gelu(pre).astype(BF16))
        return out

    valid_r = lax.broadcasted_iota(jnp.int32, (n_blk, B_KV_W), 0) < n_blk - 1
    valid_c = lax.broadcasted_iota(jnp.int32, (B_HD, n_blk), 1) < n_blk - 1
    hk = hidden(kt_ref, pek_ref, w1k_ref)
    hv = hidden(vt_ref, pev_ref, w1v_ref)
    kc = jnp.zeros((n_blk, B_KV_W), F32)
    for g in range(B_GROUPS):
        kc = kc + _dot(hk[g], w2k_ref[g])
        vct = _dot_nt(w2vt_ref[...], hv[g])
        vct_ref[0, g * B_HD:(g + 1) * B_HD, :] = jnp.where(valid_c, vct, 0.0).astype(BF16)
    kc_ref[0] = jnp.where(valid_r, kc, 0.0).astype(BF16)


def _compress(kc_tok, vc_tok, pek, pev, w1k, w1v, w2k, w2vt):
    bsz, s, _ = kc_tok.shape
    n_blk = s // CMP_STRIDE
    row_w = CMP_STRIDE * B_KV_W
    kc_tok = kc_tok.reshape(bsz, n_blk, row_w)
    vc_tok = vc_tok.reshape(bsz, n_blk, row_w)
    full = lambda a: pl.BlockSpec(a.shape, lambda b: (0,) * a.ndim)
    tokspec = pl.BlockSpec((1, n_blk, row_w), lambda b: (b, 0, 0))
    return pl.pallas_call(
        functools.partial(_compress_kernel, n_blk=n_blk),
        grid=(bsz,),
        in_specs=[tokspec, tokspec, full(pek), full(pev), full(w1k), full(w1v), full(w2k), full(w2vt)],
        out_specs=(pl.BlockSpec((1, n_blk, B_KV_W), lambda b: (b, 0, 0)),
                   pl.BlockSpec((1, B_KV_W, n_blk), lambda b: (b, 0, 0))),
        out_shape=(jax.ShapeDtypeStruct((bsz, n_blk, B_KV_W), BF16),
                   jax.ShapeDtypeStruct((bsz, B_KV_W, n_blk), BF16)),
        compiler_params=pltpu.CompilerParams(
            dimension_semantics=("parallel",), vmem_limit_bytes=VMEM_LIMIT),
        name="compress",
    )(kc_tok, vc_tok, pek, pev, w1k, w1v, w2k, w2vt)


KV_TILE = 512
WIN_KEYS = WINDOW + Q_BLOCK
QW = B_HPG * Q_BLOCK
LW = B_GROUPS * QW
STRIP = 256
LOOKAHEAD = 4


def _block_rank(imp, n_slc, i_range):
    sub = lax.broadcasted_iota(jnp.int32, (8, Q_BLOCK), 0)
    rank = jnp.zeros((n_slc, Q_BLOCK), F32)
    for i in i_range:
        ri = imp[i:i + 1, :]
        parts = []
        for r in range(n_slc // 8):
            blk = imp[8 * r:8 * r + 8, :]
            gt = jnp.where(ri > blk, 1.0, 0.0)
            ge = jnp.where(ri >= blk, 1.0, 0.0)
            if 8 * r + 7 <= i:
                parts.append(gt)
            elif 8 * r > i:
                parts.append(ge)
            else:
                parts.append(jnp.where(sub > i - 8 * r, ge, gt))
        rank = rank + jnp.concatenate(parts, axis=0)
    return rank


def _nsa_kernel(*refs, n_cmp_rows, n_slc, n_key_tiles):
    qi = pl.program_id(1)
    n_tiles = (qi * Q_BLOCK + KV_TILE - 1) // KV_TILE
    for c in range(n_key_tiles + 1):
        pl.when(n_tiles == c)(functools.partial(
            _nsa_step, c, qi, *refs, n_cmp_rows=n_cmp_rows, n_slc=n_slc))


def _nsa_step(n_tiles, qi, qt_ref, gt_ref, kc_ref, vct_ref, ks_ref, kw_ref, vt_ref, ovt_ref,
              o_ref, rhs_ref, *, n_cmp_rows, n_slc):
    q0 = qi * Q_BLOCK
    n_blk = min(n_slc, 16 * ((KV_TILE // SEL_LEN * n_tiles + Q_BLOCK // SEL_LEN + 15) // 16))
    n_cmp = min(n_cmp_rows, 128 * (((KV_TILE * n_tiles + Q_BLOCK) // CMP_STRIDE + 127) // 128))
    steady = KV_TILE * (n_tiles - 1) >= WINDOW

    qblk = qt_ref[0]
    zq = jnp.zeros((B_HD, QW), BF16)
    qrows = []
    for g in range(B_GROUPS):
        qg = jnp.concatenate(
            [qblk[(g * B_HPG + h) * B_HD:(g * B_HPG + h + 1) * B_HD, :] for h in range(B_HPG)], axis=1)
        qrows.append(jnp.concatenate([qg if gg == g else zq for gg in range(B_GROUPS)], axis=1))
    qbd = jnp.concatenate(qrows, axis=0)

    def tq(rows):
        return lax.broadcasted_iota(jnp.int32, (rows, LW), 1) & (Q_BLOCK - 1)

    qs = pl.multiple_of(q0, Q_BLOCK)
    ws = pl.multiple_of(jnp.maximum(q0 - WINDOW, 0), Q_BLOCK)
    sc = _dot(kc_ref[0, 0:n_cmp, :], qbd)
    sw = _dot(kw_ref[0, pl.ds(ws, WIN_KEYS), :], qbd)
    s_d = _dot(ks_ref[0, pl.ds(qs, Q_BLOCK), 0:B_KV_W], qbd)

    crow = lax.broadcasted_iota(jnp.int32, (n_cmp, LW), 0)
    okc = CMP_STRIDE * crow + (CMP_LEN - 1) <= q0 + tq(n_cmp)
    sc = jnp.where(okc, sc, NEG)
    pc = jnp.where(okc, jnp.exp2(sc - jnp.max(sc, axis=0, keepdims=True)), 0.0)
    pc = pc / jnp.maximum(jnp.sum(pc, axis=0, keepdims=True), 1e-30)
    pcb = pc.astype(BF16)
    oc = [_dot(vct_ref[0, g * B_HD:(g + 1) * B_HD, 0:n_cmp], pcb[:, g * QW:(g + 1) * QW])
          for g in range(B_GROUPS)]

    def strip_update(state, s, v):
        m, acc = state
        m_new = jnp.maximum(m, jnp.max(s, axis=0, keepdims=True))
        pb = jnp.exp2(s - m_new).astype(BF16)
        return m_new, jnp.exp2(m - m_new) * acc + _dot(v, pb)

    def v_block(j, start, size):
        return vt_ref[0, j * V_BLK:(j + 1) * V_BLK, pl.ds(start, size)]

    n_str = LW // STRIP
    grp = lambda j: j * STRIP // QW
    lanes = lambda j: slice(j * STRIP, (j + 1) * STRIP)
    fresh = lambda: (jnp.full((1, STRIP), NEG, F32), jnp.zeros((V_BLK, STRIP), F32))
    row_q = lax.broadcasted_iota(jnp.int32, (Q_BLOCK, STRIP), 0)
    tq_q = lax.broadcasted_iota(jnp.int32, (Q_BLOCK, STRIP), 1) & (Q_BLOCK - 1)

    jrow = lax.broadcasted_iota(jnp.int32, (n_blk, Q_BLOCK), 0)
    cur = (q0 + lax.broadcasted_iota(jnp.int32, (n_blk, Q_BLOCK), 1)) >> SEL_SHIFT
    causal_blk = jrow <= cur
    forced = (jrow == 0) | (jrow == cur) | (jrow == cur - 1)
    ovt = ovt_ref[0:n_blk, 0:n_cmp]
    imps = []
    for g in range(B_GROUPS):
        psum = pc[:, g * QW:g * QW + Q_BLOCK]
        for h in range(1, B_HPG):
            psum = psum + pc[:, g * QW + h * Q_BLOCK:g * QW + (h + 1) * Q_BLOCK]
        p_hi = psum.astype(BF16)
        p_lo = (psum - p_hi.astype(F32)).astype(BF16)
        imp = _dot(ovt, p_hi) + _dot(ovt, p_lo)
        imp = jnp.where(forced & causal_blk, FORCE_SCORE, imp)
        imps.append(jnp.where(causal_blk, imp, NEG))

    biases = []
    for g in range(B_GROUPS):
        sel = (_block_rank(imps[g], n_blk, range(n_blk)) < float(min(SEL_TOPK, n_slc))) & causal_blk
        sel = sel & (jrow // (Q_BLOCK // SEL_LEN) != qi)
        biases += [jnp.where(sel, 0.0, NEG).astype(BF16)] * B_HPG
    bias = jnp.concatenate(biases, axis=1)
    rhs_ref[...] = jnp.concatenate(
        [qbd, bias, jnp.zeros((KS_AUG_W - B_KV_W - n_blk, LW), BF16)], axis=0)

    states = [
        strip_update(fresh(), jnp.where(row_q <= tq_q, s_d[:, lanes(j)], NEG), v_block(grp(j), qs, Q_BLOCK))
        for j in range(n_str)]

    order = [(kt, j) for kt in range(n_tiles) for j in range(n_str)]

    def strip_scores(kt, j):
        return _dot(ks_ref[0, kt * KV_TILE:(kt + 1) * KV_TILE, :], rhs_ref[:, lanes(j)])

    pending = [strip_scores(*order[i]) for i in range(min(LOOKAHEAD, len(order)))]
    for i, (kt, j) in enumerate(order):
        s = pending.pop(0)
        if i + LOOKAHEAD < len(order):
            pending.append(strip_scores(*order[i + LOOKAHEAD]))
        states[j] = strip_update(states[j], s, v_block(grp(j), kt * KV_TILE, KV_TILE))

    acc_w = []
    for j in range(n_str):
        swj = sw[:, lanes(j)]
        if steady:
            lo = jnp.where(row_q > tq_q, swj[0:Q_BLOCK], NEG)
            hi = jnp.where(row_q <= tq_q, swj[WINDOW:WIN_KEYS], NEG)
            swj = jnp.concatenate([lo, swj[Q_BLOCK:WINDOW], hi], axis=0)
        else:
            krow = lax.broadcasted_iota(jnp.int32, (WIN_KEYS, STRIP), 0)
            tqw = lax.broadcasted_iota(jnp.int32, (WIN_KEYS, STRIP), 1) & (Q_BLOCK - 1)
            dist = (q0 - ws) + tqw - krow
            swj = jnp.where(lax.bitcast_convert_type(dist, jnp.uint32) < jnp.uint32(WINDOW), swj, NEG)
        acc_w.append(strip_update(fresh(), swj, v_block(B_GROUPS + grp(j), ws, WIN_KEYS))[1])

    per_g = n_str // B_GROUPS
    acc_s = [jnp.concatenate([states[g * per_g + u][1] for u in range(per_g)], axis=1) for g in range(B_GROUPS)]
    acc_w = [jnp.concatenate(acc_w[g * per_g:(g + 1) * per_g], axis=1) for g in range(B_GROUPS)]

    gts = gt_ref[0]
    for g in range(B_GROUPS):
        o_s = acc_s[g][0:B_HD] / acc_s[g][B_HD:B_HD + 1]
        o_w = acc_w[g][0:B_HD] / acc_w[g][B_HD:B_HD + 1]
        for h in range(B_HPG):
            sl = slice(h * Q_BLOCK, (h + 1) * Q_BLOCK)
            r = g * GATE_ROWS + 3 * h
            o_h = gts[r:r + 1] * oc[g][:, sl] + gts[r + 1:r + 2] * o_s[:, sl] + gts[r + 2:r + 3] * o_w[:, sl]
            o_ref[0, (g * B_HPG + h) * B_HD:(g * B_HPG + h + 1) * B_HD, :] = o_h.astype(BF16)


def _nsa(qt, gt, kc, vct, ks, kw, vt, ovt):
    bsz, _, s = qt.shape
    n_cmp_rows = kc.shape[1]
    n_slc = s // SEL_LEN
    assert n_slc <= SEL_ONEHOT_W and s % KV_TILE == 0 and s >= WIN_KEYS
    whole = lambda a: pl.BlockSpec((1,) + a.shape[1:], lambda b, i: (b, 0, 0))
    return pl.pallas_call(
        functools.partial(_nsa_kernel, n_cmp_rows=n_cmp_rows, n_slc=n_slc, n_key_tiles=s // KV_TILE),
        grid=(bsz, s // Q_BLOCK),
        in_specs=[
            pl.BlockSpec((1, B_Q_W, Q_BLOCK), lambda b, i: (b, 0, i)),
            pl.BlockSpec((1, B_GROUPS * GATE_ROWS, Q_BLOCK), lambda b, i: (b, 0, i)),
            whole(kc), whole(vct), whole(ks), whole(kw), whole(vt),
            pl.BlockSpec(ovt.shape, lambda b, i: (0, 0)),
        ],
        out_specs=pl.BlockSpec((1, B_Q_W, Q_BLOCK), lambda b, i: (b, 0, i)),
        out_shape=jax.ShapeDtypeStruct((bsz, B_Q_W, s), BF16),
        scratch_shapes=[pltpu.VMEM((KS_AUG_W, LW), BF16)],
        compiler_params=pltpu.CompilerParams(
            dimension_semantics=("parallel", "arbitrary"), vmem_limit_bytes=VMEM_LIMIT),
        name="nsa",
    )(qt, gt, kc, vct, ks, kw, vt, ovt)


FF_CHUNK = 1024


def _tail_kernel(x_ref, ya_ref, ybt_ref, mg_ref, wa_ref, wb_ref, wo_ref, g2_ref, w1_ref, w2_ref, gf_ref, o_ref):
    pa = _dot(ya_ref[0], wa_ref[...])
    pb = _dot_tn(ybt_ref[0], wb_ref[...])
    mg = mg_ref[0].astype(F32)
    merged = jax.nn.sigmoid(mg[:, 0:D_MODEL]) * pa + jax.nn.sigmoid(mg[:, D_MODEL:2 * D_MODEL]) * pb
    h = x_ref[0] + _dot(merged.astype(BF16), wo_ref[...])
    hn = _rms(h, g2_ref[...]).astype(BF16)
    acc = jnp.zeros_like(h)
    for c in range(D_FF // FF_CHUNK):
        z = _dot(hn, w1_ref[:, c * FF_CHUNK:(c + 1) * FF_CHUNK])
        acc = acc + _dot(jnp.square(jnp.maximum(z, 0.0)).astype(BF16), w2_ref[c * FF_CHUNK:(c + 1) * FF_CHUNK, :])
    o_ref[0] = _rms(h + acc, gf_ref[...])


def _tail(x, ya, ybt, mg, wa, wb, wo, g2, w1, w2, gf):
    bsz, s, _ = x.shape
    tm = min(512, s)
    tok = lambda w: pl.BlockSpec((1, tm, w), lambda b, t: (b, t, 0))
    const = lambda a: pl.BlockSpec(a.shape, lambda b, t: (0,) * a.ndim, pipeline_mode=pl.Buffered(1))
    return pl.pallas_call(
        _tail_kernel,
        grid=(bsz, s // tm),
        in_specs=[tok(D_MODEL), tok(A_W), pl.BlockSpec((1, B_Q_W, tm), lambda b, t: (b, 0, t)), tok(2 * D_MODEL),
                  const(wa), const(wb), const(wo), const(g2), const(w1), const(w2), const(gf)],
        out_specs=tok(D_MODEL),
        out_shape=jax.ShapeDtypeStruct((bsz, s, D_MODEL), F32),
        compiler_params=pltpu.CompilerParams(
            dimension_semantics=("parallel", "parallel"), vmem_limit_bytes=VMEM_LIMIT),
        name="tail",
    )(x, ya, ybt, mg, wa, wb, wo, g2, w1, w2, gf)


def _pad_groups(a, axis):
    outs = []
    for g in range(B_GROUPS):
        pads = [(0, 0)] * a.ndim
        pads[axis] = (g * B_HD, (B_GROUPS - 1 - g) * B_HD)
        outs.append(jnp.pad(a, pads))
    return jnp.stack(outs)


def _overlap_t(s):
    n_cmp = (s - CMP_LEN) // CMP_STRIDE + 1
    n_slc = s // SEL_LEN
    cs = np.arange(n_cmp)[:, None] * CMP_STRIDE
    ss = np.arange(n_slc)[None, :] * SEL_LEN
    ov = np.clip(np.minimum(cs + CMP_LEN, ss + SEL_LEN) - np.maximum(cs, ss), 0, None) / CMP_LEN
    ovt = np.zeros((n_slc, s // CMP_STRIDE), np.float32)
    ovt[:, :n_cmp] = ov.T
    return jnp.asarray(ovt, BF16)


def kernel(x, positions, norm1_g, w_in, lb_param, hgrn_norm_g, cmp_pe_k, cmp_pe_v, cmp_w1_k, cmp_w2_k,
           cmp_w1_v, cmp_w2_v, w_br_a, w_br_b, w_out, norm2_g, w_ff1, w_ff2, final_g):
    bsz, s, _ = x.shape
    assert norm1_g.shape[0] == 1, "single-layer block"

    w = w_in[0]
    o_bq = 4 * A_W
    o_kv = o_bq + B_Q_W
    o_gate = o_kv + 6 * B_KV_W
    o_mg = o_gate + 3 * B_HEADS
    kv = lambda i: w[:, o_kv + i * B_KV_W:o_kv + (i + 1) * B_KV_W]
    w_nat = jnp.concatenate([w[:, 0:o_bq], w[:, o_mg:o_mg + 2 * D_MODEL], kv(1)], axis=1).astype(BF16)
    wg = w[:, o_gate:o_mg].reshape(D_MODEL, B_GROUPS, 3 * B_HPG)
    wg = jnp.pad(wg, ((0, 0), (0, 0), (0, GATE_ROWS - 3 * B_HPG))).reshape(D_MODEL, B_GROUPS * GATE_ROWS)
    w_t = jnp.concatenate([w[:, o_bq:o_kv], kv(0), kv(2), kv(4), kv(3), kv(5), wg], axis=1).T.astype(BF16)
    invf = (ROPE_THETA ** (-jnp.arange(0, ROT_DIM, 2, dtype=F32) / ROT_DIM)).reshape(ROT_HALF, 1)

    hg, mg, vc_tok, kc_tok, ks, kw, qt, vt, gt = _proj(
        x, positions.reshape(bsz, 1, s), norm1_g, invf, w_nat, w_t)

    ya = _hgrn(hg, lb_param, hgrn_norm_g)

    half_w = CMP_LEN // 2 * B_KV_W

    def w1_halves(w1):
        padded = _pad_groups(w1.reshape(CMP_LEN, B_HD, CMP_HIDDEN), 1)
        return padded.reshape(B_GROUPS, 2, half_w, CMP_HIDDEN).astype(BF16)

    pe_halves = lambda pe: jnp.tile(pe, (1, B_GROUPS)).reshape(2, half_w)
    kc, vct = _compress(
        kc_tok, vc_tok, pe_halves(cmp_pe_k[0]), pe_halves(cmp_pe_v[0]), w1_halves(cmp_w1_k[0]),
        w1_halves(cmp_w1_v[0]), _pad_groups(cmp_w2_k[0], 1).astype(BF16), cmp_w2_v[0].T.astype(BF16))

    ybt = _nsa(qt, gt, kc, vct, ks, kw, vt, _overlap_t(s))

    return _tail(x, ya, ybt, mg, w_br_a[0].astype(BF16), w_br_b[0].astype(BF16), w_out[0].astype(BF16),
                 norm2_g, w_ff1[0].astype(BF16), w_ff2[0].astype(BF16), final_g.reshape(1, D_MODEL))
```

```python
import functools

import jax
import jax.numpy as jnp
import numpy as np
from jax import lax
from jax.experimental import pallas as pl
from jax.experimental.pallas import tpu as pltpu

F32 = jnp.float32
BF16 = jnp.bfloat16

D_MODEL = 1024
A_HEADS = 4
A_DK = 128
A_DV = 128
A_CHUNK = 64
A_W = A_HEADS * A_DK
B_HEADS = 8
B_GROUPS = 2
B_HPG = B_HEADS // B_GROUPS
B_HD = 64
B_Q_W = B_HEADS * B_HD
B_KV_W = B_GROUPS * B_HD
CMP_LEN = 32
CMP_STRIDE = 16
CMP_HIDDEN = 256
SEL_LEN = 64
SEL_TOPK = 16
WINDOW = 512
Q_BLOCK = 128
FORCE_SCORE = 1e4
NEG = -1e30
ROPE_THETA = 500000.0
ROT_DIM = B_HD // 4
ROT_HALF = ROT_DIM // 2
D_FF = 4 * D_MODEL
EPS = 1e-6
LOG2_E = 1.4426950408889634
GATE_ROWS = 16
SEL_SHIFT = SEL_LEN.bit_length() - 1
SEL_ONEHOT_W = 128
KS_AUG_W = B_KV_W + SEL_ONEHOT_W
V_AUX = 16
V_BLK = B_HD + V_AUX
VT_ROWS = 2 * B_GROUPS * V_BLK

VMEM_LIMIT = 56 * 1024 * 1024

NT = (((1,), (1,)), ((), ()))
TN = (((0,), (0,)), ((), ()))


def _dot(a, b):
    return jnp.dot(a, b, preferred_element_type=F32)


def _dot_nt(a, b):
    return lax.dot_general(a, b, NT, preferred_element_type=F32)


def _dot_tn(a, b):
    return lax.dot_general(a, b, TN, preferred_element_type=F32)


def _rms(x, g):
    return x * lax.rsqrt(jnp.mean(x * x, axis=-1, keepdims=True) + EPS) * g


def _proj_kernel(x_ref, pos_ref, g1_ref, invf_ref, wnat_ref, wt_ref,
                 hg_ref, mg_ref, vc_ref, kc_ref, ks_ref, kw_ref, qt_ref, vt_ref, gt_ref):
    xn = _rms(x_ref[0], g1_ref[...]).astype(BF16)

    def nat(c0, c1):
        return _dot(xn, wnat_ref[:, c0:c1])

    half = 2 * A_W
    for c in range(2):
        hg_ref[0, :, c * half:(c + 1) * half] = nat(c * half, (c + 1) * half).astype(BF16)
        mg_ref[0, :, c * half:(c + 1) * half] = nat(4 * A_W + c * half, 4 * A_W + (c + 1) * half).astype(BF16)
    vc_ref[0] = nat(4 * A_W + 2 * D_MODEL, 4 * A_W + 2 * D_MODEL + B_KV_W)

    def tr(r0, r1):
        return _dot_nt(wt_ref[r0:r1, :], xn)

    ang = invf_ref[...] * pos_ref[0].astype(F32)
    cos = jnp.cos(ang)
    sin = jnp.sin(ang)

    def rope(xt, n_heads):
        pieces = []
        for h in range(n_heads):
            b0 = h * B_HD
            t1 = xt[b0:b0 + ROT_HALF]
            t2 = xt[b0 + ROT_HALF:b0 + ROT_DIM]
            pieces += [t1 * cos - t2 * sin, t2 * cos + t1 * sin, xt[b0 + ROT_DIM:b0 + B_HD]]
        return jnp.concatenate(pieces, axis=0)

    r_q, r_k, r_v = B_Q_W, B_Q_W + 3 * B_KV_W, B_Q_W + 5 * B_KV_W
    qt_ref[0] = (rope(tr(0, r_q), B_HEADS) * (B_HD ** -0.5 * LOG2_E)).astype(BF16)
    k3 = rope(tr(r_q, r_k), 3 * B_GROUPS).T
    kc_ref[0] = k3[:, 0:B_KV_W]
    kw_ref[0] = k3[:, 2 * B_KV_W:3 * B_KV_W].astype(BF16)
    tm = k3.shape[0]
    tok = pl.program_id(1) * tm + lax.broadcasted_iota(jnp.int32, (tm, SEL_ONEHOT_W), 0)
    lane = lax.broadcasted_iota(jnp.int32, (tm, SEL_ONEHOT_W), 1)
    ks_ref[0, :, 0:B_KV_W] = k3[:, B_KV_W:2 * B_KV_W].astype(BF16)
    ks_ref[0, :, B_KV_W:B_KV_W + SEL_ONEHOT_W] = jnp.where((tok >> SEL_SHIFT) == lane, 1.0, 0.0).astype(BF16)
    vt = tr(r_k, r_v)
    aux = jnp.where(lax.broadcasted_iota(jnp.int32, (V_AUX, tm), 0) == 0, 1.0, 0.0)
    vt_ref[0] = jnp.concatenate(
        [piece for j in range(2 * B_GROUPS) for piece in (vt[j * B_HD:(j + 1) * B_HD], aux)], axis=0).astype(BF16)
    gt_ref[0] = jax.nn.sigmoid(tr(r_v, r_v + B_GROUPS * GATE_ROWS))


def _proj(x, pos3, g1, invf, w_nat, w_t):
    bsz, s, _ = x.shape
    tm = min(512, s)
    n_nat = w_nat.shape[1]
    n_t = w_t.shape[0]
    tok = lambda w: pl.BlockSpec((1, tm, w), lambda b, t: (b, t, 0))
    trn = lambda r: pl.BlockSpec((1, r, tm), lambda b, t: (b, 0, t))
    const = lambda shp: pl.BlockSpec(shp, lambda b, t: (0,) * len(shp), pipeline_mode=pl.Buffered(1))
    out_shape = (
        jax.ShapeDtypeStruct((bsz, s, 4 * A_W), BF16),
        jax.ShapeDtypeStruct((bsz, s, 2 * D_MODEL), BF16),
        jax.ShapeDtypeStruct((bsz, s, B_KV_W), F32),
        jax.ShapeDtypeStruct((bsz, s, B_KV_W), F32),
        jax.ShapeDtypeStruct((bsz, s, KS_AUG_W), BF16),
        jax.ShapeDtypeStruct((bsz, s, B_KV_W), BF16),
        jax.ShapeDtypeStruct((bsz, B_Q_W, s), BF16),
        jax.ShapeDtypeStruct((bsz, VT_ROWS, s), BF16),
        jax.ShapeDtypeStruct((bsz, B_GROUPS * GATE_ROWS, s), F32),
    )
    return pl.pallas_call(
        _proj_kernel,
        grid=(bsz, s // tm),
        in_specs=[tok(D_MODEL), trn(1), const((1, D_MODEL)), const((ROT_HALF, 1)),
                  const((D_MODEL, n_nat)), const((n_t, D_MODEL))],
        out_specs=(tok(4 * A_W), tok(2 * D_MODEL), tok(B_KV_W), tok(B_KV_W), tok(KS_AUG_W), tok(B_KV_W),
                   trn(B_Q_W), trn(VT_ROWS), trn(B_GROUPS * GATE_ROWS)),
        out_shape=out_shape,
        compiler_params=pltpu.CompilerParams(
            dimension_semantics=("parallel", "parallel"), vmem_limit_bytes=VMEM_LIMIT),
        name="proj",
    )(x, pos3, g1, invf, w_nat, w_t)


def _hgrn_kernel(hg_ref, lbp_ref, ng_ref, o_ref, st_ref, *, n_chunks):
    @pl.when(pl.program_id(1) == 0)
    def _():
        st_ref[...] = jnp.zeros_like(st_ref)

    lbp = lbp_ref[...]
    e = jnp.exp(lbp - jnp.max(lbp, axis=0, keepdims=True))
    lb = e[0:1] / jnp.sum(e, axis=0, keepdims=True)
    ng = ng_ref[...]
    row = lax.broadcasted_iota(jnp.int32, (A_CHUNK, A_CHUNK), 0)
    col = lax.broadcasted_iota(jnp.int32, (A_CHUNK, A_CHUNK), 1)
    causal = row >= col
    tri = jnp.where(causal, 1.0, 0.0).astype(BF16)

    heads = [slice(h * A_DK, (h + 1) * A_DK) for h in range(A_HEADS)]

    def stage1(c):
        blk = hg_ref[0, c * A_CHUNK:(c + 1) * A_CHUNK, :]
        f = lb + (1.0 - lb) * jax.nn.sigmoid(blk[:, A_W:2 * A_W].astype(F32))
        lf = jnp.log(f)
        lf_hi = lf.astype(BF16)
        r1 = lf - lf_hi.astype(F32)
        lf_mid = r1.astype(BF16)
        lf_lo = (r1 - lf_mid.astype(F32)).astype(BF16)
        b = _dot(tri, lf_hi) + _dot(tri, lf_mid) + _dot(tri, lf_lo)
        return dict(c=c, k=1.0 - f, b=b)

    def stage2(s):
        c, k, b = s["c"], s["k"], s["b"]
        blk = hg_ref[0, c * A_CHUNK:(c + 1) * A_CHUNK, :]
        dec = jnp.exp(b[A_CHUNK - 1:A_CHUNK, :])
        k_dec = k * jnp.exp(-b)
        return dict(c=c, v=blk[:, 2 * A_W:3 * A_W],
                    q_dec=(blk[:, 0:A_W].astype(F32) * jnp.exp(b)).astype(BF16),
                    k_dec=k_dec.astype(BF16), k_til=(k_dec * dec).astype(BF16), dec=dec)

    def stage3(s):
        s = dict(s)
        attn = [jnp.where(causal, _dot_nt(s["q_dec"][:, sl], s["k_dec"][:, sl]), 0.0).astype(BF16)
                for sl in heads]
        s["o_intra"] = [_dot(attn[h], s["v"][:, sl]) for h, sl in enumerate(heads)]
        s["kv_t"] = [_dot_tn(s["v"][:, sl], s["k_til"][:, sl]) for sl in heads]
        return s

    def stage4(s, state):
        c = s["c"]
        g = hg_ref[0, c * A_CHUNK:(c + 1) * A_CHUNK, 3 * A_W:4 * A_W].astype(F32)
        new_state = []
        for h, sl in enumerate(heads):
            o = s["o_intra"][h] + _dot_nt(s["q_dec"][:, sl], state[h].astype(BF16))
            new_state.append(state[h] * s["dec"][:, sl] + s["kv_t"][h])
            gh = g[:, sl]
            y = _rms(o, ng) * (gh * jax.nn.sigmoid(gh))
            o_ref[0, c * A_CHUNK:(c + 1) * A_CHUNK, sl] = y.astype(BF16)
        return new_state

    state = [st_ref[h] for h in range(A_HEADS)]
    pipe = [None, None, None]
    for t in range(n_chunks + 3):
        if pipe[2] is not None:
            state = stage4(pipe[2], state)
        pipe[2] = stage3(pipe[1]) if pipe[1] is not None else None
        pipe[1] = stage2(pipe[0]) if pipe[0] is not None else None
        pipe[0] = stage1(t) if t < n_chunks else None
    for h in range(A_HEADS):
        st_ref[h] = state[h]


def _hgrn(hg, lb_param, norm_g):
    bsz, s, _ = hg.shape
    tt = min(512, s)
    return pl.pallas_call(
        functools.partial(_hgrn_kernel, n_chunks=tt // A_CHUNK),
        grid=(bsz, s // tt),
        in_specs=[pl.BlockSpec((1, tt, 4 * A_W), lambda b, t: (b, t, 0)),
                  pl.BlockSpec(lb_param.shape, lambda b, t: (0, 0)),
                  pl.BlockSpec((1, A_DV), lambda b, t: (0, 0))],
        out_specs=pl.BlockSpec((1, tt, A_W), lambda b, t: (b, t, 0)),
        out_shape=jax.ShapeDtypeStruct((bsz, s, A_W), BF16),
        scratch_shapes=[pltpu.VMEM((A_HEADS, A_DV, A_DK), F32)],
        compiler_params=pltpu.CompilerParams(
            dimension_semantics=("parallel", "arbitrary"), vmem_limit_bytes=VMEM_LIMIT),
        name="hgrn",
    )(hg, lb_param, norm_g)


def _compress_kernel(kt_ref, vt_ref, pek_ref, pev_ref, w1k_ref, w1v_ref, w2k_ref, w2vt_ref,
                     kc_ref, vct_ref, *, n_blk):
    def hidden(t_ref, pe_ref, w1_ref):
        rows = jnp.concatenate(
            [t_ref[0, pl.ds(l, n_blk, stride=CMP_STRIDE), :] for l in range(CMP_STRIDE)], axis=1)
        halves = [(rows + pe_ref[a:a + 1, :]).astype(BF16) for a in range(2)]
        out = []
        for g in range(B_GROUPS):
            pre = _dot(halves[0], w1_ref[g, 0]) + pltpu.roll(_dot(halves[1], w1_ref[g, 1]), n_blk - 1, 0)
            out.append(jax.nn.gelu(pre).astype(BF16))
        return out

    valid_r = lax.broadcasted_iota(jnp.int32, (n_blk, B_KV_W), 0) < n_blk - 1
    valid_c = lax.broadcasted_iota(jnp.int32, (B_HD, n_blk), 1) < n_blk - 1
    hk = hidden(kt_ref, pek_ref, w1k_ref)
    hv = hidden(vt_ref, pev_ref, w1v_ref)
    kc = jnp.zeros((n_blk, B_KV_W), F32)
    for g in range(B_GROUPS):
        kc = kc + _dot(hk[g], w2k_ref[g])
        vct = _dot_nt(w2vt_ref[...], hv[g])
        vct_ref[0, g * B_HD:(g + 1) * B_HD, :] = jnp.where(valid_c, vct, 0.0).astype(BF16)
    kc_ref[0] = jnp.where(valid_r, kc, 0.0).astype(BF16)


def _compress(kc_tok, vc_tok, pek, pev, w1k, w1v, w2k, w2vt):
    bsz, s, _ = kc_tok.shape
    n_blk = s // CMP_STRIDE
    full = lambda a: pl.BlockSpec(a.shape, lambda b: (0,) * a.ndim)
    tokspec = pl.BlockSpec((1, s, B_KV_W), lambda b: (b, 0, 0))
    return pl.pallas_call(
        functools.partial(_compress_kernel, n_blk=n_blk),
        grid=(bsz,),
        in_specs=[tokspec, tokspec, full(pek), full(pev), full(w1k), full(w1v), full(w2k), full(w2vt)],
        out_specs=(pl.BlockSpec((1, n_blk, B_KV_W), lambda b: (b, 0, 0)),
                   pl.BlockSpec((1, B_KV_W, n_blk), lambda b: (b, 0, 0))),
        out_shape=(jax.ShapeDtypeStruct((bsz, n_blk, B_KV_W), BF16),
                   jax.ShapeDtypeStruct((bsz, B_KV_W, n_blk), BF16)),
        compiler_params=pltpu.CompilerParams(
            dimension_semantics=("parallel",), vmem_limit_bytes=VMEM_LIMIT),
        name="compress",
    )(kc_tok, vc_tok, pek, pev, w1k, w1v, w2k, w2vt)


KV_TILE = 512
WIN_KEYS = WINDOW + Q_BLOCK
QW = B_HPG * Q_BLOCK
LW = B_GROUPS * QW
STRIP = 256
LOOKAHEAD = 4


def _block_rank(imp, n_slc, i_range):
    sub = lax.broadcasted_iota(jnp.int32, (8, Q_BLOCK), 0)
    rank = jnp.zeros((n_slc, Q_BLOCK), F32)
    for i in i_range:
        ri = imp[i:i + 1, :]
        parts = []
        for r in range(n_slc // 8):
            blk = imp[8 * r:8 * r + 8, :]
            gt = jnp.where(ri > blk, 1.0, 0.0)
            ge = jnp.where(ri >= blk, 1.0, 0.0)
            if 8 * r + 7 <= i:
                parts.append(gt)
            elif 8 * r > i:
                parts.append(ge)
            else:
                parts.append(jnp.where(sub > i - 8 * r, ge, gt))
        rank = rank + jnp.concatenate(parts, axis=0)
    return rank


def _nsa_kernel(*refs, n_cmp_rows, n_slc, n_key_tiles):
    qi = pl.program_id(1)
    n_tiles = (qi * Q_BLOCK + KV_TILE - 1) // KV_TILE
    for c in range(n_key_tiles + 1):
        pl.when(n_tiles == c)(functools.partial(
            _nsa_step, c, qi, *refs, n_cmp_rows=n_cmp_rows, n_slc=n_slc))


def _nsa_step(n_tiles, qi, qt_ref, gt_ref, kc_ref, vct_ref, ks_ref, kw_ref, vt_ref, ovt_ref,
              o_ref, rhs_ref, *, n_cmp_rows, n_slc):
    q0 = qi * Q_BLOCK
    n_blk = min(n_slc, 16 * ((KV_TILE // SEL_LEN * n_tiles + Q_BLOCK // SEL_LEN + 15) // 16))
    n_cmp = min(n_cmp_rows, 128 * (((KV_TILE * n_tiles + Q_BLOCK) // CMP_STRIDE + 127) // 128))
    steady = KV_TILE * (n_tiles - 1) >= WINDOW

    qblk = qt_ref[0]
    zq = jnp.zeros((B_HD, QW), BF16)
    qrows = []
    for g in range(B_GROUPS):
        qg = jnp.concatenate(
            [qblk[(g * B_HPG + h) * B_HD:(g * B_HPG + h + 1) * B_HD, :] for h in range(B_HPG)], axis=1)
        qrows.append(jnp.concatenate([qg if gg == g else zq for gg in range(B_GROUPS)], axis=1))
    qbd = jnp.concatenate(qrows, axis=0)

    def tq(rows):
        return lax.broadcasted_iota(jnp.int32, (rows, LW), 1) & (Q_BLOCK - 1)

    qs = pl.multiple_of(q0, Q_BLOCK)
    ws = pl.multiple_of(jnp.maximum(q0 - WINDOW, 0), Q_BLOCK)
    sc = _dot(kc_ref[0, 0:n_cmp, :], qbd)
    sw = _dot(kw_ref[0, pl.ds(ws, WIN_KEYS), :], qbd)
    s_d = _dot(ks_ref[0, pl.ds(qs, Q_BLOCK), 0:B_KV_W], qbd)

    crow = lax.broadcasted_iota(jnp.int32, (n_cmp, LW), 0)
    okc = CMP_STRIDE * crow + (CMP_LEN - 1) <= q0 + tq(n_cmp)
    sc = jnp.where(okc, sc, NEG)
    pc = jnp.where(okc, jnp.exp2(sc - jnp.max(sc, axis=0, keepdims=True)), 0.0)
    pc = pc / jnp.maximum(jnp.sum(pc, axis=0, keepdims=True), 1e-30)
    pcb = pc.astype(BF16)
    oc = [_dot(vct_ref[0, g * B_HD:(g + 1) * B_HD, 0:n_cmp], pcb[:, g * QW:(g + 1) * QW])
          for g in range(B_GROUPS)]

    def strip_update(state, s, v):
        m, acc = state
        m_new = jnp.maximum(m, jnp.max(s, axis=0, keepdims=True))
        pb = jnp.exp2(s - m_new).astype(BF16)
        return m_new, jnp.exp2(m - m_new) * acc + _dot(v, pb)

    def v_block(j, start, size):
        return vt_ref[0, j * V_BLK:(j + 1) * V_BLK, pl.ds(start, size)]

    n_str = LW // STRIP
    grp = lambda j: j * STRIP // QW
    lanes = lambda j: slice(j * STRIP, (j + 1) * STRIP)
    fresh = lambda: (jnp.full((1, STRIP), NEG, F32), jnp.zeros((V_BLK, STRIP), F32))
    row_q = lax.broadcasted_iota(jnp.int32, (Q_BLOCK, STRIP), 0)
    tq_q = lax.broadcasted_iota(jnp.int32, (Q_BLOCK, STRIP), 1) & (Q_BLOCK - 1)

    jrow = lax.broadcasted_iota(jnp.int32, (n_blk, Q_BLOCK), 0)
    cur = (q0 + lax.broadcasted_iota(jnp.int32, (n_blk, Q_BLOCK), 1)) >> SEL_SHIFT
    causal_blk = jrow <= cur
    forced = (jrow == 0) | (jrow == cur) | (jrow == cur - 1)
    ovt = ovt_ref[0:n_blk, 0:n_cmp]
    imps = []
    for g in range(B_GROUPS):
        psum = pc[:, g * QW:g * QW + Q_BLOCK]
        for h in range(1, B_HPG):
            psum = psum + pc[:, g * QW + h * Q_BLOCK:g * QW + (h + 1) * Q_BLOCK]
        p_hi = psum.astype(BF16)
        p_lo = (psum - p_hi.astype(F32)).astype(BF16)
        imp = _dot(ovt, p_hi) + _dot(ovt, p_lo)
        imp = jnp.where(forced & causal_blk, FORCE_SCORE, imp)
        imps.append(jnp.where(causal_blk, imp, NEG))

    biases = []
    for g in range(B_GROUPS):
        sel = (_block_rank(imps[g], n_blk, range(n_blk)) < float(min(SEL_TOPK, n_slc))) & causal_blk
        sel = sel & (jrow // (Q_BLOCK // SEL_LEN) != qi)
        biases += [jnp.where(sel, 0.0, NEG).astype(BF16)] * B_HPG
    bias = jnp.concatenate(biases, axis=1)
    rhs_ref[...] = jnp.concatenate(
        [qbd, bias, jnp.zeros((KS_AUG_W - B_KV_W - n_blk, LW), BF16)], axis=0)

    states = [
        strip_update(fresh(), jnp.where(row_q <= tq_q, s_d[:, lanes(j)], NEG), v_block(grp(j), qs, Q_BLOCK))
        for j in range(n_str)]

    order = [(kt, j) for kt in range(n_tiles) for j in range(n_str)]

    def strip_scores(kt, j):
        return _dot(ks_ref[0, kt * KV_TILE:(kt + 1) * KV_TILE, :], rhs_ref[:, lanes(j)])

    pending = [strip_scores(*order[i]) for i in range(min(LOOKAHEAD, len(order)))]
    for i, (kt, j) in enumerate(order):
        s = pending.pop(0)
        if i + LOOKAHEAD < len(order):
            pending.append(strip_scores(*order[i + LOOKAHEAD]))
        states[j] = strip_update(states[j], s, v_block(grp(j), kt * KV_TILE, KV_TILE))

    acc_w = []
    for j in range(n_str):
        swj = sw[:, lanes(j)]
        if steady:
            lo = jnp.where(row_q > tq_q, swj[0:Q_BLOCK], NEG)
            hi = jnp.where(row_q <= tq_q, swj[WINDOW:WIN_KEYS], NEG)
            swj = jnp.concatenate([lo, swj[Q_BLOCK:WINDOW], hi], axis=0)
        else:
            krow = lax.broadcasted_iota(jnp.int32, (WIN_KEYS, STRIP), 0)
            tqw = lax.broadcasted_iota(jnp.int32, (WIN_KEYS, STRIP), 1) & (Q_BLOCK - 1)
            dist = (q0 - ws) + tqw - krow
            swj = jnp.where(lax.bitcast_convert_type(dist, jnp.uint32) < jnp.uint32(WINDOW), swj, NEG)
        acc_w.append(strip_update(fresh(), swj, v_block(B_GROUPS + grp(j), ws, WIN_KEYS))[1])

    per_g = n_str // B_GROUPS
    acc_s = [jnp.concatenate([states[g * per_g + u][1] for u in range(per_g)], axis=1) for g in range(B_GROUPS)]
    acc_w = [jnp.concatenate(acc_w[g * per_g:(g + 1) * per_g], axis=1) for g in range(B_GROUPS)]

    gts = gt_ref[0]
    for g in range(B_GROUPS):
        o_s = acc_s[g][0:B_HD] / acc_s[g][B_HD:B_HD + 1]
        o_w = acc_w[g][0:B_HD] / acc_w[g][B_HD:B_HD + 1]
        for h in range(B_HPG):
            sl = slice(h * Q_BLOCK, (h + 1) * Q_BLOCK)
            r = g * GATE_ROWS + 3 * h
            o_h = gts[r:r + 1] * oc[g][:, sl] + gts[r + 1:r + 2] * o_s[:, sl] + gts[r + 2:r + 3] * o_w[:, sl]
            o_ref[0, (g * B_HPG + h) * B_HD:(g * B_HPG + h + 1) * B_HD, :] = o_h.astype(BF16)


def _nsa(qt, gt, kc, vct, ks, kw, vt, ovt):
    bsz, _, s = qt.shape
    n_cmp_rows = kc.shape[1]
    n_slc = s // SEL_LEN
    assert n_slc <= SEL_ONEHOT_W and s % KV_TILE == 0 and s >= WIN_KEYS
    whole = lambda a: pl.BlockSpec((1,) + a.shape[1:], lambda b, i: (b, 0, 0))
    return pl.pallas_call(
        functools.partial(_nsa_kernel, n_cmp_rows=n_cmp_rows, n_slc=n_slc, n_key_tiles=s // KV_TILE),
        grid=(bsz, s // Q_BLOCK),
        in_specs=[
            pl.BlockSpec((1, B_Q_W, Q_BLOCK), lambda b, i: (b, 0, i)),
            pl.BlockSpec((1, B_GROUPS * GATE_ROWS, Q_BLOCK), lambda b, i: (b, 0, i)),
            whole(kc), whole(vct), whole(ks), whole(kw), whole(vt),
            pl.BlockSpec(ovt.shape, lambda b, i: (0, 0)),
        ],
        out_specs=pl.BlockSpec((1, B_Q_W, Q_BLOCK), lambda b, i: (b, 0, i)),
        out_shape=jax.ShapeDtypeStruct((bsz, B_Q_W, s), BF16),
        scratch_shapes=[pltpu.VMEM((KS_AUG_W, LW), BF16)],
        compiler_params=pltpu.CompilerParams(
            dimension_semantics=("parallel", "arbitrary"), vmem_limit_bytes=VMEM_LIMIT),
        name="nsa",
    )(qt, gt, kc, vct, ks, kw, vt, ovt)


FF_CHUNK = 1024


def _tail_kernel(x_ref, ya_ref, ybt_ref, mg_ref, wa_ref, wb_ref, wo_ref, g2_ref, w1_ref, w2_ref, gf_ref, o_ref):
    pa = _dot(ya_ref[0], wa_ref[...])
    pb = _dot_tn(ybt_ref[0], wb_ref[...])
    mg = mg_ref[0].astype(F32)
    merged = jax.nn.sigmoid(mg[:, 0:D_MODEL]) * pa + jax.nn.sigmoid(mg[:, D_MODEL:2 * D_MODEL]) * pb
    h = x_ref[0] + _dot(merged.astype(BF16), wo_ref[...])
    hn = _rms(h, g2_ref[...]).astype(BF16)
    acc = jnp.zeros_like(h)
    for c in range(D_FF // FF_CHUNK):
        z = _dot(hn, w1_ref[:, c * FF_CHUNK:(c + 1) * FF_CHUNK])
        acc = acc + _dot(jnp.square(jnp.maximum(z, 0.0)).astype(BF16), w2_ref[c * FF_CHUNK:(c + 1) * FF_CHUNK, :])
    o_ref[0] = _rms(h + acc, gf_ref[...])


def _tail(x, ya, ybt, mg, wa, wb, wo, g2, w1, w2, gf):
    bsz, s, _ = x.shape
    tm = min(512, s)
    tok = lambda w: pl.BlockSpec((1, tm, w), lambda b, t: (b, t, 0))
    const = lambda a: pl.BlockSpec(a.shape, lambda b, t: (0,) * a.ndim, pipeline_mode=pl.Buffered(1))
    return pl.pallas_call(
        _tail_kernel,
        grid=(bsz, s // tm),
        in_specs=[tok(D_MODEL), tok(A_W), pl.BlockSpec((1, B_Q_W, tm), lambda b, t: (b, 0, t)), tok(2 * D_MODEL),
                  const(wa), const(wb), const(wo), const(g2), const(w1), const(w2), const(gf)],
        out_specs=tok(D_MODEL),
        out_shape=jax.ShapeDtypeStruct((bsz, s, D_MODEL), F32),
        compiler_params=pltpu.CompilerParams(
            dimension_semantics=("parallel", "parallel"), vmem_limit_bytes=VMEM_LIMIT),
        name="tail",
    )(x, ya, ybt, mg, wa, wb, wo, g2, w1, w2, gf)


def _pad_groups(a, axis):
    outs = []
    for g in range(B_GROUPS):
        pads = [(0, 0)] * a.ndim
        pads[axis] = (g * B_HD, (B_GROUPS - 1 - g) * B_HD)
        outs.append(jnp.pad(a, pads))
    return jnp.stack(outs)


def _overlap_t(s):
    n_cmp = (s - CMP_LEN) // CMP_STRIDE + 1
    n_slc = s // SEL_LEN
    cs = np.arange(n_cmp)[:, None] * CMP_STRIDE
    ss = np.arange(n_slc)[None, :] * SEL_LEN
    ov = np.clip(np.minimum(cs + CMP_LEN, ss + SEL_LEN) - np.maximum(cs, ss), 0, None) / CMP_LEN
    ovt = np.zeros((n_slc, s // CMP_STRIDE), np.float32)
    ovt[:, :n_cmp] = ov.T
    return jnp.asarray(ovt, BF16)


def kernel(x, positions, norm1_g, w_in, lb_param, hgrn_norm_g, cmp_pe_k, cmp_pe_v, cmp_w1_k, cmp_w2_k,
           cmp_w1_v, cmp_w2_v, w_br_a, w_br_b, w_out, norm2_g, w_ff1, w_ff2, final_g):
    bsz, s, _ = x.shape
    assert norm1_g.shape[0] == 1, "single-layer block"

    w = w_in[0]
    o_bq = 4 * A_W
    o_kv = o_bq + B_Q_W
    o_gate = o_kv + 6 * B_KV_W
    o_mg = o_gate + 3 * B_HEADS
    kv = lambda i: w[:, o_kv + i * B_KV_W:o_kv + (i + 1) * B_KV_W]
    w_nat = jnp.concatenate([w[:, 0:o_bq], w[:, o_mg:o_mg + 2 * D_MODEL], kv(1)], axis=1).astype(BF16)
    wg = w[:, o_gate:o_mg].reshape(D_MODEL, B_GROUPS, 3 * B_HPG)
    wg = jnp.pad(wg, ((0, 0), (0, 0), (0, GATE_ROWS - 3 * B_HPG))).reshape(D_MODEL, B_GROUPS * GATE_ROWS)
    w_t = jnp.concatenate([w[:, o_bq:o_kv], kv(0), kv(2), kv(4), kv(3), kv(5), wg], axis=1).T.astype(BF16)
    invf = (ROPE_THETA ** (-jnp.arange(0, ROT_DIM, 2, dtype=F32) / ROT_DIM)).reshape(ROT_HALF, 1)

    hg, mg, vc_tok, kc_tok, ks, kw, qt, vt, gt = _proj(
        x, positions.reshape(bsz, 1, s), norm1_g, invf, w_nat, w_t)

    ya = _hgrn(hg, lb_param, hgrn_norm_g)

    half_w = CMP_LEN // 2 * B_KV_W

    def w1_halves(w1):
        padded = _pad_groups(w1.reshape(CMP_LEN, B_HD, CMP_HIDDEN), 1)
        return padded.reshape(B_GROUPS, 2, half_w, CMP_HIDDEN).astype(BF16)

    pe_halves = lambda pe: jnp.tile(pe, (1, B_GROUPS)).reshape(2, half_w)
    kc, vct = _compress(
        kc_tok, vc_tok, pe_halves(cmp_pe_k[0]), pe_halves(cmp_pe_v[0]), w1_halves(cmp_w1_k[0]),
        w1_halves(cmp_w1_v[0]), _pad_groups(cmp_w2_k[0], 1).astype(BF16), cmp_w2_v[0].T.astype(BF16))

    ybt = _nsa(qt, gt, kc, vct, ks, kw, vt, _overlap_t(s))

    return _tail(x, ya, ybt, mg, w_br_a[0].astype(BF16), w_br_b[0].astype(BF16), w_out[0].astype(BF16),
                 norm2_g, w_ff1[0].astype(BF16), w_ff2[0].astype(BF16), final_g.reshape(1, D_MODEL))
```

```python
import functools

import jax
import jax.numpy as jnp
import numpy as np
from jax import lax
from jax.experimental import pallas as pl
from jax.experimental.pallas import tpu as pltpu

F32 = jnp.float32
BF16 = jnp.bfloat16

D_MODEL = 1024
A_HEADS = 4
A_DK = 128
A_DV = 128
A_CHUNK = 64
A_W = A_HEADS * A_DK
B_HEADS = 8
B_GROUPS = 2
B_HPG = B_HEADS // B_GROUPS
B_HD = 64
B_Q_W = B_HEADS * B_HD
B_KV_W = B_GROUPS * B_HD
CMP_LEN = 32
CMP_STRIDE = 16
CMP_HIDDEN = 256
SEL_LEN = 64
SEL_TOPK = 16
WINDOW = 512
Q_BLOCK = 128
FORCE_SCORE = 1e4
NEG = -1e30
ROPE_THETA = 500000.0
ROT_DIM = B_HD // 4
ROT_HALF = ROT_DIM // 2
D_FF = 4 * D_MODEL
EPS = 1e-6
LOG2_E = 1.4426950408889634
GATE_ROWS = 16
SEL_SHIFT = SEL_LEN.bit_length() - 1
SEL_ONEHOT_W = 128
KS_AUG_W = B_KV_W + SEL_ONEHOT_W
V_AUX = 16
V_BLK = B_HD + V_AUX
VT_ROWS = 2 * B_GROUPS * V_BLK

VMEM_LIMIT = 56 * 1024 * 1024

NT = (((1,), (1,)), ((), ()))
TN = (((0,), (0,)), ((), ()))


def _dot(a, b):
    return jnp.dot(a, b, preferred_element_type=F32)


def _dot_nt(a, b):
    return lax.dot_general(a, b, NT, preferred_element_type=F32)


def _dot_tn(a, b):
    return lax.dot_general(a, b, TN, preferred_element_type=F32)


def _rms(x, g):
    return x * lax.rsqrt(jnp.mean(x * x, axis=-1, keepdims=True) + EPS) * g


def _proj_kernel(x_ref, pos_ref, g1_ref, invf_ref, wnat_ref, wt_ref,
                 hg_ref, mg_ref, vc_ref, kc_ref, ks_ref, kw_ref, qt_ref, vt_ref, gt_ref):
    xn = _rms(x_ref[0], g1_ref[...]).astype(BF16)

    def nat(c0, c1):
        return _dot(xn, wnat_ref[:, c0:c1])

    half = 2 * A_W
    for c in range(2):
        hg_ref[0, :, c * half:(c + 1) * half] = nat(c * half, (c + 1) * half).astype(BF16)
        mg_ref[0, :, c * half:(c + 1) * half] = nat(4 * A_W + c * half, 4 * A_W + (c + 1) * half).astype(BF16)
    vc_ref[0] = nat(4 * A_W + 2 * D_MODEL, 4 * A_W + 2 * D_MODEL + B_KV_W)

    def tr(r0, r1):
        return _dot_nt(wt_ref[r0:r1, :], xn)

    ang = invf_ref[...] * pos_ref[0].astype(F32)
    cos = jnp.cos(ang)
    sin = jnp.sin(ang)

    def rope(xt, n_heads):
        pieces = []
        for h in range(n_heads):
            b0 = h * B_HD
            t1 = xt[b0:b0 + ROT_HALF]
            t2 = xt[b0 + ROT_HALF:b0 + ROT_DIM]
            pieces += [t1 * cos - t2 * sin, t2 * cos + t1 * sin, xt[b0 + ROT_DIM:b0 + B_HD]]
        return jnp.concatenate(pieces, axis=0)

    r_q, r_k, r_v = B_Q_W, B_Q_W + 3 * B_KV_W, B_Q_W + 5 * B_KV_W
    qt_ref[0] = (rope(tr(0, r_q), B_HEADS) * (B_HD ** -0.5 * LOG2_E)).astype(BF16)
    k3 = rope(tr(r_q, r_k), 3 * B_GROUPS).T
    kc_ref[0] = k3[:, 0:B_KV_W]
    kw_ref[0] = k3[:, 2 * B_KV_W:3 * B_KV_W].astype(BF16)
    tm = k3.shape[0]
    tok = pl.program_id(1) * tm + lax.broadcasted_iota(jnp.int32, (tm, SEL_ONEHOT_W), 0)
    lane = lax.broadcasted_iota(jnp.int32, (tm, SEL_ONEHOT_W), 1)
    ks_ref[0, :, 0:B_KV_W] = k3[:, B_KV_W:2 * B_KV_W].astype(BF16)
    ks_ref[0, :, B_KV_W:B_KV_W + SEL_ONEHOT_W] = jnp.where((tok >> SEL_SHIFT) == lane, 1.0, 0.0).astype(BF16)
    vt = tr(r_k, r_v)
    aux = jnp.where(lax.broadcasted_iota(jnp.int32, (V_AUX, tm), 0) == 0, 1.0, 0.0)
    vt_ref[0] = jnp.concatenate(
        [piece for j in range(2 * B_GROUPS) for piece in (vt[j * B_HD:(j + 1) * B_HD], aux)], axis=0).astype(BF16)
    gt_ref[0] = jax.nn.sigmoid(tr(r_v, r_v + B_GROUPS * GATE_ROWS))


def _proj(x, pos3, g1, invf, w_nat, w_t):
    bsz, s, _ = x.shape
    tm = min(512, s)
    n_nat = w_nat.shape[1]
    n_t = w_t.shape[0]
    tok = lambda w: pl.BlockSpec((1, tm, w), lambda b, t: (b, t, 0))
    trn = lambda r: pl.BlockSpec((1, r, tm), lambda b, t: (b, 0, t))
    const = lambda shp: pl.BlockSpec(shp, lambda b, t: (0,) * len(shp), pipeline_mode=pl.Buffered(1))
    out_shape = (
        jax.ShapeDtypeStruct((bsz, s, 4 * A_W), BF16),
        jax.ShapeDtypeStruct((bsz, s, 2 * D_MODEL), BF16),
        jax.ShapeDtypeStruct((bsz, s, B_KV_W), F32),
        jax.ShapeDtypeStruct((bsz, s, B_KV_W), F32),
        jax.ShapeDtypeStruct((bsz, s, KS_AUG_W), BF16),
        jax.ShapeDtypeStruct((bsz, s, B_KV_W), BF16),
        jax.ShapeDtypeStruct((bsz, B_Q_W, s), BF16),
        jax.ShapeDtypeStruct((bsz, VT_ROWS, s), BF16),
        jax.ShapeDtypeStruct((bsz, B_GROUPS * GATE_ROWS, s), F32),
    )
    return pl.pallas_call(
        _proj_kernel,
        grid=(bsz, s // tm),
        in_specs=[tok(D_MODEL), trn(1), const((1, D_MODEL)), const((ROT_HALF, 1)),
                  const((D_MODEL, n_nat)), const((n_t, D_MODEL))],
        out_specs=(tok(4 * A_W), tok(2 * D_MODEL), tok(B_KV_W), tok(B_KV_W), tok(KS_AUG_W), tok(B_KV_W),
                   trn(B_Q_W), trn(VT_ROWS), trn(B_GROUPS * GATE_ROWS)),
        out_shape=out_shape,
        compiler_params=pltpu.CompilerParams(
            dimension_semantics=("parallel", "parallel"), vmem_limit_bytes=VMEM_LIMIT),
        name="proj",
    )(x, pos3, g1, invf, w_nat, w_t)


def _hgrn_kernel(hg_ref, lbp_ref, ng_ref, o_ref, st_ref, *, n_chunks):
    @pl.when(pl.program_id(1) == 0)
    def _():
        st_ref[...] = jnp.zeros_like(st_ref)

    lbp = lbp_ref[...]
    e = jnp.exp(lbp - jnp.max(lbp, axis=0, keepdims=True))
    lb = e[0:1] / jnp.sum(e, axis=0, keepdims=True)
    ng = ng_ref[...]
    row = lax.broadcasted_iota(jnp.int32, (A_CHUNK, A_CHUNK), 0)
    col = lax.broadcasted_iota(jnp.int32, (A_CHUNK, A_CHUNK), 1)
    causal = row >= col
    tri = jnp.where(causal, 1.0, 0.0).astype(BF16)

    heads = [slice(h * A_DK, (h + 1) * A_DK) for h in range(A_HEADS)]

    def stage1(c):
        blk = hg_ref[0, c * A_CHUNK:(c + 1) * A_CHUNK, :]
        f = lb + (1.0 - lb) * jax.nn.sigmoid(blk[:, A_W:2 * A_W].astype(F32))
        lf = jnp.log(f)
        lf_hi = lf.astype(BF16)
        r1 = lf - lf_hi.astype(F32)
        lf_mid = r1.astype(BF16)
        lf_lo = (r1 - lf_mid.astype(F32)).astype(BF16)
        b = _dot(tri, lf_hi) + _dot(tri, lf_mid) + _dot(tri, lf_lo)
        return dict(c=c, k=1.0 - f, b=b)

    def stage2(s):
        c, k, b = s["c"], s["k"], s["b"]
        blk = hg_ref[0, c * A_CHUNK:(c + 1) * A_CHUNK, :]
        dec = jnp.exp(b[A_CHUNK - 1:A_CHUNK, :])
        k_dec = k * jnp.exp(-b)
        return dict(c=c, v=blk[:, 2 * A_W:3 * A_W],
                    q_dec=(blk[:, 0:A_W].astype(F32) * jnp.exp(b)).astype(BF16),
                    k_dec=k_dec.astype(BF16), k_til=(k_dec * dec).astype(BF16), dec=dec)

    def stage3(s):
        s = dict(s)
        attn = [jnp.where(causal, _dot_nt(s["q_dec"][:, sl], s["k_dec"][:, sl]), 0.0).astype(BF16)
                for sl in heads]
        s["o_intra"] = [_dot(attn[h], s["v"][:, sl]) for h, sl in enumerate(heads)]
        s["kv_t"] = [_dot_tn(s["v"][:, sl], s["k_til"][:, sl]) for sl in heads]
        return s

    def stage4(s, state):
        c = s["c"]
        g = hg_ref[0, c * A_CHUNK:(c + 1) * A_CHUNK, 3 * A_W:4 * A_W].astype(F32)
        new_state = []
        for h, sl in enumerate(heads):
            o = s["o_intra"][h] + _dot_nt(s["q_dec"][:, sl], state[h].astype(BF16))
            new_state.append(state[h] * s["dec"][:, sl] + s["kv_t"][h])
            gh = g[:, sl]
            y = _rms(o, ng) * (gh * jax.nn.sigmoid(gh))
            o_ref[0, c * A_CHUNK:(c + 1) * A_CHUNK, sl] = y.astype(BF16)
        return new_state

    state = [st_ref[h] for h in range(A_HEADS)]
    pipe = [None, None, None]
    for t in range(n_chunks + 3):
        if pipe[2] is not None:
            state = stage4(pipe[2], state)
        pipe[2] = stage3(pipe[1]) if pipe[1] is not None else None
        pipe[1] = stage2(pipe[0]) if pipe[0] is not None else None
        pipe[0] = stage1(t) if t < n_chunks else None
    for h in range(A_HEADS):
        st_ref[h] = state[h]


def _hgrn(hg, lb_param, norm_g):
    bsz, s, _ = hg.shape
    tt = min(512, s)
    return pl.pallas_call(
        functools.partial(_hgrn_kernel, n_chunks=tt // A_CHUNK),
        grid=(bsz, s // tt),
        in_specs=[pl.BlockSpec((1, tt, 4 * A_W), lambda b, t: (b, t, 0)),
                  pl.BlockSpec(lb_param.shape, lambda b, t: (0, 0)),
                  pl.BlockSpec((1, A_DV), lambda b, t: (0, 0))],
        out_specs=pl.BlockSpec((1, tt, A_W), lambda b, t: (b, t, 0)),
        out_shape=jax.ShapeDtypeStruct((bsz, s, A_W), BF16),
        scratch_shapes=[pltpu.VMEM((A_HEADS, A_DV, A_DK), F32)],
        compiler_params=pltpu.CompilerParams(
            dimension_semantics=("parallel", "arbitrary"), vmem_limit_bytes=VMEM_LIMIT),
        name="hgrn",
    )(hg, lb_param, norm_g)


def _compress_kernel(kt_ref, vt_ref, pek_ref, pev_ref, w1k_ref, w1v_ref, w2k_ref, w2vt_ref,
                     kc_ref, vct_ref, *, n_blk):
    def hidden(t_ref, pe_ref, w1_ref):
        rows = jnp.concatenate(
            [t_ref[0, pl.ds(l, n_blk, stride=CMP_STRIDE), :] for l in range(CMP_STRIDE)], axis=1)
        halves = [(rows + pe_ref[a:a + 1, :]).astype(BF16) for a in range(2)]
        out = []
        for g in range(B_GROUPS):
            pre = _dot(halves[0], w1_ref[g, 0]) + pltpu.roll(_dot(halves[1], w1_ref[g, 1]), n_blk - 1, 0)
            out.append(jax.nn.gelu(pre).astype(BF16))
        return out

    valid_r = lax.broadcasted_iota(jnp.int32, (n_blk, B_KV_W), 0) < n_blk - 1
    valid_c = lax.broadcasted_iota(jnp.int32, (B_HD, n_blk), 1) < n_blk - 1
    hk = hidden(kt_ref, pek_ref, w1k_ref)
    hv = hidden(vt_ref, pev_ref, w1v_ref)
    kc = jnp.zeros((n_blk, B_KV_W), F32)
    for g in range(B_GROUPS):
        kc = kc + _dot(hk[g], w2k_ref[g])
        vct = _dot_nt(w2vt_ref[...], hv[g])
        vct_ref[0, g * B_HD:(g + 1) * B_HD, :] = jnp.where(valid_c, vct, 0.0).astype(BF16)
    kc_ref[0] = jnp.where(valid_r, kc, 0.0).astype(BF16)


def _compress(kc_tok, vc_tok, pek, pev, w1k, w1v, w2k, w2vt):
    bsz, s, _ = kc_tok.shape
    n_blk = s // CMP_STRIDE
    full = lambda a: pl.BlockSpec(a.shape, lambda b: (0,) * a.ndim)
    tokspec = pl.BlockSpec((1, s, B_KV_W), lambda b: (b, 0, 0))
    return pl.pallas_call(
        functools.partial(_compress_kernel, n_blk=n_blk),
        grid=(bsz,),
        in_specs=[tokspec, tokspec, full(pek), full(pev), full(w1k), full(w1v), full(w2k), full(w2vt)],
        out_specs=(pl.BlockSpec((1, n_blk, B_KV_W), lambda b: (b, 0, 0)),
                   pl.BlockSpec((1, B_KV_W, n_blk), lambda b: (b, 0, 0))),
        out_shape=(jax.ShapeDtypeStruct((bsz, n_blk, B_KV_W), BF16),
                   jax.ShapeDtypeStruct((bsz, B_KV_W, n_blk), BF16)),
        compiler_params=pltpu.CompilerParams(
            dimension_semantics=("parallel",), vmem_limit_bytes=VMEM_LIMIT),
        name="compress",
    )(kc_tok, vc_tok, pek, pev, w1k, w1v, w2k, w2vt)


KV_TILE = 256
WIN_KEYS = WINDOW + Q_BLOCK
QW = B_HPG * Q_BLOCK
LW = B_GROUPS * QW
STRIP = 256
LOOKAHEAD = 4


def _block_rank(imp, n_slc, i_range):
    sub = lax.broadcasted_iota(jnp.int32, (8, Q_BLOCK), 0)
    rank = jnp.zeros((n_slc, Q_BLOCK), F32)
    for i in i_range:
        ri = imp[i:i + 1, :]
        parts = []
        for r in range(n_slc // 8):
            blk = imp[8 * r:8 * r + 8, :]
            gt = jnp.where(ri > blk, 1.0, 0.0)
            ge = jnp.where(ri >= blk, 1.0, 0.0)
            if 8 * r + 7 <= i:
                parts.append(gt)
            elif 8 * r > i:
                parts.append(ge)
            else:
                parts.append(jnp.where(sub > i - 8 * r, ge, gt))
        rank = rank + jnp.concatenate(parts, axis=0)
    return rank


def _nsa_kernel(*refs, n_cmp_rows, n_slc, n_key_tiles):
    qi = pl.program_id(1)
    n_tiles = (qi * Q_BLOCK + KV_TILE - 1) // KV_TILE
    for c in range(n_key_tiles + 1):
        pl.when(n_tiles == c)(functools.partial(
            _nsa_step, c, qi, *refs, n_cmp_rows=n_cmp_rows, n_slc=n_slc))


def _nsa_step(n_tiles, qi, qt_ref, gt_ref, kc_ref, vct_ref, ks_ref, kw_ref, vt_ref, ovt_ref, cend_ref,
              o_ref, rhs_ref, *, n_cmp_rows, n_slc):
    q0 = qi * Q_BLOCK
    n_blk = min(n_slc, 16 * ((KV_TILE // SEL_LEN * n_tiles + Q_BLOCK // SEL_LEN + 15) // 16))
    n_cmp = min(n_cmp_rows, 128 * (((KV_TILE * n_tiles + Q_BLOCK) // CMP_STRIDE + 127) // 128))
    steady = KV_TILE * (n_tiles - 1) >= WINDOW

    qblk = qt_ref[0]
    zq = jnp.zeros((B_HD, QW), BF16)
    qrows = []
    for g in range(B_GROUPS):
        qg = jnp.concatenate(
            [qblk[(g * B_HPG + h) * B_HD:(g * B_HPG + h + 1) * B_HD, :] for h in range(B_HPG)], axis=1)
        qrows.append(jnp.concatenate([qg if gg == g else zq for gg in range(B_GROUPS)], axis=1))
    qbd = jnp.concatenate(qrows, axis=0)

    qs = pl.multiple_of(q0, Q_BLOCK)
    ws = pl.multiple_of(jnp.maximum(q0 - WINDOW, 0), Q_BLOCK)
    sc = _dot(kc_ref[0, 0:n_cmp, :], qbd)
    sw = _dot(kw_ref[0, pl.ds(ws, WIN_KEYS), :], qbd)
    s_d = _dot(ks_ref[0, pl.ds(qs, Q_BLOCK), 0:B_KV_W], qbd)

    okc = cend_ref[0:n_cmp, :] <= q0
    sc = jnp.where(okc, sc, NEG)
    pc = jnp.exp2(sc - jnp.max(sc, axis=0, keepdims=True))
    l_c = jnp.sum(pc, axis=0, keepdims=True)
    if n_tiles == 0:
        pc = jnp.where(okc, pc, 0.0)
        l_c = jnp.maximum(jnp.sum(pc, axis=0, keepdims=True), 1e-30)
    pc = pc / l_c
    pcb = pc.astype(BF16)
    oc = [_dot(vct_ref[0, g * B_HD:(g + 1) * B_HD, 0:n_cmp], pcb[:, g * QW:(g + 1) * QW])
          for g in range(B_GROUPS)]

    def strip_update(state, s, v):
        m, acc = state
        m_new = jnp.maximum(m, jnp.max(s, axis=0, keepdims=True))
        pb = jnp.exp2(s - m_new).astype(BF16)
        return m_new, jnp.exp2(m - m_new) * acc + _dot(v, pb)

    def v_block(j, start, size):
        return vt_ref[0, j * V_BLK:(j + 1) * V_BLK, pl.ds(start, size)]

    n_str = LW // STRIP
    grp = lambda j: j * STRIP // QW
    lanes = lambda j: slice(j * STRIP, (j + 1) * STRIP)
    fresh = lambda: (jnp.full((1, STRIP), NEG, F32), jnp.zeros((V_BLK, STRIP), F32))
    row_q = lax.broadcasted_iota(jnp.int32, (Q_BLOCK, STRIP), 0)
    tq_q = lax.broadcasted_iota(jnp.int32, (Q_BLOCK, STRIP), 1) & (Q_BLOCK - 1)

    jrow = lax.broadcasted_iota(jnp.int32, (n_blk, Q_BLOCK), 0)
    cur = (q0 + lax.broadcasted_iota(jnp.int32, (n_blk, Q_BLOCK), 1)) >> SEL_SHIFT
    causal_blk = jrow <= cur
    forced = (jrow == 0) | (jrow == cur) | (jrow == cur - 1)
    ovt = ovt_ref[0:n_blk, 0:n_cmp]
    imps = []
    for g in range(B_GROUPS):
        psum = pc[:, g * QW:g * QW + Q_BLOCK]
        for h in range(1, B_HPG):
            psum = psum + pc[:, g * QW + h * Q_BLOCK:g * QW + (h + 1) * Q_BLOCK]
        p_hi = psum.astype(BF16)
        p_lo = (psum - p_hi.astype(F32)).astype(BF16)
        imp = _dot(ovt, p_hi) + _dot(ovt, p_lo)
        imp = jnp.where(forced & causal_blk, FORCE_SCORE, imp)
        imps.append(jnp.where(causal_blk, imp, NEG))

    biases = []
    for g in range(B_GROUPS):
        sel = (_block_rank(imps[g], n_blk, range(n_blk)) < float(min(SEL_TOPK, n_slc))) & causal_blk
        sel = sel & (jrow // (Q_BLOCK // SEL_LEN) != qi)
        biases += [jnp.where(sel, 0.0, NEG).astype(BF16)] * B_HPG
    bias = jnp.concatenate(biases, axis=1)
    rhs_ref[...] = jnp.concatenate(
        [qbd, bias, jnp.zeros((KS_AUG_W - B_KV_W - n_blk, LW), BF16)], axis=0)

    states = [
        strip_update(fresh(), jnp.where(row_q <= tq_q, s_d[:, lanes(j)], NEG), v_block(grp(j), qs, Q_BLOCK))
        for j in range(n_str)]

    order = [(kt, j) for kt in range(n_tiles) for j in range(n_str)]

    def strip_scores(kt, j):
        return _dot(ks_ref[0, kt * KV_TILE:(kt + 1) * KV_TILE, :], rhs_ref[:, lanes(j)])

    pending = [strip_scores(*order[i]) for i in range(min(LOOKAHEAD, len(order)))]
    for i, (kt, j) in enumerate(order):
        s = pending.pop(0)
        if i + LOOKAHEAD < len(order):
            pending.append(strip_scores(*order[i + LOOKAHEAD]))
        states[j] = strip_update(states[j], s, v_block(grp(j), kt * KV_TILE, KV_TILE))

    acc_w = []
    for j in range(n_str):
        swj = sw[:, lanes(j)]
        if steady:
            lo = jnp.where(row_q > tq_q, swj[0:Q_BLOCK], NEG)
            hi = jnp.where(row_q <= tq_q, swj[WINDOW:WIN_KEYS], NEG)
            swj = jnp.concatenate([lo, swj[Q_BLOCK:WINDOW], hi], axis=0)
        else:
            krow = lax.broadcasted_iota(jnp.int32, (WIN_KEYS, STRIP), 0)
            tqw = lax.broadcasted_iota(jnp.int32, (WIN_KEYS, STRIP), 1) & (Q_BLOCK - 1)
            dist = (q0 - ws) + tqw - krow
            swj = jnp.where(lax.bitcast_convert_type(dist, jnp.uint32) < jnp.uint32(WINDOW), swj, NEG)
        acc_w.append(strip_update(fresh(), swj, v_block(B_GROUPS + grp(j), ws, WIN_KEYS))[1])

    per_g = n_str // B_GROUPS
    acc_s = [jnp.concatenate([states[g * per_g + u][1] for u in range(per_g)], axis=1) for g in range(B_GROUPS)]
    acc_w = [jnp.concatenate(acc_w[g * per_g:(g + 1) * per_g], axis=1) for g in range(B_GROUPS)]

    gts = gt_ref[0]
    for g in range(B_GROUPS):
        o_s = acc_s[g][0:B_HD] / acc_s[g][B_HD:B_HD + 1]
        o_w = acc_w[g][0:B_HD] / acc_w[g][B_HD:B_HD + 1]
        for h in range(B_HPG):
            sl = slice(h * Q_BLOCK, (h + 1) * Q_BLOCK)
            r = g * GATE_ROWS + 3 * h
            o_h = gts[r:r + 1] * oc[g][:, sl] + gts[r + 1:r + 2] * o_s[:, sl] + gts[r + 2:r + 3] * o_w[:, sl]
            o_ref[0, (g * B_HPG + h) * B_HD:(g * B_HPG + h + 1) * B_HD, :] = o_h.astype(BF16)


def _nsa(qt, gt, kc, vct, ks, kw, vt, ovt):
    bsz, _, s = qt.shape
    n_cmp_rows = kc.shape[1]
    n_slc = s // SEL_LEN
    assert n_slc <= SEL_ONEHOT_W and s % KV_TILE == 0 and s >= WIN_KEYS
    whole = lambda a: pl.BlockSpec((1,) + a.shape[1:], lambda b, i: (b, 0, 0))
    cend = jnp.asarray((np.arange(n_cmp_rows)[:, None] * CMP_STRIDE + CMP_LEN - 1)
                       - (np.arange(LW)[None, :] % Q_BLOCK), jnp.int32)
    return pl.pallas_call(
        functools.partial(_nsa_kernel, n_cmp_rows=n_cmp_rows, n_slc=n_slc, n_key_tiles=s // KV_TILE),
        grid=(bsz, s // Q_BLOCK),
        in_specs=[
            pl.BlockSpec((1, B_Q_W, Q_BLOCK), lambda b, i: (b, 0, i)),
            pl.BlockSpec((1, B_GROUPS * GATE_ROWS, Q_BLOCK), lambda b, i: (b, 0, i)),
            whole(kc), whole(vct), whole(ks), whole(kw), whole(vt),
            pl.BlockSpec(ovt.shape, lambda b, i: (0, 0)),
            pl.BlockSpec(cend.shape, lambda b, i: (0, 0)),
        ],
        out_specs=pl.BlockSpec((1, B_Q_W, Q_BLOCK), lambda b, i: (b, 0, i)),
        out_shape=jax.ShapeDtypeStruct((bsz, B_Q_W, s), BF16),
        scratch_shapes=[pltpu.VMEM((KS_AUG_W, LW), BF16)],
        compiler_params=pltpu.CompilerParams(
            dimension_semantics=("parallel", "arbitrary"), vmem_limit_bytes=VMEM_LIMIT),
        name="nsa",
    )(qt, gt, kc, vct, ks, kw, vt, ovt, cend)


FF_CHUNK = 1024


def _tail_kernel(x_ref, ya_ref, ybt_ref, mg_ref, wa_ref, wb_ref, wo_ref, g2_ref, w1_ref, w2_ref, gf_ref, o_ref):
    pa = _dot(ya_ref[0], wa_ref[...])
    pb = _dot_tn(ybt_ref[0], wb_ref[...])
    mg = mg_ref[0].astype(F32)
    merged = jax.nn.sigmoid(mg[:, 0:D_MODEL]) * pa + jax.nn.sigmoid(mg[:, D_MODEL:2 * D_MODEL]) * pb
    h = x_ref[0] + _dot(merged.astype(BF16), wo_ref[...])
    hn = _rms(h, g2_ref[...]).astype(BF16)
    acc = jnp.zeros_like(h)
    for c in range(D_FF // FF_CHUNK):
        z = _dot(hn, w1_ref[:, c * FF_CHUNK:(c + 1) * FF_CHUNK])
        acc = acc + _dot(jnp.square(jnp.maximum(z, 0.0)).astype(BF16), w2_ref[c * FF_CHUNK:(c + 1) * FF_CHUNK, :])
    o_ref[0] = _rms(h + acc, gf_ref[...])


def _tail(x, ya, ybt, mg, wa, wb, wo, g2, w1, w2, gf):
    bsz, s, _ = x.shape
    tm = min(512, s)
    tok = lambda w: pl.BlockSpec((1, tm, w), lambda b, t: (b, t, 0))
    const = lambda a: pl.BlockSpec(a.shape, lambda b, t: (0,) * a.ndim, pipeline_mode=pl.Buffered(1))
    return pl.pallas_call(
        _tail_kernel,
        grid=(bsz, s // tm),
        in_specs=[tok(D_MODEL), tok(A_W), pl.BlockSpec((1, B_Q_W, tm), lambda b, t: (b, 0, t)), tok(2 * D_MODEL),
                  const(wa), const(wb), const(wo), const(g2), const(w1), const(w2), const(gf)],
        out_specs=tok(D_MODEL),
        out_shape=jax.ShapeDtypeStruct((bsz, s, D_MODEL), F32),
        compiler_params=pltpu.CompilerParams(
            dimension_semantics=("parallel", "parallel"), vmem_limit_bytes=VMEM_LIMIT),
        name="tail",
    )(x, ya, ybt, mg, wa, wb, wo, g2, w1, w2, gf)


def _pad_groups(a, axis):
    outs = []
    for g in range(B_GROUPS):
        pads = [(0, 0)] * a.ndim
        pads[axis] = (g * B_HD, (B_GROUPS - 1 - g) * B_HD)
        outs.append(jnp.pad(a, pads))
    return jnp.stack(outs)


def _overlap_t(s):
    n_cmp = (s - CMP_LEN) // CMP_STRIDE + 1
    n_slc = s // SEL_LEN
    cs = np.arange(n_cmp)[:, None] * CMP_STRIDE
    ss = np.arange(n_slc)[None, :] * SEL_LEN
    ov = np.clip(np.minimum(cs + CMP_LEN, ss + SEL_LEN) - np.maximum(cs, ss), 0, None) / CMP_LEN
    ovt = np.zeros((n_slc, s // CMP_STRIDE), np.float32)
    ovt[:, :n_cmp] = ov.T
    return jnp.asarray(ovt, BF16)


def kernel(x, positions, norm1_g, w_in, lb_param, hgrn_norm_g, cmp_pe_k, cmp_pe_v, cmp_w1_k, cmp_w2_k,
           cmp_w1_v, cmp_w2_v, w_br_a, w_br_b, w_out, norm2_g, w_ff1, w_ff2, final_g):
    bsz, s, _ = x.shape
    assert norm1_g.shape[0] == 1, "single-layer block"

    w = w_in[0]
    o_bq = 4 * A_W
    o_kv = o_bq + B_Q_W
    o_gate = o_kv + 6 * B_KV_W
    o_mg = o_gate + 3 * B_HEADS
    kv = lambda i: w[:, o_kv + i * B_KV_W:o_kv + (i + 1) * B_KV_W]
    w_nat = jnp.concatenate([w[:, 0:o_bq], w[:, o_mg:o_mg + 2 * D_MODEL], kv(1)], axis=1).astype(BF16)
    wg = w[:, o_gate:o_mg].reshape(D_MODEL, B_GROUPS, 3 * B_HPG)
    wg = jnp.pad(wg, ((0, 0), (0, 0), (0, GATE_ROWS - 3 * B_HPG))).reshape(D_MODEL, B_GROUPS * GATE_ROWS)
    w_t = jnp.concatenate([w[:, o_bq:o_kv], kv(0), kv(2), kv(4), kv(3), kv(5), wg], axis=1).T.astype(BF16)
    invf = (ROPE_THETA ** (-jnp.arange(0, ROT_DIM, 2, dtype=F32) / ROT_DIM)).reshape(ROT_HALF, 1)

    hg, mg, vc_tok, kc_tok, ks, kw, qt, vt, gt = _proj(
        x, positions.reshape(bsz, 1, s), norm1_g, invf, w_nat, w_t)

    ya = _hgrn(hg, lb_param, hgrn_norm_g)

    half_w = CMP_LEN // 2 * B_KV_W

    def w1_halves(w1):
        padded = _pad_groups(w1.reshape(CMP_LEN, B_HD, CMP_HIDDEN), 1)
        return padded.reshape(B_GROUPS, 2, half_w, CMP_HIDDEN).astype(BF16)

    pe_halves = lambda pe: jnp.tile(pe, (1, B_GROUPS)).reshape(2, half_w)
    kc, vct = _compress(
        kc_tok, vc_tok, pe_halves(cmp_pe_k[0]), pe_halves(cmp_pe_v[0]), w1_halves(cmp_w1_k[0]),
        w1_halves(cmp_w1_v[0]), _pad_groups(cmp_w2_k[0], 1).astype(BF16), cmp_w2_v[0].T.astype(BF16))

    ybt = _nsa(qt, gt, kc, vct, ks, kw, vt, _overlap_t(s))

    return _tail(x, ya, ybt, mg, w_br_a[0].astype(BF16), w_br_b[0].astype(BF16), w_out[0].astype(BF16),
                 norm2_g, w_ff1[0].astype(BF16), w_ff2[0].astype(BF16), final_g.reshape(1, D_MODEL))
```

```python
import functools

import jax
import jax.numpy as jnp
import numpy as np
from jax import lax
from jax.experimental import pallas as pl
from jax.experimental.pallas import tpu as pltpu

F32 = jnp.float32
BF16 = jnp.bfloat16

D_MODEL = 1024
A_HEADS = 4
A_DK = 128
A_DV = 128
A_CHUNK = 64
A_W = A_HEADS * A_DK
B_HEADS = 8
B_GROUPS = 2
B_HPG = B_HEADS // B_GROUPS
B_HD = 64
B_Q_W = B_HEADS * B_HD
B_KV_W = B_GROUPS * B_HD
CMP_LEN = 32
CMP_STRIDE = 16
CMP_HIDDEN = 256
SEL_LEN = 64
SEL_TOPK = 16
WINDOW = 512
Q_BLOCK = 128
FORCE_SCORE = 1e4
NEG = -1e30
ROPE_THETA = 500000.0
ROT_DIM = B_HD // 4
ROT_HALF = ROT_DIM // 2
D_FF = 4 * D_MODEL
EPS = 1e-6
LOG2_E = 1.4426950408889634
GATE_ROWS = 16
SEL_SHIFT = SEL_LEN.bit_length() - 1
SEL_ONEHOT_W = 128
KS_AUG_W = B_KV_W + SEL_ONEHOT_W
V_AUX = 16
V_BLK = B_HD + V_AUX
VT_ROWS = 2 * B_GROUPS * V_BLK

VMEM_LIMIT = 56 * 1024 * 1024

NT = (((1,), (1,)), ((), ()))
TN = (((0,), (0,)), ((), ()))


def _dot(a, b):
    return jnp.dot(a, b, preferred_element_type=F32)


def _dot_nt(a, b):
    return lax.dot_general(a, b, NT, preferred_element_type=F32)


def _dot_tn(a, b):
    return lax.dot_general(a, b, TN, preferred_element_type=F32)


def _rms(x, g):
    return x * lax.rsqrt(jnp.mean(x * x, axis=-1, keepdims=True) + EPS) * g


def _proj_kernel(x_ref, pos_ref, g1_ref, invf_ref, wnat_ref, wt_ref,
                 hg_ref, mg_ref, vc_ref, kc_ref, ks_ref, kw_ref, qt_ref, vt_ref, gt_ref):
    xn = _rms(x_ref[0], g1_ref[...]).astype(BF16)

    def nat(c0, c1):
        return _dot(xn, wnat_ref[:, c0:c1])

    half = 2 * A_W
    for c in range(2):
        hg_ref[0, :, c * half:(c + 1) * half] = nat(c * half, (c + 1) * half).astype(BF16)
        mg_ref[0, :, c * half:(c + 1) * half] = nat(4 * A_W + c * half, 4 * A_W + (c + 1) * half).astype(BF16)
    vc_ref[0] = nat(4 * A_W + 2 * D_MODEL, 4 * A_W + 2 * D_MODEL + B_KV_W)

    def tr(r0, r1):
        return _dot_nt(wt_ref[r0:r1, :], xn)

    ang = invf_ref[...] * pos_ref[0].astype(F32)
    cos = jnp.cos(ang)
    sin = jnp.sin(ang)

    def rope(xt, n_heads):
        pieces = []
        for h in range(n_heads):
            b0 = h * B_HD
            t1 = xt[b0:b0 + ROT_HALF]
            t2 = xt[b0 + ROT_HALF:b0 + ROT_DIM]
            pieces += [t1 * cos - t2 * sin, t2 * cos + t1 * sin, xt[b0 + ROT_DIM:b0 + B_HD]]
        return jnp.concatenate(pieces, axis=0)

    r_q, r_k, r_v = B_Q_W, B_Q_W + 3 * B_KV_W, B_Q_W + 5 * B_KV_W
    qt_ref[0] = (rope(tr(0, r_q), B_HEADS) * (B_HD ** -0.5 * LOG2_E)).astype(BF16)
    k3 = rope(tr(r_q, r_k), 3 * B_GROUPS).T
    kc_ref[0] = k3[:, 0:B_KV_W]
    kw_ref[0] = k3[:, 2 * B_KV_W:3 * B_KV_W].astype(BF16)
    tm = k3.shape[0]
    tok = pl.program_id(1) * tm + lax.broadcasted_iota(jnp.int32, (tm, SEL_ONEHOT_W), 0)
    lane = lax.broadcasted_iota(jnp.int32, (tm, SEL_ONEHOT_W), 1)
    ks_ref[0, :, 0:B_KV_W] = k3[:, B_KV_W:2 * B_KV_W].astype(BF16)
    ks_ref[0, :, B_KV_W:B_KV_W + SEL_ONEHOT_W] = jnp.where((tok >> SEL_SHIFT) == lane, 1.0, 0.0).astype(BF16)
    vt = tr(r_k, r_v)
    aux = jnp.where(lax.broadcasted_iota(jnp.int32, (V_AUX, tm), 0) == 0, 1.0, 0.0)
    vt_ref[0] = jnp.concatenate(
        [piece for j in range(2 * B_GROUPS) for piece in (vt[j * B_HD:(j + 1) * B_HD], aux)], axis=0).astype(BF16)
    gt_ref[0] = jax.nn.sigmoid(tr(r_v, r_v + B_GROUPS * GATE_ROWS))


def _proj(x, pos3, g1, invf, w_nat, w_t):
    bsz, s, _ = x.shape
    tm = min(512, s)
    n_nat = w_nat.shape[1]
    n_t = w_t.shape[0]
    tok = lambda w: pl.BlockSpec((1, tm, w), lambda b, t: (b, t, 0))
    trn = lambda r: pl.BlockSpec((1, r, tm), lambda b, t: (b, 0, t))
    const = lambda shp: pl.BlockSpec(shp, lambda b, t: (0,) * len(shp), pipeline_mode=pl.Buffered(1))
    out_shape = (
        jax.ShapeDtypeStruct((bsz, s, 4 * A_W), BF16),
        jax.ShapeDtypeStruct((bsz, s, 2 * D_MODEL), BF16),
        jax.ShapeDtypeStruct((bsz, s, B_KV_W), F32),
        jax.ShapeDtypeStruct((bsz, s, B_KV_W), F32),
        jax.ShapeDtypeStruct((bsz, s, KS_AUG_W), BF16),
        jax.ShapeDtypeStruct((bsz, s, B_KV_W), BF16),
        jax.ShapeDtypeStruct((bsz, B_Q_W, s), BF16),
        jax.ShapeDtypeStruct((bsz, VT_ROWS, s), BF16),
        jax.ShapeDtypeStruct((bsz, B_GROUPS * GATE_ROWS, s), F32),
    )
    return pl.pallas_call(
        _proj_kernel,
        grid=(bsz, s // tm),
        in_specs=[tok(D_MODEL), trn(1), const((1, D_MODEL)), const((ROT_HALF, 1)),
                  const((D_MODEL, n_nat)), const((n_t, D_MODEL))],
        out_specs=(tok(4 * A_W), tok(2 * D_MODEL), tok(B_KV_W), tok(B_KV_W), tok(KS_AUG_W), tok(B_KV_W),
                   trn(B_Q_W), trn(VT_ROWS), trn(B_GROUPS * GATE_ROWS)),
        out_shape=out_shape,
        compiler_params=pltpu.CompilerParams(
            dimension_semantics=("parallel", "parallel"), vmem_limit_bytes=VMEM_LIMIT),
        name="proj",
    )(x, pos3, g1, invf, w_nat, w_t)


def _hgrn_kernel(hg_ref, lbp_ref, ng_ref, o_ref, st_ref, *, n_chunks):
    @pl.when(pl.program_id(1) == 0)
    def _():
        st_ref[...] = jnp.zeros_like(st_ref)

    lbp = lbp_ref[...]
    e = jnp.exp(lbp - jnp.max(lbp, axis=0, keepdims=True))
    lb = e[0:1] / jnp.sum(e, axis=0, keepdims=True)
    ng = ng_ref[...]
    row = lax.broadcasted_iota(jnp.int32, (A_CHUNK, A_CHUNK), 0)
    col = lax.broadcasted_iota(jnp.int32, (A_CHUNK, A_CHUNK), 1)
    causal = row >= col
    tri = jnp.where(causal, 1.0, 0.0).astype(BF16)

    heads = [slice(h * A_DK, (h + 1) * A_DK) for h in range(A_HEADS)]

    def stage1(c):
        blk = hg_ref[0, c * A_CHUNK:(c + 1) * A_CHUNK, :]
        f = lb + (1.0 - lb) * jax.nn.sigmoid(blk[:, A_W:2 * A_W].astype(F32))
        lf = jnp.log(f)
        lf_hi = lf.astype(BF16)
        r1 = lf - lf_hi.astype(F32)
        lf_mid = r1.astype(BF16)
        lf_lo = (r1 - lf_mid.astype(F32)).astype(BF16)
        b = _dot(tri, lf_hi) + _dot(tri, lf_mid) + _dot(tri, lf_lo)
        return dict(c=c, k=1.0 - f, b=b)

    def stage2(s):
        c, k, b = s["c"], s["k"], s["b"]
        blk = hg_ref[0, c * A_CHUNK:(c + 1) * A_CHUNK, :]
        dec = jnp.exp(b[A_CHUNK - 1:A_CHUNK, :])
        k_dec = k * jnp.exp(-b)
        return dict(c=c, v=blk[:, 2 * A_W:3 * A_W],
                    q_dec=(blk[:, 0:A_W].astype(F32) * jnp.exp(b)).astype(BF16),
                    k_dec=k_dec.astype(BF16), k_til=(k_dec * dec).astype(BF16), dec=dec)

    def stage3(s):
        s = dict(s)
        attn = [jnp.where(causal, _dot_nt(s["q_dec"][:, sl], s["k_dec"][:, sl]), 0.0).astype(BF16)
                for sl in heads]
        s["o_intra"] = [_dot(attn[h], s["v"][:, sl]) for h, sl in enumerate(heads)]
        s["kv_t"] = [_dot_tn(s["v"][:, sl], s["k_til"][:, sl]) for sl in heads]
        return s

    def stage4(s, state):
        c = s["c"]
        g = hg_ref[0, c * A_CHUNK:(c + 1) * A_CHUNK, 3 * A_W:4 * A_W].astype(F32)
        new_state = []
        for h, sl in enumerate(heads):
            o = s["o_intra"][h] + _dot_nt(s["q_dec"][:, sl], state[h].astype(BF16))
            new_state.append(state[h] * s["dec"][:, sl] + s["kv_t"][h])
            gh = g[:, sl]
            y = _rms(o, ng) * (gh * jax.nn.sigmoid(gh))
            o_ref[0, c * A_CHUNK:(c + 1) * A_CHUNK, sl] = y.astype(BF16)
        return new_state

    state = [st_ref[h] for h in range(A_HEADS)]
    pipe = [None, None, None]
    for t in range(n_chunks + 3):
        if pipe[2] is not None:
            state = stage4(pipe[2], state)
        pipe[2] = stage3(pipe[1]) if pipe[1] is not None else None
        pipe[1] = stage2(pipe[0]) if pipe[0] is not None else None
        pipe[0] = stage1(t) if t < n_chunks else None
    for h in range(A_HEADS):
        st_ref[h] = state[h]


def _hgrn(hg, lb_param, norm_g):
    bsz, s, _ = hg.shape
    tt = min(512, s)
    return pl.pallas_call(
        functools.partial(_hgrn_kernel, n_chunks=tt // A_CHUNK),
        grid=(bsz, s // tt),
        in_specs=[pl.BlockSpec((1, tt, 4 * A_W), lambda b, t: (b, t, 0)),
                  pl.BlockSpec(lb_param.shape, lambda b, t: (0, 0)),
                  pl.BlockSpec((1, A_DV), lambda b, t: (0, 0))],
        out_specs=pl.BlockSpec((1, tt, A_W), lambda b, t: (b, t, 0)),
        out_shape=jax.ShapeDtypeStruct((bsz, s, A_W), BF16),
        scratch_shapes=[pltpu.VMEM((A_HEADS, A_DV, A_DK), F32)],
        compiler_params=pltpu.CompilerParams(
            dimension_semantics=("parallel", "arbitrary"), vmem_limit_bytes=VMEM_LIMIT),
        name="hgrn",
    )(hg, lb_param, norm_g)


def _compress_kernel(kt_ref, vt_ref, pek_ref, pev_ref, w1k_ref, w1v_ref, w2k_ref, w2vt_ref,
                     kc_ref, vct_ref, *, n_blk):
    def hidden(t_ref, pe_ref, w1_ref):
        rows = jnp.concatenate(
            [t_ref[0, pl.ds(l, n_blk, stride=CMP_STRIDE), :] for l in range(CMP_STRIDE)], axis=1)
        halves = [(rows + pe_ref[a:a + 1, :]).astype(BF16) for a in range(2)]
        out = []
        for g in range(B_GROUPS):
            pre = _dot(halves[0], w1_ref[g, 0]) + pltpu.roll(_dot(halves[1], w1_ref[g, 1]), n_blk - 1, 0)
            out.append(jax.nn.gelu(pre).astype(BF16))
        return out

    valid_r = lax.broadcasted_iota(jnp.int32, (n_blk, B_KV_W), 0) < n_blk - 1
    valid_c = lax.broadcasted_iota(jnp.int32, (B_HD, n_blk), 1) < n_blk - 1
    hk = hidden(kt_ref, pek_ref, w1k_ref)
    hv = hidden(vt_ref, pev_ref, w1v_ref)
    kc = jnp.zeros((n_blk, B_KV_W), F32)
    for g in range(B_GROUPS):
        kc = kc + _dot(hk[g], w2k_ref[g])
        vct = _dot_nt(w2vt_ref[...], hv[g])
        vct_ref[0, g * B_HD:(g + 1) * B_HD, :] = jnp.where(valid_c, vct, 0.0).astype(BF16)
    kc_ref[0] = jnp.where(valid_r, kc, 0.0).astype(BF16)


def _compress(kc_tok, vc_tok, pek, pev, w1k, w1v, w2k, w2vt):
    bsz, s, _ = kc_tok.shape
    n_blk = s // CMP_STRIDE
    full = lambda a: pl.BlockSpec(a.shape, lambda b: (0,) * a.ndim)
    tokspec = pl.BlockSpec((1, s, B_KV_W), lambda b: (b, 0, 0))
    return pl.pallas_call(
        functools.partial(_compress_kernel, n_blk=n_blk),
        grid=(bsz,),
        in_specs=[tokspec, tokspec, full(pek), full(pev), full(w1k), full(w1v), full(w2k), full(w2vt)],
        out_specs=(pl.BlockSpec((1, n_blk, B_KV_W), lambda b: (b, 0, 0)),
                   pl.BlockSpec((1, B_KV_W, n_blk), lambda b: (b, 0, 0))),
        out_shape=(jax.ShapeDtypeStruct((bsz, n_blk, B_KV_W), BF16),
                   jax.ShapeDtypeStruct((bsz, B_KV_W, n_blk), BF16)),
        compiler_params=pltpu.CompilerParams(
            dimension_semantics=("parallel",), vmem_limit_bytes=VMEM_LIMIT),
        name="compress",
    )(kc_tok, vc_tok, pek, pev, w1k, w1v, w2k, w2vt)


KV_TILE = 512
WIN_KEYS = WINDOW + Q_BLOCK
QW = B_HPG * Q_BLOCK
LW = B_GROUPS * QW
STRIP = 256
TILE_CLASS = 2
LOOKAHEAD = 4


def _block_rank(imp, n_slc, i_range):
    sub = lax.broadcasted_iota(jnp.int32, (8, Q_BLOCK), 0)
    rank = jnp.zeros((n_slc, Q_BLOCK), F32)
    for i in i_range:
        ri = imp[i:i + 1, :]
        parts = []
        for r in range(n_slc // 8):
            blk = imp[8 * r:8 * r + 8, :]
            gt = jnp.where(ri > blk, 1.0, 0.0)
            ge = jnp.where(ri >= blk, 1.0, 0.0)
            if 8 * r + 7 <= i:
                parts.append(gt)
            elif 8 * r > i:
                parts.append(ge)
            else:
                parts.append(jnp.where(sub > i - 8 * r, ge, gt))
        rank = rank + jnp.concatenate(parts, axis=0)
    return rank


def _nsa_kernel(*refs, n_cmp_rows, n_slc, n_key_tiles):
    qi = pl.program_id(1)
    n_tiles = (qi * Q_BLOCK + KV_TILE - 1) // KV_TILE
    n_class = (n_key_tiles + TILE_CLASS - 1) // TILE_CLASS
    for cb in range(n_class + 1):
        n_max = min(cb * TILE_CLASS, n_key_tiles)
        n_min = max((cb - 1) * TILE_CLASS + 1, 0) if cb else 0
        q0_min = KV_TILE * (n_min - 1) + Q_BLOCK if n_min else 0
        pl.when((n_tiles + TILE_CLASS - 1) // TILE_CLASS == cb)(functools.partial(
            _nsa_step, n_max, q0_min, qi, *refs, n_cmp_rows=n_cmp_rows, n_slc=n_slc))


def _nsa_step(n_tiles, q0_min, qi, qt_ref, gt_ref, kc_ref, vct_ref, ks_ref, kw_ref, vt_ref, ovt_ref, cend_ref,
              o_ref, rhs_ref, *, n_cmp_rows, n_slc):
    q0 = qi * Q_BLOCK
    n_blk = min(n_slc, 16 * ((KV_TILE // SEL_LEN * n_tiles + Q_BLOCK // SEL_LEN + 15) // 16))
    n_cmp = min(n_cmp_rows, 128 * (((KV_TILE * n_tiles + Q_BLOCK) // CMP_STRIDE + 127) // 128))
    steady = q0_min >= WINDOW

    qblk = qt_ref[0]
    zq = jnp.zeros((B_HD, QW), BF16)
    qrows = []
    for g in range(B_GROUPS):
        qg = jnp.concatenate(
            [qblk[(g * B_HPG + h) * B_HD:(g * B_HPG + h + 1) * B_HD, :] for h in range(B_HPG)], axis=1)
        qrows.append(jnp.concatenate([qg if gg == g else zq for gg in range(B_GROUPS)], axis=1))
    qbd = jnp.concatenate(qrows, axis=0)

    qs = pl.multiple_of(q0, Q_BLOCK)
    ws = pl.multiple_of(jnp.maximum(q0 - WINDOW, 0), Q_BLOCK)
    sc = _dot(kc_ref[0, 0:n_cmp, :], qbd)
    sw = _dot(kw_ref[0, pl.ds(ws, WIN_KEYS), :], qbd)
    s_d = _dot(ks_ref[0, pl.ds(qs, Q_BLOCK), 0:B_KV_W], qbd)

    okc = cend_ref[0:n_cmp, :] <= q0
    sc = jnp.where(okc, sc, NEG)
    pc = jnp.exp2(sc - jnp.max(sc, axis=0, keepdims=True))
    l_c = jnp.sum(pc, axis=0, keepdims=True)
    if n_tiles == 0:
        pc = jnp.where(okc, pc, 0.0)
        l_c = jnp.maximum(jnp.sum(pc, axis=0, keepdims=True), 1e-30)
    pc = pc / l_c
    pcb = pc.astype(BF16)
    oc = [_dot(vct_ref[0, g * B_HD:(g + 1) * B_HD, 0:n_cmp], pcb[:, g * QW:(g + 1) * QW])
          for g in range(B_GROUPS)]

    def strip_update(state, s, v):
        m, acc = state
        m_new = jnp.maximum(m, jnp.max(s, axis=0, keepdims=True))
        pb = jnp.exp2(s - m_new).astype(BF16)
        return m_new, jnp.exp2(m - m_new) * acc + _dot(v, pb)

    def v_block(j, start, size):
        return vt_ref[0, j * V_BLK:(j + 1) * V_BLK, pl.ds(start, size)]

    n_str = LW // STRIP
    grp = lambda j: j * STRIP // QW
    lanes = lambda j: slice(j * STRIP, (j + 1) * STRIP)
    fresh = lambda: (jnp.full((1, STRIP), NEG, F32), jnp.zeros((V_BLK, STRIP), F32))
    row_q = lax.broadcasted_iota(jnp.int32, (Q_BLOCK, STRIP), 0)
    tq_q = lax.broadcasted_iota(jnp.int32, (Q_BLOCK, STRIP), 1) & (Q_BLOCK - 1)

    jrow = lax.broadcasted_iota(jnp.int32, (n_blk, Q_BLOCK), 0)
    cur = (q0 + lax.broadcasted_iota(jnp.int32, (n_blk, Q_BLOCK), 1)) >> SEL_SHIFT
    causal_blk = jrow <= cur
    forced = (jrow == 0) | (jrow == cur) | (jrow == cur - 1)
    ovt = ovt_ref[0:n_blk, 0:n_cmp]
    imps = []
    for g in range(B_GROUPS):
        psum = pc[:, g * QW:g * QW + Q_BLOCK]
        for h in range(1, B_HPG):
            psum = psum + pc[:, g * QW + h * Q_BLOCK:g * QW + (h + 1) * Q_BLOCK]
        p_hi = psum.astype(BF16)
        p_lo = (psum - p_hi.astype(F32)).astype(BF16)
        imp = _dot(ovt, p_hi) + _dot(ovt, p_lo)
        imp = jnp.where(forced & causal_blk, FORCE_SCORE, imp)
        imps.append(jnp.where(causal_blk, imp, NEG))

    biases = []
    for g in range(B_GROUPS):
        sel = (_block_rank(imps[g], n_blk, range(n_blk)) < float(min(SEL_TOPK, n_slc))) & causal_blk
        sel = sel & (jrow // (Q_BLOCK // SEL_LEN) != qi)
        biases += [jnp.where(sel, 0.0, NEG).astype(BF16)] * B_HPG
    bias = jnp.concatenate(biases, axis=1)
    rhs_ref[...] = jnp.concatenate(
        [qbd, bias, jnp.zeros((KS_AUG_W - B_KV_W - n_blk, LW), BF16)], axis=0)

    states = [
        strip_update(fresh(), jnp.where(row_q <= tq_q, s_d[:, lanes(j)], NEG), v_block(grp(j), qs, Q_BLOCK))
        for j in range(n_str)]

    order = [(kt, j) for kt in range(n_tiles) for j in range(n_str)]

    def strip_scores(kt, j):
        return _dot(ks_ref[0, kt * KV_TILE:(kt + 1) * KV_TILE, :], rhs_ref[:, lanes(j)])

    pending = [strip_scores(*order[i]) for i in range(min(LOOKAHEAD, len(order)))]
    for i, (kt, j) in enumerate(order):
        s = pending.pop(0)
        if i + LOOKAHEAD < len(order):
            pending.append(strip_scores(*order[i + LOOKAHEAD]))
        states[j] = strip_update(states[j], s, v_block(grp(j), kt * KV_TILE, KV_TILE))

    acc_w = []
    for j in range(n_str):
        swj = sw[:, lanes(j)]
        if steady:
            lo = jnp.where(row_q > tq_q, swj[0:Q_BLOCK], NEG)
            hi = jnp.where(row_q <= tq_q, swj[WINDOW:WIN_KEYS], NEG)
            swj = jnp.concatenate([lo, swj[Q_BLOCK:WINDOW], hi], axis=0)
        else:
            krow = lax.broadcasted_iota(jnp.int32, (WIN_KEYS, STRIP), 0)
            tqw = lax.broadcasted_iota(jnp.int32, (WIN_KEYS, STRIP), 1) & (Q_BLOCK - 1)
            dist = (q0 - ws) + tqw - krow
            swj = jnp.where(lax.bitcast_convert_type(dist, jnp.uint32) < jnp.uint32(WINDOW), swj, NEG)
        acc_w.append(strip_update(fresh(), swj, v_block(B_GROUPS + grp(j), ws, WIN_KEYS))[1])

    per_g = n_str // B_GROUPS
    acc_s = [jnp.concatenate([states[g * per_g + u][1] for u in range(per_g)], axis=1) for g in range(B_GROUPS)]
    acc_w = [jnp.concatenate(acc_w[g * per_g:(g + 1) * per_g], axis=1) for g in range(B_GROUPS)]

    gts = gt_ref[0]
    for g in range(B_GROUPS):
        o_s = acc_s[g][0:B_HD] / acc_s[g][B_HD:B_HD + 1]
        o_w = acc_w[g][0:B_HD] / acc_w[g][B_HD:B_HD + 1]
        for h in range(B_HPG):
            sl = slice(h * Q_BLOCK, (h + 1) * Q_BLOCK)
            r = g * GATE_ROWS + 3 * h
            o_h = gts[r:r + 1] * oc[g][:, sl] + gts[r + 1:r + 2] * o_s[:, sl] + gts[r + 2:r + 3] * o_w[:, sl]
            o_ref[0, (g * B_HPG + h) * B_HD:(g * B_HPG + h + 1) * B_HD, :] = o_h.astype(BF16)


def _nsa(qt, gt, kc, vct, ks, kw, vt, ovt):
    bsz, _, s = qt.shape
    n_cmp_rows = kc.shape[1]
    n_slc = s // SEL_LEN
    assert n_slc <= SEL_ONEHOT_W and s % KV_TILE == 0 and s >= WIN_KEYS
    whole = lambda a: pl.BlockSpec((1,) + a.shape[1:], lambda b, i: (b, 0, 0))
    cend = jnp.asarray((np.arange(n_cmp_rows)[:, None] * CMP_STRIDE + CMP_LEN - 1)
                       - (np.arange(LW)[None, :] % Q_BLOCK), jnp.int32)
    return pl.pallas_call(
        functools.partial(_nsa_kernel, n_cmp_rows=n_cmp_rows, n_slc=n_slc, n_key_tiles=s // KV_TILE),
        grid=(bsz, s // Q_BLOCK),
        in_specs=[
            pl.BlockSpec((1, B_Q_W, Q_BLOCK), lambda b, i: (b, 0, i)),
            pl.BlockSpec((1, B_GROUPS * GATE_ROWS, Q_BLOCK), lambda b, i: (b, 0, i)),
            whole(kc), whole(vct), whole(ks), whole(kw), whole(vt),
            pl.BlockSpec(ovt.shape, lambda b, i: (0, 0)),
            pl.BlockSpec(cend.shape, lambda b, i: (0, 0)),
        ],
        out_specs=pl.BlockSpec((1, B_Q_W, Q_BLOCK), lambda b, i: (b, 0, i)),
        out_shape=jax.ShapeDtypeStruct((bsz, B_Q_W, s), BF16),
        scratch_shapes=[pltpu.VMEM((KS_AUG_W, LW), BF16)],
        compiler_params=pltpu.CompilerParams(
            dimension_semantics=("parallel", "arbitrary"), vmem_limit_bytes=VMEM_LIMIT),
        name="nsa",
    )(qt, gt, kc, vct, ks, kw, vt, ovt, cend)


FF_CHUNK = 1024


def _tail_kernel(x_ref, ya_ref, ybt_ref, mg_ref, wa_ref, wb_ref, wo_ref, g2_ref, w1_ref, w2_ref, gf_ref, o_ref):
    pa = _dot(ya_ref[0], wa_ref[...])
    pb = _dot_tn(ybt_ref[0], wb_ref[...])
    mg = mg_ref[0].astype(F32)
    merged = jax.nn.sigmoid(mg[:, 0:D_MODEL]) * pa + jax.nn.sigmoid(mg[:, D_MODEL:2 * D_MODEL]) * pb
    h = x_ref[0] + _dot(merged.astype(BF16), wo_ref[...])
    hn = _rms(h, g2_ref[...]).astype(BF16)
    acc = jnp.zeros_like(h)
    for c in range(D_FF // FF_CHUNK):
        z = _dot(hn, w1_ref[:, c * FF_CHUNK:(c + 1) * FF_CHUNK])
        acc = acc + _dot(jnp.square(jnp.maximum(z, 0.0)).astype(BF16), w2_ref[c * FF_CHUNK:(c + 1) * FF_CHUNK, :])
    o_ref[0] = _rms(h + acc, gf_ref[...])


def _tail(x, ya, ybt, mg, wa, wb, wo, g2, w1, w2, gf):
    bsz, s, _ = x.shape
    tm = min(512, s)
    tok = lambda w: pl.BlockSpec((1, tm, w), lambda b, t: (b, t, 0))
    const = lambda a: pl.BlockSpec(a.shape, lambda b, t: (0,) * a.ndim, pipeline_mode=pl.Buffered(1))
    return pl.pallas_call(
        _tail_kernel,
        grid=(bsz, s // tm),
        in_specs=[tok(D_MODEL), tok(A_W), pl.BlockSpec((1, B_Q_W, tm), lambda b, t: (b, 0, t)), tok(2 * D_MODEL),
                  const(wa), const(wb), const(wo), const(g2), const(w1), const(w2), const(gf)],
        out_specs=tok(D_MODEL),
        out_shape=jax.ShapeDtypeStruct((bsz, s, D_MODEL), F32),
        compiler_params=pltpu.CompilerParams(
            dimension_semantics=("parallel", "parallel"), vmem_limit_bytes=VMEM_LIMIT),
        name="tail",
    )(x, ya, ybt, mg, wa, wb, wo, g2, w1, w2, gf)


def _pad_groups(a, axis):
    outs = []
    for g in range(B_GROUPS):
        pads = [(0, 0)] * a.ndim
        pads[axis] = (g * B_HD, (B_GROUPS - 1 - g) * B_HD)
        outs.append(jnp.pad(a, pads))
    return jnp.stack(outs)


def _overlap_t(s):
    n_cmp = (s - CMP_LEN) // CMP_STRIDE + 1
    n_slc = s // SEL_LEN
    cs = np.arange(n_cmp)[:, None] * CMP_STRIDE
    ss = np.arange(n_slc)[None, :] * SEL_LEN
    ov = np.clip(np.minimum(cs + CMP_LEN, ss + SEL_LEN) - np.maximum(cs, ss), 0, None) / CMP_LEN
    ovt = np.zeros((n_slc, s // CMP_STRIDE), np.float32)
    ovt[:, :n_cmp] = ov.T
    return jnp.asarray(ovt, BF16)


def kernel(x, positions, norm1_g, w_in, lb_param, hgrn_norm_g, cmp_pe_k, cmp_pe_v, cmp_w1_k, cmp_w2_k,
           cmp_w1_v, cmp_w2_v, w_br_a, w_br_b, w_out, norm2_g, w_ff1, w_ff2, final_g):
    bsz, s, _ = x.shape
    assert norm1_g.shape[0] == 1, "single-layer block"

    w = w_in[0]
    o_bq = 4 * A_W
    o_kv = o_bq + B_Q_W
    o_gate = o_kv + 6 * B_KV_W
    o_mg = o_gate + 3 * B_HEADS
    kv = lambda i: w[:, o_kv + i * B_KV_W:o_kv + (i + 1) * B_KV_W]
    w_nat = jnp.concatenate([w[:, 0:o_bq], w[:, o_mg:o_mg + 2 * D_MODEL], kv(1)], axis=1).astype(BF16)
    wg = w[:, o_gate:o_mg].reshape(D_MODEL, B_GROUPS, 3 * B_HPG)
    wg = jnp.pad(wg, ((0, 0), (0, 0), (0, GATE_ROWS - 3 * B_HPG))).reshape(D_MODEL, B_GROUPS * GATE_ROWS)
    w_t = jnp.concatenate([w[:, o_bq:o_kv], kv(0), kv(2), kv(4), kv(3), kv(5), wg], axis=1).T.astype(BF16)
    invf = (ROPE_THETA ** (-jnp.arange(0, ROT_DIM, 2, dtype=F32) / ROT_DIM)).reshape(ROT_HALF, 1)

    hg, mg, vc_tok, kc_tok, ks, kw, qt, vt, gt = _proj(
        x, positions.reshape(bsz, 1, s), norm1_g, invf, w_nat, w_t)

    ya = _hgrn(hg, lb_param, hgrn_norm_g)

    half_w = CMP_LEN // 2 * B_KV_W

    def w1_halves(w1):
        padded = _pad_groups(w1.reshape(CMP_LEN, B_HD, CMP_HIDDEN), 1)
        return padded.reshape(B_GROUPS, 2, half_w, CMP_HIDDEN).astype(BF16)

    pe_halves = lambda pe: jnp.tile(pe, (1, B_GROUPS)).reshape(2, half_w)
    kc, vct = _compress(
        kc_tok, vc_tok, pe_halves(cmp_pe_k[0]), pe_halves(cmp_pe_v[0]), w1_halves(cmp_w1_k[0]),
        w1_halves(cmp_w1_v[0]), _pad_groups(cmp_w2_k[0], 1).astype(BF16), cmp_w2_v[0].T.astype(BF16))

    ybt = _nsa(qt, gt, kc, vct, ks, kw, vt, _overlap_t(s))

    return _tail(x, ya, ybt, mg, w_br_a[0].astype(BF16), w_br_b[0].astype(BF16), w_out[0].astype(BF16),
                 norm2_g, w_ff1[0].astype(BF16), w_ff2[0].astype(BF16), final_g.reshape(1, D_MODEL))
```

```python
import functools

import jax
import jax.numpy as jnp
import numpy as np
from jax import lax
from jax.experimental import pallas as pl
from jax.experimental.pallas import tpu as pltpu

F32 = jnp.float32
BF16 = jnp.bfloat16

D_MODEL = 1024
A_HEADS = 4
A_DK = 128
A_DV = 128
A_CHUNK = 64
A_W = A_HEADS * A_DK
B_HEADS = 8
B_GROUPS = 2
B_HPG = B_HEADS // B_GROUPS
B_HD = 64
B_Q_W = B_HEADS * B_HD
B_KV_W = B_GROUPS * B_HD
CMP_LEN = 32
CMP_STRIDE = 16
CMP_HIDDEN = 256
SEL_LEN = 64
SEL_TOPK = 16
WINDOW = 512
Q_BLOCK = 128
FORCE_SCORE = 1e4
NEG = -1e30
ROPE_THETA = 500000.0
ROT_DIM = B_HD // 4
ROT_HALF = ROT_DIM // 2
D_FF = 4 * D_MODEL
EPS = 1e-6
LOG2_E = 1.4426950408889634
GATE_ROWS = 16
SEL_SHIFT = SEL_LEN.bit_length() - 1
SEL_ONEHOT_W = 128
KS_AUG_W = B_KV_W + SEL_ONEHOT_W
V_AUX = 16
V_BLK = B_HD + V_AUX
VT_ROWS = 2 * B_GROUPS * V_BLK

VMEM_LIMIT = 56 * 1024 * 1024

NT = (((1,), (1,)), ((), ()))
TN = (((0,), (0,)), ((), ()))


def _dot(a, b):
    return jnp.dot(a, b, preferred_element_type=F32)


def _dot_nt(a, b):
    return lax.dot_general(a, b, NT, preferred_element_type=F32)


def _dot_tn(a, b):
    return lax.dot_general(a, b, TN, preferred_element_type=F32)


def _rms(x, g):
    return x * lax.rsqrt(jnp.mean(x * x, axis=-1, keepdims=True) + EPS) * g


def _proj_kernel(x_ref, pos_ref, g1_ref, invf_ref, wnat_ref, wt_ref,
                 hg_ref, mg_ref, vc_ref, kc_ref, ks_ref, kw_ref, qt_ref, vt_ref, gt_ref):
    xn = _rms(x_ref[0], g1_ref[...]).astype(BF16)

    def nat(c0, c1):
        return _dot(xn, wnat_ref[:, c0:c1])

    half = 2 * A_W
    for c in range(2):
        hg_ref[0, :, c * half:(c + 1) * half] = nat(c * half, (c + 1) * half).astype(BF16)
        mg_ref[0, :, c * half:(c + 1) * half] = nat(4 * A_W + c * half, 4 * A_W + (c + 1) * half).astype(BF16)
    vc_ref[0] = nat(4 * A_W + 2 * D_MODEL, 4 * A_W + 2 * D_MODEL + B_KV_W)

    def tr(r0, r1):
        return _dot_nt(wt_ref[r0:r1, :], xn)

    ang = invf_ref[...] * pos_ref[0].astype(F32)
    cos = jnp.cos(ang)
    sin = jnp.sin(ang)

    def rope(xt, n_heads):
        pieces = []
        for h in range(n_heads):
            b0 = h * B_HD
            t1 = xt[b0:b0 + ROT_HALF]
            t2 = xt[b0 + ROT_HALF:b0 + ROT_DIM]
            pieces += [t1 * cos - t2 * sin, t2 * cos + t1 * sin, xt[b0 + ROT_DIM:b0 + B_HD]]
        return jnp.concatenate(pieces, axis=0)

    r_q, r_k, r_v = B_Q_W, B_Q_W + 3 * B_KV_W, B_Q_W + 5 * B_KV_W
    qt_ref[0] = (rope(tr(0, r_q), B_HEADS) * (B_HD ** -0.5 * LOG2_E)).astype(BF16)
    k3 = rope(tr(r_q, r_k), 3 * B_GROUPS).T
    kc_ref[0] = k3[:, 0:B_KV_W]
    kw_ref[0] = k3[:, 2 * B_KV_W:3 * B_KV_W].astype(BF16)
    tm = k3.shape[0]
    tok = pl.program_id(1) * tm + lax.broadcasted_iota(jnp.int32, (tm, SEL_ONEHOT_W), 0)
    lane = lax.broadcasted_iota(jnp.int32, (tm, SEL_ONEHOT_W), 1)
    ks_ref[0, :, 0:B_KV_W] = k3[:, B_KV_W:2 * B_KV_W].astype(BF16)
    ks_ref[0, :, B_KV_W:B_KV_W + SEL_ONEHOT_W] = jnp.where((tok >> SEL_SHIFT) == lane, 1.0, 0.0).astype(BF16)
    vt = tr(r_k, r_v)
    aux = jnp.where(lax.broadcasted_iota(jnp.int32, (V_AUX, tm), 0) == 0, 1.0, 0.0)
    vt_ref[0] = jnp.concatenate(
        [piece for j in range(2 * B_GROUPS) for piece in (vt[j * B_HD:(j + 1) * B_HD], aux)], axis=0).astype(BF16)
    gt_ref[0] = jax.nn.sigmoid(tr(r_v, r_v + B_GROUPS * GATE_ROWS))


def _proj(x, pos3, g1, invf, w_nat, w_t):
    bsz, s, _ = x.shape
    tm = min(512, s)
    n_nat = w_nat.shape[1]
    n_t = w_t.shape[0]
    tok = lambda w: pl.BlockSpec((1, tm, w), lambda b, t: (b, t, 0))
    trn = lambda r: pl.BlockSpec((1, r, tm), lambda b, t: (b, 0, t))
    const = lambda shp: pl.BlockSpec(shp, lambda b, t: (0,) * len(shp), pipeline_mode=pl.Buffered(1))
    out_shape = (
        jax.ShapeDtypeStruct((bsz, s, 4 * A_W), BF16),
        jax.ShapeDtypeStruct((bsz, s, 2 * D_MODEL), BF16),
        jax.ShapeDtypeStruct((bsz, s, B_KV_W), F32),
        jax.ShapeDtypeStruct((bsz, s, B_KV_W), F32),
        jax.ShapeDtypeStruct((bsz, s, KS_AUG_W), BF16),
        jax.ShapeDtypeStruct((bsz, s, B_KV_W), BF16),
        jax.ShapeDtypeStruct((bsz, B_Q_W, s), BF16),
        jax.ShapeDtypeStruct((bsz, VT_ROWS, s), BF16),
        jax.ShapeDtypeStruct((bsz, B_GROUPS * GATE_ROWS, s), F32),
    )
    return pl.pallas_call(
        _proj_kernel,
        grid=(bsz, s // tm),
        in_specs=[tok(D_MODEL), trn(1), const((1, D_MODEL)), const((ROT_HALF, 1)),
                  const((D_MODEL, n_nat)), const((n_t, D_MODEL))],
        out_specs=(tok(4 * A_W), tok(2 * D_MODEL), tok(B_KV_W), tok(B_KV_W), tok(KS_AUG_W), tok(B_KV_W),
                   trn(B_Q_W), trn(VT_ROWS), trn(B_GROUPS * GATE_ROWS)),
        out_shape=out_shape,
        compiler_params=pltpu.CompilerParams(
            dimension_semantics=("parallel", "parallel"), vmem_limit_bytes=VMEM_LIMIT),
        name="proj",
    )(x, pos3, g1, invf, w_nat, w_t)


def _hgrn_kernel(hg_ref, lbp_ref, ng_ref, o_ref, st_ref, *, n_chunks):
    @pl.when(pl.program_id(1) == 0)
    def _():
        st_ref[...] = jnp.zeros_like(st_ref)

    lbp = lbp_ref[...]
    e = jnp.exp(lbp - jnp.max(lbp, axis=0, keepdims=True))
    lb = e[0:1] / jnp.sum(e, axis=0, keepdims=True)
    ng = ng_ref[...]
    row = lax.broadcasted_iota(jnp.int32, (A_CHUNK, A_CHUNK), 0)
    col = lax.broadcasted_iota(jnp.int32, (A_CHUNK, A_CHUNK), 1)
    causal = row >= col
    tri = jnp.where(causal, 1.0, 0.0).astype(BF16)

    heads = [slice(h * A_DK, (h + 1) * A_DK) for h in range(A_HEADS)]

    def stage1(c):
        blk = hg_ref[0, c * A_CHUNK:(c + 1) * A_CHUNK, :]
        f = lb + (1.0 - lb) * jax.nn.sigmoid(blk[:, A_W:2 * A_W].astype(F32))
        lf = jnp.log(f)
        lf_hi = lf.astype(BF16)
        r1 = lf - lf_hi.astype(F32)
        lf_mid = r1.astype(BF16)
        lf_lo = (r1 - lf_mid.astype(F32)).astype(BF16)
        b = _dot(tri, lf_hi) + _dot(tri, lf_mid) + _dot(tri, lf_lo)
        return dict(c=c, k=1.0 - f, b=b)

    def stage2(s):
        c, k, b = s["c"], s["k"], s["b"]
        blk = hg_ref[0, c * A_CHUNK:(c + 1) * A_CHUNK, :]
        dec = jnp.exp(b[A_CHUNK - 1:A_CHUNK, :])
        k_dec = k * jnp.exp(-b)
        return dict(c=c, v=blk[:, 2 * A_W:3 * A_W],
                    q_dec=(blk[:, 0:A_W].astype(F32) * jnp.exp(b)).astype(BF16),
                    k_dec=k_dec.astype(BF16), k_til=(k_dec * dec).astype(BF16), dec=dec)

    def stage3(s):
        s = dict(s)
        attn = [jnp.where(causal, _dot_nt(s["q_dec"][:, sl], s["k_dec"][:, sl]), 0.0).astype(BF16)
                for sl in heads]
        s["o_intra"] = [_dot(attn[h], s["v"][:, sl]) for h, sl in enumerate(heads)]
        s["kv_t"] = [_dot_tn(s["v"][:, sl], s["k_til"][:, sl]) for sl in heads]
        return s

    def stage4(s, state):
        c = s["c"]
        g = hg_ref[0, c * A_CHUNK:(c + 1) * A_CHUNK, 3 * A_W:4 * A_W].astype(F32)
        new_state = []
        for h, sl in enumerate(heads):
            o = s["o_intra"][h] + _dot_nt(s["q_dec"][:, sl], state[h].astype(BF16))
            new_state.append(state[h] * s["dec"][:, sl] + s["kv_t"][h])
            gh = g[:, sl]
            y = _rms(o, ng) * (gh * jax.nn.sigmoid(gh))
            o_ref[0, c * A_CHUNK:(c + 1) * A_CHUNK, sl] = y.astype(BF16)
        return new_state

    state = [st_ref[h] for h in range(A_HEADS)]
    pipe = [None, None, None]
    for t in range(n_chunks + 3):
        if pipe[2] is not None:
            state = stage4(pipe[2], state)
        pipe[2] = stage3(pipe[1]) if pipe[1] is not None else None
        pipe[1] = stage2(pipe[0]) if pipe[0] is not None else None
        pipe[0] = stage1(t) if t < n_chunks else None
    for h in range(A_HEADS):
        st_ref[h] = state[h]


def _hgrn(hg, lb_param, norm_g):
    bsz, s, _ = hg.shape
    tt = min(512, s)
    return pl.pallas_call(
        functools.partial(_hgrn_kernel, n_chunks=tt // A_CHUNK),
        grid=(bsz, s // tt),
        in_specs=[pl.BlockSpec((1, tt, 4 * A_W), lambda b, t: (b, t, 0)),
                  pl.BlockSpec(lb_param.shape, lambda b, t: (0, 0)),
                  pl.BlockSpec((1, A_DV), lambda b, t: (0, 0))],
        out_specs=pl.BlockSpec((1, tt, A_W), lambda b, t: (b, t, 0)),
        out_shape=jax.ShapeDtypeStruct((bsz, s, A_W), BF16),
        scratch_shapes=[pltpu.VMEM((A_HEADS, A_DV, A_DK), F32)],
        compiler_params=pltpu.CompilerParams(
            dimension_semantics=("parallel", "arbitrary"), vmem_limit_bytes=VMEM_LIMIT),
        name="hgrn",
    )(hg, lb_param, norm_g)


def _compress_kernel(kt_ref, vt_ref, pek_ref, pev_ref, w1k_ref, w1v_ref, w2k_ref, w2vt_ref,
                     kc_ref, vct_ref, *, n_blk):
    def hidden(t_ref, pe_ref, w1_ref):
        rows = jnp.concatenate(
            [t_ref[0, pl.ds(l, n_blk, stride=CMP_STRIDE), :] for l in range(CMP_STRIDE)], axis=1)
        halves = [(rows + pe_ref[a:a + 1, :]).astype(BF16) for a in range(2)]
        out = []
        for g in range(B_GROUPS):
            pre = _dot(halves[0], w1_ref[g, 0]) + pltpu.roll(_dot(halves[1], w1_ref[g, 1]), n_blk - 1, 0)
            out.append(jax.nn.gelu(pre).astype(BF16))
        return out

    valid_r = lax.broadcasted_iota(jnp.int32, (n_blk, B_KV_W), 0) < n_blk - 1
    valid_c = lax.broadcasted_iota(jnp.int32, (B_HD, n_blk), 1) < n_blk - 1
    hk = hidden(kt_ref, pek_ref, w1k_ref)
    hv = hidden(vt_ref, pev_ref, w1v_ref)
    kc = jnp.zeros((n_blk, B_KV_W), F32)
    for g in range(B_GROUPS):
        kc = kc + _dot(hk[g], w2k_ref[g])
        vct = _dot_nt(w2vt_ref[...], hv[g])
        vct_ref[0, g * B_HD:(g + 1) * B_HD, :] = jnp.where(valid_c, vct, 0.0).astype(BF16)
    kc_ref[0] = jnp.where(valid_r, kc, 0.0).astype(BF16)


def _compress(kc_tok, vc_tok, pek, pev, w1k, w1v, w2k, w2vt):
    bsz, s, _ = kc_tok.shape
    n_blk = s // CMP_STRIDE
    full = lambda a: pl.BlockSpec(a.shape, lambda b: (0,) * a.ndim)
    tokspec = pl.BlockSpec((1, s, B_KV_W), lambda b: (b, 0, 0))
    return pl.pallas_call(
        functools.partial(_compress_kernel, n_blk=n_blk),
        grid=(bsz,),
        in_specs=[tokspec, tokspec, full(pek), full(pev), full(w1k), full(w1v), full(w2k), full(w2vt)],
        out_specs=(pl.BlockSpec((1, n_blk, B_KV_W), lambda b: (b, 0, 0)),
                   pl.BlockSpec((1, B_KV_W, n_blk), lambda b: (b, 0, 0))),
        out_shape=(jax.ShapeDtypeStruct((bsz, n_blk, B_KV_W), BF16),
                   jax.ShapeDtypeStruct((bsz, B_KV_W, n_blk), BF16)),
        compiler_params=pltpu.CompilerParams(
            dimension_semantics=("parallel",), vmem_limit_bytes=VMEM_LIMIT),
        name="compress",
    )(kc_tok, vc_tok, pek, pev, w1k, w1v, w2k, w2vt)


KV_TILE = 512
WIN_KEYS = WINDOW + Q_BLOCK
QW = B_HPG * Q_BLOCK
LW = B_GROUPS * QW
STRIP = 256
TILE_CLASS = 1
LOOKAHEAD = 4


def _block_rank(imp, n_slc, i_range):
    sub = lax.broadcasted_iota(jnp.int32, (8, Q_BLOCK), 0)
    rank = jnp.zeros((n_slc, Q_BLOCK), F32)
    for i in i_range:
        ri = imp[i:i + 1, :]
        parts = []
        for r in range(n_slc // 8):
            blk = imp[8 * r:8 * r + 8, :]
            gt = jnp.where(ri > blk, 1.0, 0.0)
            ge = jnp.where(ri >= blk, 1.0, 0.0)
            if 8 * r + 7 <= i:
                parts.append(gt)
            elif 8 * r > i:
                parts.append(ge)
            else:
                parts.append(jnp.where(sub > i - 8 * r, ge, gt))
        rank = rank + jnp.concatenate(parts, axis=0)
    return rank


def _nsa_kernel(*refs, n_cmp_rows, n_slc, n_key_tiles):
    qi = pl.program_id(1)
    n_tiles = (qi * Q_BLOCK + KV_TILE - 1) // KV_TILE
    n_class = (n_key_tiles + TILE_CLASS - 1) // TILE_CLASS
    for cb in range(n_class + 1):
        n_max = min(cb * TILE_CLASS, n_key_tiles)
        n_min = max((cb - 1) * TILE_CLASS + 1, 0) if cb else 0
        q0_min = KV_TILE * (n_min - 1) + Q_BLOCK if n_min else 0
        pl.when((n_tiles + TILE_CLASS - 1) // TILE_CLASS == cb)(functools.partial(
            _nsa_step, n_max, q0_min, qi, *refs, n_cmp_rows=n_cmp_rows, n_slc=n_slc))


def _nsa_step(n_tiles, q0_min, qi, qt_ref, gt_ref, kc_ref, vct_ref, ks_ref, kw_ref, vt_ref, ovt_ref, cend_ref,
              o_ref, rhs_ref, *, n_cmp_rows, n_slc):
    q0 = qi * Q_BLOCK
    n_blk = min(n_slc, 16 * ((KV_TILE // SEL_LEN * n_tiles + Q_BLOCK // SEL_LEN + 15) // 16))
    n_cmp = min(n_cmp_rows, 128 * (((KV_TILE * n_tiles + Q_BLOCK) // CMP_STRIDE + 127) // 128))
    steady = q0_min >= WINDOW

    qblk = qt_ref[0]
    zq = jnp.zeros((B_HD, QW), BF16)
    qrows = []
    for g in range(B_GROUPS):
        qg = jnp.concatenate(
            [qblk[(g * B_HPG + h) * B_HD:(g * B_HPG + h + 1) * B_HD, :] for h in range(B_HPG)], axis=1)
        qrows.append(jnp.concatenate([qg if gg == g else zq for gg in range(B_GROUPS)], axis=1))
    qbd = jnp.concatenate(qrows, axis=0)

    qs = pl.multiple_of(q0, Q_BLOCK)
    ws = pl.multiple_of(jnp.maximum(q0 - WINDOW, 0), Q_BLOCK)
    sc = _dot(kc_ref[0, 0:n_cmp, :], qbd)
    sw = _dot(kw_ref[0, pl.ds(ws, WIN_KEYS), :], qbd)
    s_d = _dot(ks_ref[0, pl.ds(qs, Q_BLOCK), 0:B_KV_W], qbd)

    okc = cend_ref[0:n_cmp, :] <= q0
    sc = jnp.where(okc, sc, NEG)
    pc = jnp.exp2(sc - jnp.max(sc, axis=0, keepdims=True))
    l_c = jnp.sum(pc, axis=0, keepdims=True)
    if n_tiles == 0:
        pc = jnp.where(okc, pc, 0.0)
        l_c = jnp.maximum(jnp.sum(pc, axis=0, keepdims=True), 1e-30)
    pc = pc / l_c
    pcb = pc.astype(BF16)
    oc = [_dot(vct_ref[0, g * B_HD:(g + 1) * B_HD, 0:n_cmp], pcb[:, g * QW:(g + 1) * QW])
          for g in range(B_GROUPS)]

    def strip_update(state, s, v):
        m, acc = state
        m_new = jnp.maximum(m, jnp.max(s, axis=0, keepdims=True))
        pb = jnp.exp2(s - m_new).astype(BF16)
        return m_new, jnp.exp2(m - m_new) * acc + _dot(v, pb)

    def v_block(j, start, size):
        return vt_ref[0, j * V_BLK:(j + 1) * V_BLK, pl.ds(start, size)]

    n_str = LW // STRIP
    grp = lambda j: j * STRIP // QW
    lanes = lambda j: slice(j * STRIP, (j + 1) * STRIP)
    fresh = lambda: (jnp.full((1, STRIP), NEG, F32), jnp.zeros((V_BLK, STRIP), F32))
    row_q = lax.broadcasted_iota(jnp.int32, (Q_BLOCK, STRIP), 0)
    tq_q = lax.broadcasted_iota(jnp.int32, (Q_BLOCK, STRIP), 1) & (Q_BLOCK - 1)

    jrow = lax.broadcasted_iota(jnp.int32, (n_blk, Q_BLOCK), 0)
    cur = (q0 + lax.broadcasted_iota(jnp.int32, (n_blk, Q_BLOCK), 1)) >> SEL_SHIFT
    causal_blk = jrow <= cur
    forced = (jrow == 0) | (jrow == cur) | (jrow == cur - 1)
    ovt = ovt_ref[0:n_blk, 0:n_cmp]
    imps = []
    for g in range(B_GROUPS):
        psum = pc[:, g * QW:g * QW + Q_BLOCK]
        for h in range(1, B_HPG):
            psum = psum + pc[:, g * QW + h * Q_BLOCK:g * QW + (h + 1) * Q_BLOCK]
        p_hi = psum.astype(BF16)
        p_lo = (psum - p_hi.astype(F32)).astype(BF16)
        imp = _dot(ovt, p_hi) + _dot(ovt, p_lo)
        imp = jnp.where(forced & causal_blk, FORCE_SCORE, imp)
        imps.append(jnp.where(causal_blk, imp, NEG))

    biases = []
    for g in range(B_GROUPS):
        sel = (_block_rank(imps[g], n_blk, range(n_blk)) < float(min(SEL_TOPK, n_slc))) & causal_blk
        sel = sel & (jrow // (Q_BLOCK // SEL_LEN) != qi)
        biases += [jnp.where(sel, 0.0, NEG).astype(BF16)] * B_HPG
    bias = jnp.concatenate(biases, axis=1)
    rhs_ref[...] = jnp.concatenate(
        [qbd, bias, jnp.zeros((KS_AUG_W - B_KV_W - n_blk, LW), BF16)], axis=0)

    states = [
        strip_update(fresh(), jnp.where(row_q <= tq_q, s_d[:, lanes(j)], NEG), v_block(grp(j), qs, Q_BLOCK))
        for j in range(n_str)]

    order = [(kt, j) for kt in range(n_tiles) for j in range(n_str)]

    def strip_scores(kt, j):
        return _dot(ks_ref[0, kt * KV_TILE:(kt + 1) * KV_TILE, :], rhs_ref[:, lanes(j)])

    pending = [strip_scores(*order[i]) for i in range(min(LOOKAHEAD, len(order)))]
    for i, (kt, j) in enumerate(order):
        s = pending.pop(0)
        if i + LOOKAHEAD < len(order):
            pending.append(strip_scores(*order[i + LOOKAHEAD]))
        states[j] = strip_update(states[j], s, v_block(grp(j), kt * KV_TILE, KV_TILE))

    acc_w = []
    for j in range(n_str):
        swj = sw[:, lanes(j)]
        if steady:
            lo = jnp.where(row_q > tq_q, swj[0:Q_BLOCK], NEG)
            hi = jnp.where(row_q <= tq_q, swj[WINDOW:WIN_KEYS], NEG)
            swj = jnp.concatenate([lo, swj[Q_BLOCK:WINDOW], hi], axis=0)
        else:
            krow = lax.broadcasted_iota(jnp.int32, (WIN_KEYS, STRIP), 0)
            tqw = lax.broadcasted_iota(jnp.int32, (WIN_KEYS, STRIP), 1) & (Q_BLOCK - 1)
            dist = (q0 - ws) + tqw - krow
            swj = jnp.where(lax.bitcast_convert_type(dist, jnp.uint32) < jnp.uint32(WINDOW), swj, NEG)
        acc_w.append(strip_update(fresh(), swj, v_block(B_GROUPS + grp(j), ws, WIN_KEYS))[1])

    per_g = n_str // B_GROUPS
    acc_s = [jnp.concatenate([states[g * per_g + u][1] for u in range(per_g)], axis=1) for g in range(B_GROUPS)]
    acc_w = [jnp.concatenate(acc_w[g * per_g:(g + 1) * per_g], axis=1) for g in range(B_GROUPS)]

    gts = gt_ref[0]
    for g in range(B_GROUPS):
        o_s = acc_s[g][0:B_HD] / acc_s[g][B_HD:B_HD + 1]
        o_w = acc_w[g][0:B_HD] / acc_w[g][B_HD:B_HD + 1]
        for h in range(B_HPG):
            sl = slice(h * Q_BLOCK, (h + 1) * Q_BLOCK)
            r = g * GATE_ROWS + 3 * h
            o_h = gts[r:r + 1] * oc[g][:, sl] + gts[r + 1:r + 2] * o_s[:, sl] + gts[r + 2:r + 3] * o_w[:, sl]
            o_ref[0, (g * B_HPG + h) * B_HD:(g * B_HPG + h + 1) * B_HD, :] = o_h.astype(BF16)


def _nsa(qt, gt, kc, vct, ks, kw, vt, ovt):
    bsz, _, s = qt.shape
    n_cmp_rows = kc.shape[1]
    n_slc = s // SEL_LEN
    assert n_slc <= SEL_ONEHOT_W and s % KV_TILE == 0 and s >= WIN_KEYS
    whole = lambda a: pl.BlockSpec((1,) + a.shape[1:], lambda b, i: (b, 0, 0))
    cend = jnp.asarray((np.arange(n_cmp_rows)[:, None] * CMP_STRIDE + CMP_LEN - 1)
                       - (np.arange(LW)[None, :] % Q_BLOCK), jnp.int32)
    return pl.pallas_call(
        functools.partial(_nsa_kernel, n_cmp_rows=n_cmp_rows, n_slc=n_slc, n_key_tiles=s // KV_TILE),
        grid=(bsz, s // Q_BLOCK),
        in_specs=[
            pl.BlockSpec((1, B_Q_W, Q_BLOCK), lambda b, i: (b, 0, i)),
            pl.BlockSpec((1, B_GROUPS * GATE_ROWS, Q_BLOCK), lambda b, i: (b, 0, i)),
            whole(kc), whole(vct), whole(ks), whole(kw), whole(vt),
            pl.BlockSpec(ovt.shape, lambda b, i: (0, 0)),
            pl.BlockSpec(cend.shape, lambda b, i: (0, 0)),
        ],
        out_specs=pl.BlockSpec((1, B_Q_W, Q_BLOCK), lambda b, i: (b, 0, i)),
        out_shape=jax.ShapeDtypeStruct((bsz, B_Q_W, s), BF16),
        scratch_shapes=[pltpu.VMEM((KS_AUG_W, LW), BF16)],
        compiler_params=pltpu.CompilerParams(
            dimension_semantics=("parallel", "arbitrary"), vmem_limit_bytes=VMEM_LIMIT),
        name="nsa",
    )(qt, gt, kc, vct, ks, kw, vt, ovt, cend)


FF_CHUNK = 1024


def _tail_kernel(x_ref, ya_ref, ybt_ref, mg_ref, wa_ref, wb_ref, wo_ref, g2_ref, w1_ref, w2_ref, gf_ref, o_ref):
    pa = _dot(ya_ref[0], wa_ref[...])
    pb = _dot_tn(ybt_ref[0], wb_ref[...])
    mg = mg_ref[0].astype(F32)
    merged = jax.nn.sigmoid(mg[:, 0:D_MODEL]) * pa + jax.nn.sigmoid(mg[:, D_MODEL:2 * D_MODEL]) * pb
    h = x_ref[0] + _dot(merged.astype(BF16), wo_ref[...])
    hn = _rms(h, g2_ref[...]).astype(BF16)
    acc = jnp.zeros_like(h)
    for c in range(D_FF // FF_CHUNK):
        z = _dot(hn, w1_ref[:, c * FF_CHUNK:(c + 1) * FF_CHUNK])
        acc = acc + _dot(jnp.square(jnp.maximum(z, 0.0)).astype(BF16), w2_ref[c * FF_CHUNK:(c + 1) * FF_CHUNK, :])
    o_ref[0] = _rms(h + acc, gf_ref[...])


def _tail(x, ya, ybt, mg, wa, wb, wo, g2, w1, w2, gf):
    bsz, s, _ = x.shape
    tm = min(512, s)
    tok = lambda w: pl.BlockSpec((1, tm, w), lambda b, t: (b, t, 0))
    const = lambda a: pl.BlockSpec(a.shape, lambda b, t: (0,) * a.ndim, pipeline_mode=pl.Buffered(1))
    return pl.pallas_call(
        _tail_kernel,
        grid=(bsz, s // tm),
        in_specs=[tok(D_MODEL), tok(A_W), pl.BlockSpec((1, B_Q_W, tm), lambda b, t: (b, 0, t)), tok(2 * D_MODEL),
                  const(wa), const(wb), const(wo), const(g2), const(w1), const(w2), const(gf)],
        out_specs=tok(D_MODEL),
        out_shape=jax.ShapeDtypeStruct((bsz, s, D_MODEL), F32),
        compiler_params=pltpu.CompilerParams(
            dimension_semantics=("parallel", "parallel"), vmem_limit_bytes=VMEM_LIMIT),
        name="tail",
    )(x, ya, ybt, mg, wa, wb, wo, g2, w1, w2, gf)


def _pad_groups(a, axis):
    outs = []
    for g in range(B_GROUPS):
        pads = [(0, 0)] * a.ndim
        pads[axis] = (g * B_HD, (B_GROUPS - 1 - g) * B_HD)
        outs.append(jnp.pad(a, pads))
    return jnp.stack(outs)


def _overlap_t(s):
    n_cmp = (s - CMP_LEN) // CMP_STRIDE + 1
    n_slc = s // SEL_LEN
    cs = np.arange(n_cmp)[:, None] * CMP_STRIDE
    ss = np.arange(n_slc)[None, :] * SEL_LEN
    ov = np.clip(np.minimum(cs + CMP_LEN, ss + SEL_LEN) - np.maximum(cs, ss), 0, None) / CMP_LEN
    ovt = np.zeros((n_slc, s // CMP_STRIDE), np.float32)
    ovt[:, :n_cmp] = ov.T
    return jnp.asarray(ovt, BF16)


def kernel(x, positions, norm1_g, w_in, lb_param, hgrn_norm_g, cmp_pe_k, cmp_pe_v, cmp_w1_k, cmp_w2_k,
           cmp_w1_v, cmp_w2_v, w_br_a, w_br_b, w_out, norm2_g, w_ff1, w_ff2, final_g):
    bsz, s, _ = x.shape
    assert norm1_g.shape[0] == 1, "single-layer block"

    w = w_in[0]
    o_bq = 4 * A_W
    o_kv = o_bq + B_Q_W
    o_gate = o_kv + 6 * B_KV_W
    o_mg = o_gate + 3 * B_HEADS
    kv = lambda i: w[:, o_kv + i * B_KV_W:o_kv + (i + 1) * B_KV_W]
    w_nat = jnp.concatenate([w[:, 0:o_bq], w[:, o_mg:o_mg + 2 * D_MODEL], kv(1)], axis=1).astype(BF16)
    wg = w[:, o_gate:o_mg].reshape(D_MODEL, B_GROUPS, 3 * B_HPG)
    wg = jnp.pad(wg, ((0, 0), (0, 0), (0, GATE_ROWS - 3 * B_HPG))).reshape(D_MODEL, B_GROUPS * GATE_ROWS)
    w_t = jnp.concatenate([w[:, o_bq:o_kv], kv(0), kv(2), kv(4), kv(3), kv(5), wg], axis=1).T.astype(BF16)
    invf = (ROPE_THETA ** (-jnp.arange(0, ROT_DIM, 2, dtype=F32) / ROT_DIM)).reshape(ROT_HALF, 1)

    hg, mg, vc_tok, kc_tok, ks, kw, qt, vt, gt = _proj(
        x, positions.reshape(bsz, 1, s), norm1_g, invf, w_nat, w_t)

    ya = _hgrn(hg, lb_param, hgrn_norm_g)

    half_w = CMP_LEN // 2 * B_KV_W

    def w1_halves(w1):
        padded = _pad_groups(w1.reshape(CMP_LEN, B_HD, CMP_HIDDEN), 1)
        return padded.reshape(B_GROUPS, 2, half_w, CMP_HIDDEN).astype(BF16)

    pe_halves = lambda pe: jnp.tile(pe, (1, B_GROUPS)).reshape(2, half_w)
    kc, vct = _compress(
        kc_tok, vc_tok, pe_halves(cmp_pe_k[0]), pe_halves(cmp_pe_v[0]), w1_halves(cmp_w1_k[0]),
        w1_halves(cmp_w1_v[0]), _pad_groups(cmp_w2_k[0], 1).astype(BF16), cmp_w2_v[0].T.astype(BF16))

    ybt = _nsa(qt, gt, kc, vct, ks, kw, vt, _overlap_t(s))

    return _tail(x, ya, ybt, mg, w_br_a[0].astype(BF16), w_br_b[0].astype(BF16), w_out[0].astype(BF16),
                 norm2_g, w_ff1[0].astype(BF16), w_ff2[0].astype(BF16), final_g.reshape(1, D_MODEL))
```

```python
import functools

import jax
import jax.numpy as jnp
import numpy as np
from jax import lax
from jax.experimental import pallas as pl
from jax.experimental.pallas import tpu as pltpu

F32 = jnp.float32
BF16 = jnp.bfloat16

D_MODEL = 1024
A_HEADS = 4
A_DK = 128
A_DV = 128
A_CHUNK = 64
A_W = A_HEADS * A_DK
B_HEADS = 8
B_GROUPS = 2
B_HPG = B_HEADS // B_GROUPS
B_HD = 64
B_Q_W = B_HEADS * B_HD
B_KV_W = B_GROUPS * B_HD
CMP_LEN = 32
CMP_STRIDE = 16
CMP_HIDDEN = 256
SEL_LEN = 64
SEL_TOPK = 16
WINDOW = 512
Q_BLOCK = 128
FORCE_SCORE = 1e4
NEG = -1e30
ROPE_THETA = 500000.0
ROT_DIM = B_HD // 4
ROT_HALF = ROT_DIM // 2
D_FF = 4 * D_MODEL
EPS = 1e-6
LOG2_E = 1.4426950408889634
GATE_ROWS = 16
SEL_SHIFT = SEL_LEN.bit_length() - 1
SEL_ONEHOT_W = 128
KS_AUG_W = B_KV_W + SEL_ONEHOT_W
V_AUX = 16
V_BLK = B_HD + V_AUX
VT_ROWS = 2 * B_GROUPS * V_BLK

V7X_VMEM_BYTES = 64 * 1024 * 1024
VMEM_LIMIT = V7X_VMEM_BYTES * 7 // 8

NT = (((1,), (1,)), ((), ()))
TN = (((0,), (0,)), ((), ()))


def _dot(a, b):
    return jnp.dot(a, b, preferred_element_type=F32)


def _dot_nt(a, b):
    return lax.dot_general(a, b, NT, preferred_element_type=F32)


def _dot_tn(a, b):
    return lax.dot_general(a, b, TN, preferred_element_type=F32)


def _rms(x, g):
    return x * lax.rsqrt(jnp.mean(x * x, axis=-1, keepdims=True) + EPS) * g


def _proj_kernel(x_ref, pos_ref, g1_ref, invf_ref, wnat_ref, wt_ref,
                 hg_ref, mg_ref, vc_ref, kc_ref, ks_ref, kw_ref, qt_ref, vt_ref, gt_ref):
    xn = _rms(x_ref[0], g1_ref[...]).astype(BF16)

    def nat(c0, c1):
        return _dot(xn, wnat_ref[:, c0:c1])

    half = 2 * A_W
    for c in range(2):
        hg_ref[0, :, c * half:(c + 1) * half] = nat(c * half, (c + 1) * half).astype(BF16)
        mg_ref[0, :, c * half:(c + 1) * half] = nat(4 * A_W + c * half, 4 * A_W + (c + 1) * half).astype(BF16)
    vc_ref[0] = nat(4 * A_W + 2 * D_MODEL, 4 * A_W + 2 * D_MODEL + B_KV_W)

    def tr(r0, r1):
        return _dot_nt(wt_ref[r0:r1, :], xn)

    ang = invf_ref[...] * pos_ref[0].astype(F32)
    cos = jnp.cos(ang)
    sin = jnp.sin(ang)

    def rope(xt, n_heads):
        pieces = []
        for h in range(n_heads):
            b0 = h * B_HD
            t1 = xt[b0:b0 + ROT_HALF]
            t2 = xt[b0 + ROT_HALF:b0 + ROT_DIM]
            pieces += [t1 * cos - t2 * sin, t2 * cos + t1 * sin, xt[b0 + ROT_DIM:b0 + B_HD]]
        return jnp.concatenate(pieces, axis=0)

    r_q, r_k, r_v = B_Q_W, B_Q_W + 3 * B_KV_W, B_Q_W + 5 * B_KV_W
    qt_ref[0] = (rope(tr(0, r_q), B_HEADS) * (B_HD ** -0.5 * LOG2_E)).astype(BF16)
    k3 = rope(tr(r_q, r_k), 3 * B_GROUPS).T
    kc_ref[0] = k3[:, 0:B_KV_W]
    kw_ref[0] = k3[:, 2 * B_KV_W:3 * B_KV_W].astype(BF16)
    tm = k3.shape[0]
    tok = pl.program_id(1) * tm + lax.broadcasted_iota(jnp.int32, (tm, SEL_ONEHOT_W), 0)
    lane = lax.broadcasted_iota(jnp.int32, (tm, SEL_ONEHOT_W), 1)
    ks_ref[0, :, 0:B_KV_W] = k3[:, B_KV_W:2 * B_KV_W].astype(BF16)
    ks_ref[0, :, B_KV_W:B_KV_W + SEL_ONEHOT_W] = jnp.where((tok >> SEL_SHIFT) == lane, 1.0, 0.0).astype(BF16)
    vt = tr(r_k, r_v)
    aux = jnp.where(lax.broadcasted_iota(jnp.int32, (V_AUX, tm), 0) == 0, 1.0, 0.0)
    vt_ref[0] = jnp.concatenate(
        [piece for j in range(2 * B_GROUPS) for piece in (vt[j * B_HD:(j + 1) * B_HD], aux)], axis=0).astype(BF16)
    gt_ref[0] = jax.nn.sigmoid(tr(r_v, r_v + B_GROUPS * GATE_ROWS))


def _proj(x, pos3, g1, invf, w_nat, w_t):
    bsz, s, _ = x.shape
    tm = min(512, s)
    n_nat = w_nat.shape[1]
    n_t = w_t.shape[0]
    tok = lambda w: pl.BlockSpec((1, tm, w), lambda b, t: (b, t, 0))
    trn = lambda r: pl.BlockSpec((1, r, tm), lambda b, t: (b, 0, t))
    const = lambda shp: pl.BlockSpec(shp, lambda b, t: (0,) * len(shp), pipeline_mode=pl.Buffered(1))
    out_shape = (
        jax.ShapeDtypeStruct((bsz, s, 4 * A_W), BF16),
        jax.ShapeDtypeStruct((bsz, s, 2 * D_MODEL), BF16),
        jax.ShapeDtypeStruct((bsz, s, B_KV_W), F32),
        jax.ShapeDtypeStruct((bsz, s, B_KV_W), F32),
        jax.ShapeDtypeStruct((bsz, s, KS_AUG_W), BF16),
        jax.ShapeDtypeStruct((bsz, s, B_KV_W), BF16),
        jax.ShapeDtypeStruct((bsz, B_Q_W, s), BF16),
        jax.ShapeDtypeStruct((bsz, VT_ROWS, s), BF16),
        jax.ShapeDtypeStruct((bsz, B_GROUPS * GATE_ROWS, s), F32),
    )
    return pl.pallas_call(
        _proj_kernel,
        grid=(bsz, s // tm),
        in_specs=[tok(D_MODEL), trn(1), const((1, D_MODEL)), const((ROT_HALF, 1)),
                  const((D_MODEL, n_nat)), const((n_t, D_MODEL))],
        out_specs=(tok(4 * A_W), tok(2 * D_MODEL), tok(B_KV_W), tok(B_KV_W), tok(KS_AUG_W), tok(B_KV_W),
                   trn(B_Q_W), trn(VT_ROWS), trn(B_GROUPS * GATE_ROWS)),
        out_shape=out_shape,
        compiler_params=pltpu.CompilerParams(
            dimension_semantics=("parallel", "parallel"), vmem_limit_bytes=VMEM_LIMIT),
        name="proj",
    )(x, pos3, g1, invf, w_nat, w_t)


def _hgrn_kernel(hg_ref, lbp_ref, ng_ref, o_ref, st_ref, *, n_chunks):
    @pl.when(pl.program_id(1) == 0)
    def _():
        st_ref[...] = jnp.zeros_like(st_ref)

    lbp = lbp_ref[...]
    e = jnp.exp(lbp - jnp.max(lbp, axis=0, keepdims=True))
    lb = e[0:1] / jnp.sum(e, axis=0, keepdims=True)
    ng = ng_ref[...]
    row = lax.broadcasted_iota(jnp.int32, (A_CHUNK, A_CHUNK), 0)
    col = lax.broadcasted_iota(jnp.int32, (A_CHUNK, A_CHUNK), 1)
    causal = row >= col
    tri = jnp.where(causal, 1.0, 0.0).astype(BF16)

    heads = [slice(h * A_DK, (h + 1) * A_DK) for h in range(A_HEADS)]

    def stage1(c):
        blk = hg_ref[0, c * A_CHUNK:(c + 1) * A_CHUNK, :]
        f = lb + (1.0 - lb) * jax.nn.sigmoid(blk[:, A_W:2 * A_W].astype(F32))
        lf = jnp.log(f)
        lf_hi = lf.astype(BF16)
        r1 = lf - lf_hi.astype(F32)
        lf_mid = r1.astype(BF16)
        lf_lo = (r1 - lf_mid.astype(F32)).astype(BF16)
        b = _dot(tri, lf_hi) + _dot(tri, lf_mid) + _dot(tri, lf_lo)
        return dict(c=c, k=1.0 - f, b=b)

    def stage2(s):
        c, k, b = s["c"], s["k"], s["b"]
        blk = hg_ref[0, c * A_CHUNK:(c + 1) * A_CHUNK, :]
        dec = jnp.exp(b[A_CHUNK - 1:A_CHUNK, :])
        k_dec = k * jnp.exp(-b)
        return dict(c=c, v=blk[:, 2 * A_W:3 * A_W],
                    q_dec=(blk[:, 0:A_W].astype(F32) * jnp.exp(b)).astype(BF16),
                    k_dec=k_dec.astype(BF16), k_til=(k_dec * dec).astype(BF16), dec=dec)

    def stage3(s):
        s = dict(s)
        attn = [jnp.where(causal, _dot_nt(s["q_dec"][:, sl], s["k_dec"][:, sl]), 0.0).astype(BF16)
                for sl in heads]
        s["o_intra"] = [_dot(attn[h], s["v"][:, sl]) for h, sl in enumerate(heads)]
        s["kv_t"] = [_dot_tn(s["v"][:, sl], s["k_til"][:, sl]) for sl in heads]
        return s

    def stage4(s, state):
        c = s["c"]
        g = hg_ref[0, c * A_CHUNK:(c + 1) * A_CHUNK, 3 * A_W:4 * A_W].astype(F32)
        new_state = []
        for h, sl in enumerate(heads):
            o = s["o_intra"][h] + _dot_nt(s["q_dec"][:, sl], state[h].astype(BF16))
            new_state.append(state[h] * s["dec"][:, sl] + s["kv_t"][h])
            gh = g[:, sl]
            y = _rms(o, ng) * (gh * jax.nn.sigmoid(gh))
            o_ref[0, c * A_CHUNK:(c + 1) * A_CHUNK, sl] = y.astype(BF16)
        return new_state

    state = [st_ref[h] for h in range(A_HEADS)]
    pipe = [None, None, None]
    for t in range(n_chunks + 3):
        if pipe[2] is not None:
            state = stage4(pipe[2], state)
        pipe[2] = stage3(pipe[1]) if pipe[1] is not None else None
        pipe[1] = stage2(pipe[0]) if pipe[0] is not None else None
        pipe[0] = stage1(t) if t < n_chunks else None
    for h in range(A_HEADS):
        st_ref[h] = state[h]


def _hgrn(hg, lb_param, norm_g):
    bsz, s, _ = hg.shape
    tt = min(1024, s)
    return pl.pallas_call(
        functools.partial(_hgrn_kernel, n_chunks=tt // A_CHUNK),
        grid=(bsz, s // tt),
        in_specs=[pl.BlockSpec((1, tt, 4 * A_W), lambda b, t: (b, t, 0)),
                  pl.BlockSpec(lb_param.shape, lambda b, t: (0, 0)),
                  pl.BlockSpec((1, A_DV), lambda b, t: (0, 0))],
        out_specs=pl.BlockSpec((1, tt, A_W), lambda b, t: (b, t, 0)),
        out_shape=jax.ShapeDtypeStruct((bsz, s, A_W), BF16),
        scratch_shapes=[pltpu.VMEM((A_HEADS, A_DV, A_DK), F32)],
        compiler_params=pltpu.CompilerParams(
            dimension_semantics=("parallel", "arbitrary"), vmem_limit_bytes=VMEM_LIMIT),
        name="hgrn",
    )(hg, lb_param, norm_g)


def _compress_kernel(kt_ref, vt_ref, pek_ref, pev_ref, w1k_ref, w1v_ref, w2k_ref, w2vt_ref,
                     kc_ref, vct_ref, *, n_blk):
    def hidden(t_ref, pe_ref, w1_ref):
        rows = jnp.concatenate(
            [t_ref[0, pl.ds(l, n_blk, stride=CMP_STRIDE), :] for l in range(CMP_STRIDE)], axis=1)
        halves = [(rows + pe_ref[a:a + 1, :]).astype(BF16) for a in range(2)]
        out = []
        for g in range(B_GROUPS):
            pre = _dot(halves[0], w1_ref[g, 0]) + pltpu.roll(_dot(halves[1], w1_ref[g, 1]), n_blk - 1, 0)
            out.append(jax.nn.gelu(pre).astype(BF16))
        return out

    valid_r = lax.broadcasted_iota(jnp.int32, (n_blk, B_KV_W), 0) < n_blk - 1
    valid_c = lax.broadcasted_iota(jnp.int32, (B_HD, n_blk), 1) < n_blk - 1
    hk = hidden(kt_ref, pek_ref, w1k_ref)
    hv = hidden(vt_ref, pev_ref, w1v_ref)
    kc = jnp.zeros((n_blk, B_KV_W), F32)
    for g in range(B_GROUPS):
        kc = kc + _dot(hk[g], w2k_ref[g])
        vct = _dot_nt(w2vt_ref[...], hv[g])
        vct_ref[0, g * B_HD:(g + 1) * B_HD, :] = jnp.where(valid_c, vct, 0.0).astype(BF16)
    kc_ref[0] = jnp.where(valid_r, kc, 0.0).astype(BF16)


def _compress(kc_tok, vc_tok, pek, pev, w1k, w1v, w2k, w2vt):
    bsz, s, _ = kc_tok.shape
    n_blk = s // CMP_STRIDE
    full = lambda a: pl.BlockSpec(a.shape, lambda b: (0,) * a.ndim)
    tokspec = pl.BlockSpec((1, s, B_KV_W), lambda b: (b, 0, 0))
    return pl.pallas_call(
        functools.partial(_compress_kernel, n_blk=n_blk),
        grid=(bsz,),
        in_specs=[tokspec, tokspec, full(pek), full(pev), full(w1k), full(w1v), full(w2k), full(w2vt)],
        out_specs=(pl.BlockSpec((1, n_blk, B_KV_W), lambda b: (b, 0, 0)),
                   pl.BlockSpec((1, B_KV_W, n_blk), lambda b: (b, 0, 0))),
        out_shape=(jax.ShapeDtypeStruct((bsz, n_blk, B_KV_W), BF16),
                   jax.ShapeDtypeStruct((bsz, B_KV_W, n_blk), BF16)),
        compiler_params=pltpu.CompilerParams(
            dimension_semantics=("parallel",), vmem_limit_bytes=VMEM_LIMIT),
        name="compress",
    )(kc_tok, vc_tok, pek, pev, w1k, w1v, w2k, w2vt)


KV_TILE = 512
WIN_KEYS = WINDOW + Q_BLOCK
QW = B_HPG * Q_BLOCK
LW = B_GROUPS * QW
STRIP = 256
TILE_CLASS = 1
LOOKAHEAD = 4


def _block_rank(imp, n_slc, i_range):
    sub = lax.broadcasted_iota(jnp.int32, (8, Q_BLOCK), 0)
    rank = jnp.zeros((n_slc, Q_BLOCK), F32)
    for i in i_range:
        ri = imp[i:i + 1, :]
        parts = []
        for r in range(n_slc // 8):
            blk = imp[8 * r:8 * r + 8, :]
            gt = jnp.where(ri > blk, 1.0, 0.0)
            ge = jnp.where(ri >= blk, 1.0, 0.0)
            if 8 * r + 7 <= i:
                parts.append(gt)
            elif 8 * r > i:
                parts.append(ge)
            else:
                parts.append(jnp.where(sub > i - 8 * r, ge, gt))
        rank = rank + jnp.concatenate(parts, axis=0)
    return rank


def _nsa_kernel(*refs, n_cmp_rows, n_slc, n_key_tiles):
    qi = pl.program_id(1)
    n_tiles = (qi * Q_BLOCK + KV_TILE - 1) // KV_TILE
    n_class = (n_key_tiles + TILE_CLASS - 1) // TILE_CLASS
    for cb in range(n_class + 1):
        n_max = min(cb * TILE_CLASS, n_key_tiles)
        n_min = max((cb - 1) * TILE_CLASS + 1, 0) if cb else 0
        q0_min = KV_TILE * (n_min - 1) + Q_BLOCK if n_min else 0
        pl.when((n_tiles + TILE_CLASS - 1) // TILE_CLASS == cb)(functools.partial(
            _nsa_step, n_max, q0_min, qi, *refs, n_cmp_rows=n_cmp_rows, n_slc=n_slc))


def _nsa_step(n_tiles, q0_min, qi, qt_ref, gt_ref, kc_ref, vct_ref, ks_ref, kw_ref, vt_ref, ovt_ref, cend_ref,
              o_ref, rhs_ref, *, n_cmp_rows, n_slc):
    q0 = qi * Q_BLOCK
    n_blk = min(n_slc, 16 * ((KV_TILE // SEL_LEN * n_tiles + Q_BLOCK // SEL_LEN + 15) // 16))
    n_cmp = min(n_cmp_rows, 128 * (((KV_TILE * n_tiles + Q_BLOCK) // CMP_STRIDE + 127) // 128))
    steady = q0_min >= WINDOW

    qblk = qt_ref[0]
    zq = jnp.zeros((B_HD, QW), BF16)
    qrows = []
    for g in range(B_GROUPS):
        qg = jnp.concatenate(
            [qblk[(g * B_HPG + h) * B_HD:(g * B_HPG + h + 1) * B_HD, :] for h in range(B_HPG)], axis=1)
        qrows.append(jnp.concatenate([qg if gg == g else zq for gg in range(B_GROUPS)], axis=1))
    qbd = jnp.concatenate(qrows, axis=0)

    qs = pl.multiple_of(q0, Q_BLOCK)
    ws = pl.multiple_of(jnp.maximum(q0 - WINDOW, 0), Q_BLOCK)
    sc = _dot(kc_ref[0, 0:n_cmp, :], qbd)
    sw = _dot(kw_ref[0, pl.ds(ws, WIN_KEYS), :], qbd)
    s_d = _dot(ks_ref[0, pl.ds(qs, Q_BLOCK), 0:B_KV_W], qbd)

    okc = cend_ref[0:n_cmp, :] <= q0
    sc = jnp.where(okc, sc, NEG)
    pc = jnp.exp2(sc - jnp.max(sc, axis=0, keepdims=True))
    l_c = jnp.sum(pc, axis=0, keepdims=True)
    if n_tiles == 0:
        pc = jnp.where(okc, pc, 0.0)
        l_c = jnp.maximum(jnp.sum(pc, axis=0, keepdims=True), 1e-30)
    pc = pc / l_c
    pcb = pc.astype(BF16)
    oc = [_dot(vct_ref[0, g * B_HD:(g + 1) * B_HD, 0:n_cmp], pcb[:, g * QW:(g + 1) * QW])
          for g in range(B_GROUPS)]

    def strip_update(state, s, v):
        m, acc = state
        m_new = jnp.maximum(m, jnp.max(s, axis=0, keepdims=True))
        pb = jnp.exp2(s - m_new).astype(BF16)
        return m_new, jnp.exp2(m - m_new) * acc + _dot(v, pb)

    def v_block(j, start, size):
        return vt_ref[0, j * V_BLK:(j + 1) * V_BLK, pl.ds(start, size)]

    n_str = LW // STRIP
    grp = lambda j: j * STRIP // QW
    lanes = lambda j: slice(j * STRIP, (j + 1) * STRIP)
    fresh = lambda: (jnp.full((1, STRIP), NEG, F32), jnp.zeros((V_BLK, STRIP), F32))
    row_q = lax.broadcasted_iota(jnp.int32, (Q_BLOCK, STRIP), 0)
    tq_q = lax.broadcasted_iota(jnp.int32, (Q_BLOCK, STRIP), 1) & (Q_BLOCK - 1)

    jrow = lax.broadcasted_iota(jnp.int32, (n_blk, Q_BLOCK), 0)
    cur = (q0 + lax.broadcasted_iota(jnp.int32, (n_blk, Q_BLOCK), 1)) >> SEL_SHIFT
    causal_blk = jrow <= cur
    forced = (jrow == 0) | (jrow == cur) | (jrow == cur - 1)
    ovt = ovt_ref[0:n_blk, 0:n_cmp]
    imps = []
    for g in range(B_GROUPS):
        psum = pc[:, g * QW:g * QW + Q_BLOCK]
        for h in range(1, B_HPG):
            psum = psum + pc[:, g * QW + h * Q_BLOCK:g * QW + (h + 1) * Q_BLOCK]
        p_hi = psum.astype(BF16)
        p_lo = (psum - p_hi.astype(F32)).astype(BF16)
        imp = _dot(ovt, p_hi) + _dot(ovt, p_lo)
        imp = jnp.where(forced & causal_blk, FORCE_SCORE, imp)
        imps.append(jnp.where(causal_blk, imp, NEG))

    biases = []
    for g in range(B_GROUPS):
        sel = (_block_rank(imps[g], n_blk, range(n_blk)) < float(min(SEL_TOPK, n_slc))) & causal_blk
        sel = sel & (jrow // (Q_BLOCK // SEL_LEN) != qi)
        biases += [jnp.where(sel, 0.0, NEG).astype(BF16)] * B_HPG
    bias = jnp.concatenate(biases, axis=1)
    rhs_ref[...] = jnp.concatenate(
        [qbd, bias, jnp.zeros((KS_AUG_W - B_KV_W - n_blk, LW), BF16)], axis=0)

    states = [
        strip_update(fresh(), jnp.where(row_q <= tq_q, s_d[:, lanes(j)], NEG), v_block(grp(j), qs, Q_BLOCK))
        for j in range(n_str)]

    order = [(kt, j) for kt in range(n_tiles) for j in range(n_str)]

    def strip_scores(kt, j):
        return _dot(ks_ref[0, kt * KV_TILE:(kt + 1) * KV_TILE, :], rhs_ref[:, lanes(j)])

    pending = [strip_scores(*order[i]) for i in range(min(LOOKAHEAD, len(order)))]
    for i, (kt, j) in enumerate(order):
        s = pending.pop(0)
        if i + LOOKAHEAD < len(order):
            pending.append(strip_scores(*order[i + LOOKAHEAD]))
        states[j] = strip_update(states[j], s, v_block(grp(j), kt * KV_TILE, KV_TILE))

    acc_w = []
    for j in range(n_str):
        swj = sw[:, lanes(j)]
        if steady:
            lo = jnp.where(row_q > tq_q, swj[0:Q_BLOCK], NEG)
            hi = jnp.where(row_q <= tq_q, swj[WINDOW:WIN_KEYS], NEG)
            swj = jnp.concatenate([lo, swj[Q_BLOCK:WINDOW], hi], axis=0)
        else:
            krow = lax.broadcasted_iota(jnp.int32, (WIN_KEYS, STRIP), 0)
            tqw = lax.broadcasted_iota(jnp.int32, (WIN_KEYS, STRIP), 1) & (Q_BLOCK - 1)
            dist = (q0 - ws) + tqw - krow
            swj = jnp.where(lax.bitcast_convert_type(dist, jnp.uint32) < jnp.uint32(WINDOW), swj, NEG)
        acc_w.append(strip_update(fresh(), swj, v_block(B_GROUPS + grp(j), ws, WIN_KEYS))[1])

    per_g = n_str // B_GROUPS
    acc_s = [jnp.concatenate([states[g * per_g + u][1] for u in range(per_g)], axis=1) for g in range(B_GROUPS)]
    acc_w = [jnp.concatenate(acc_w[g * per_g:(g + 1) * per_g], axis=1) for g in range(B_GROUPS)]

    gts = gt_ref[0]
    for g in range(B_GROUPS):
        o_s = acc_s[g][0:B_HD] / acc_s[g][B_HD:B_HD + 1]
        o_w = acc_w[g][0:B_HD] / acc_w[g][B_HD:B_HD + 1]
        for h in range(B_HPG):
            sl = slice(h * Q_BLOCK, (h + 1) * Q_BLOCK)
            r = g * GATE_ROWS + 3 * h
            o_h = gts[r:r + 1] * oc[g][:, sl] + gts[r + 1:r + 2] * o_s[:, sl] + gts[r + 2:r + 3] * o_w[:, sl]
            o_ref[0, (g * B_HPG + h) * B_HD:(g * B_HPG + h + 1) * B_HD, :] = o_h.astype(BF16)


def _nsa(qt, gt, kc, vct, ks, kw, vt, ovt):
    bsz, _, s = qt.shape
    n_cmp_rows = kc.shape[1]
    n_slc = s // SEL_LEN
    assert n_slc <= SEL_ONEHOT_W and s % KV_TILE == 0 and s >= WIN_KEYS
    whole = lambda a: pl.BlockSpec((1,) + a.shape[1:], lambda b, i: (b, 0, 0))
    cend = jnp.asarray((np.arange(n_cmp_rows)[:, None] * CMP_STRIDE + CMP_LEN - 1)
                       - (np.arange(LW)[None, :] % Q_BLOCK), jnp.int32)
    return pl.pallas_call(
        functools.partial(_nsa_kernel, n_cmp_rows=n_cmp_rows, n_slc=n_slc, n_key_tiles=s // KV_TILE),
        grid=(bsz, s // Q_BLOCK),
        in_specs=[
            pl.BlockSpec((1, B_Q_W, Q_BLOCK), lambda b, i: (b, 0, i)),
            pl.BlockSpec((1, B_GROUPS * GATE_ROWS, Q_BLOCK), lambda b, i: (b, 0, i)),
            whole(kc), whole(vct), whole(ks), whole(kw), whole(vt),
            pl.BlockSpec(ovt.shape, lambda b, i: (0, 0)),
            pl.BlockSpec(cend.shape, lambda b, i: (0, 0)),
        ],
        out_specs=pl.BlockSpec((1, B_Q_W, Q_BLOCK), lambda b, i: (b, 0, i)),
        out_shape=jax.ShapeDtypeStruct((bsz, B_Q_W, s), BF16),
        scratch_shapes=[pltpu.VMEM((KS_AUG_W, LW), BF16)],
        compiler_params=pltpu.CompilerParams(
            dimension_semantics=("parallel", "arbitrary"), vmem_limit_bytes=VMEM_LIMIT),
        name="nsa",
    )(qt, gt, kc, vct, ks, kw, vt, ovt, cend)


FF_CHUNK = 1024


def _tail_kernel(x_ref, ya_ref, ybt_ref, mg_ref, wa_ref, wb_ref, wo_ref, g2_ref, w1_ref, w2_ref, gf_ref, o_ref):
    pa = _dot(ya_ref[0], wa_ref[...])
    pb = _dot_tn(ybt_ref[0], wb_ref[...])
    mg = mg_ref[0].astype(F32)
    merged = jax.nn.sigmoid(mg[:, 0:D_MODEL]) * pa + jax.nn.sigmoid(mg[:, D_MODEL:2 * D_MODEL]) * pb
    h = x_ref[0] + _dot(merged.astype(BF16), wo_ref[...])
    hn = _rms(h, g2_ref[...]).astype(BF16)
    acc = jnp.zeros_like(h)
    for c in range(D_FF // FF_CHUNK):
        z = _dot(hn, w1_ref[:, c * FF_CHUNK:(c + 1) * FF_CHUNK])
        acc = acc + _dot(jnp.square(jnp.maximum(z, 0.0)).astype(BF16), w2_ref[c * FF_CHUNK:(c + 1) * FF_CHUNK, :])
    o_ref[0] = _rms(h + acc, gf_ref[...])


def _tail(x, ya, ybt, mg, wa, wb, wo, g2, w1, w2, gf):
    bsz, s, _ = x.shape
    tm = min(512, s)
    tok = lambda w: pl.BlockSpec((1, tm, w), lambda b, t: (b, t, 0))
    const = lambda a: pl.BlockSpec(a.shape, lambda b, t: (0,) * a.ndim, pipeline_mode=pl.Buffered(1))
    return pl.pallas_call(
        _tail_kernel,
        grid=(bsz, s // tm),
        in_specs=[tok(D_MODEL), tok(A_W), pl.BlockSpec((1, B_Q_W, tm), lambda b, t: (b, 0, t)), tok(2 * D_MODEL),
                  const(wa), const(wb), const(wo), const(g2), const(w1), const(w2), const(gf)],
        out_specs=tok(D_MODEL),
        out_shape=jax.ShapeDtypeStruct((bsz, s, D_MODEL), F32),
        compiler_params=pltpu.CompilerParams(
            dimension_semantics=("parallel", "parallel"), vmem_limit_bytes=VMEM_LIMIT),
        name="tail",
    )(x, ya, ybt, mg, wa, wb, wo, g2, w1, w2, gf)


def _pad_groups(a, axis):
    outs = []
    for g in range(B_GROUPS):
        pads = [(0, 0)] * a.ndim
        pads[axis] = (g * B_HD, (B_GROUPS - 1 - g) * B_HD)
        outs.append(jnp.pad(a, pads))
    return jnp.stack(outs)


def _overlap_t(s):
    n_cmp = (s - CMP_LEN) // CMP_STRIDE + 1
    n_slc = s // SEL_LEN
    cs = np.arange(n_cmp)[:, None] * CMP_STRIDE
    ss = np.arange(n_slc)[None, :] * SEL_LEN
    ov = np.clip(np.minimum(cs + CMP_LEN, ss + SEL_LEN) - np.maximum(cs, ss), 0, None) / CMP_LEN
    ovt = np.zeros((n_slc, s // CMP_STRIDE), np.float32)
    ovt[:, :n_cmp] = ov.T
    return jnp.asarray(ovt, BF16)


def kernel(x, positions, norm1_g, w_in, lb_param, hgrn_norm_g, cmp_pe_k, cmp_pe_v, cmp_w1_k, cmp_w2_k,
           cmp_w1_v, cmp_w2_v, w_br_a, w_br_b, w_out, norm2_g, w_ff1, w_ff2, final_g):
    bsz, s, _ = x.shape
    assert norm1_g.shape[0] == 1, "single-layer block"

    w = w_in[0]
    o_bq = 4 * A_W
    o_kv = o_bq + B_Q_W
    o_gate = o_kv + 6 * B_KV_W
    o_mg = o_gate + 3 * B_HEADS
    kv = lambda i: w[:, o_kv + i * B_KV_W:o_kv + (i + 1) * B_KV_W]
    w_nat = jnp.concatenate([w[:, 0:o_bq], w[:, o_mg:o_mg + 2 * D_MODEL], kv(1)], axis=1).astype(BF16)
    wg = w[:, o_gate:o_mg].reshape(D_MODEL, B_GROUPS, 3 * B_HPG)
    wg = jnp.pad(wg, ((0, 0), (0, 0), (0, GATE_ROWS - 3 * B_HPG))).reshape(D_MODEL, B_GROUPS * GATE_ROWS)
    w_t = jnp.concatenate([w[:, o_bq:o_kv], kv(0), kv(2), kv(4), kv(3), kv(5), wg], axis=1).T.astype(BF16)
    invf = (ROPE_THETA ** (-jnp.arange(0, ROT_DIM, 2, dtype=F32) / ROT_DIM)).reshape(ROT_HALF, 1)

    hg, mg, vc_tok, kc_tok, ks, kw, qt, vt, gt = _proj(
        x, positions.reshape(bsz, 1, s), norm1_g, invf, w_nat, w_t)

    ya = _hgrn(hg, lb_param, hgrn_norm_g)

    half_w = CMP_LEN // 2 * B_KV_W

    def w1_halves(w1):
        padded = _pad_groups(w1.reshape(CMP_LEN, B_HD, CMP_HIDDEN), 1)
        return padded.reshape(B_GROUPS, 2, half_w, CMP_HIDDEN).astype(BF16)

    pe_halves = lambda pe: jnp.tile(pe, (1, B_GROUPS)).reshape(2, half_w)
    kc, vct = _compress(
        kc_tok, vc_tok, pe_halves(cmp_pe_k[0]), pe_halves(cmp_pe_v[0]), w1_halves(cmp_w1_k[0]),
        w1_halves(cmp_w1_v[0]), _pad_groups(cmp_w2_k[0], 1).astype(BF16), cmp_w2_v[0].T.astype(BF16))

    ybt = _nsa(qt, gt, kc, vct, ks, kw, vt, _overlap_t(s))

    return _tail(x, ya, ybt, mg, w_br_a[0].astype(BF16), w_br_b[0].astype(BF16), w_out[0].astype(BF16),
                 norm2_g, w_ff1[0].astype(BF16), w_ff2[0].astype(BF16), final_g.reshape(1, D_MODEL))
```

```python
import functools

import jax
import jax.numpy as jnp
import numpy as np
from jax import lax
from jax.experimental import pallas as pl
from jax.experimental.pallas import tpu as pltpu

F32 = jnp.float32
BF16 = jnp.bfloat16

D_MODEL = 1024
A_HEADS = 4
A_DK = 128
A_DV = 128
A_CHUNK = 64
A_W = A_HEADS * A_DK
B_HEADS = 8
B_GROUPS = 2
B_HPG = B_HEADS // B_GROUPS
B_HD = 64
B_Q_W = B_HEADS * B_HD
B_KV_W = B_GROUPS * B_HD
CMP_LEN = 32
CMP_STRIDE = 16
CMP_HIDDEN = 256
SEL_LEN = 64
SEL_TOPK = 16
WINDOW = 512
Q_BLOCK = 128
FORCE_SCORE = 1e4
NEG = -1e30
ROPE_THETA = 500000.0
ROT_DIM = B_HD // 4
ROT_HALF = ROT_DIM // 2
D_FF = 4 * D_MODEL
EPS = 1e-6
LOG2_E = 1.4426950408889634
GATE_ROWS = 16
SEL_SHIFT = SEL_LEN.bit_length() - 1
SEL_ONEHOT_W = 128
KS_AUG_W = B_KV_W + SEL_ONEHOT_W
V_AUX = 16
V_BLK = B_HD + V_AUX
VT_ROWS = 2 * B_GROUPS * V_BLK

V7X_VMEM_BYTES = 64 * 1024 * 1024
VMEM_LIMIT = V7X_VMEM_BYTES * 7 // 8

NT = (((1,), (1,)), ((), ()))
TN = (((0,), (0,)), ((), ()))


def _dot(a, b):
    return jnp.dot(a, b, preferred_element_type=F32)


def _dot_nt(a, b):
    return lax.dot_general(a, b, NT, preferred_element_type=F32)


def _dot_tn(a, b):
    return lax.dot_general(a, b, TN, preferred_element_type=F32)


def _rms(x, g):
    return x * lax.rsqrt(jnp.mean(x * x, axis=-1, keepdims=True) + EPS) * g


def _proj_kernel(x_ref, pos_ref, g1_ref, invf_ref, wnat_ref, wt_ref,
                 hg_ref, mg_ref, vc_ref, kc_ref, ks_ref, kw_ref, qt_ref, vt_ref, gt_ref):
    xn = _rms(x_ref[0], g1_ref[...]).astype(BF16)

    def nat(c0, c1):
        return _dot(xn, wnat_ref[:, c0:c1])

    half = 2 * A_W
    for c in range(2):
        hg_ref[0, :, c * half:(c + 1) * half] = nat(c * half, (c + 1) * half).astype(BF16)
        mg_ref[0, :, c * half:(c + 1) * half] = nat(4 * A_W + c * half, 4 * A_W + (c + 1) * half).astype(BF16)
    vc_ref[0] = nat(4 * A_W + 2 * D_MODEL, 4 * A_W + 2 * D_MODEL + B_KV_W)

    def tr(r0, r1):
        return _dot_nt(wt_ref[r0:r1, :], xn)

    ang = invf_ref[...] * pos_ref[0].astype(F32)
    cos = jnp.cos(ang)
    sin = jnp.sin(ang)

    def rope(xt, n_heads):
        pieces = []
        for h in range(n_heads):
            b0 = h * B_HD
            t1 = xt[b0:b0 + ROT_HALF]
            t2 = xt[b0 + ROT_HALF:b0 + ROT_DIM]
            pieces += [t1 * cos - t2 * sin, t2 * cos + t1 * sin, xt[b0 + ROT_DIM:b0 + B_HD]]
        return jnp.concatenate(pieces, axis=0)

    r_q, r_k, r_v = B_Q_W, B_Q_W + 3 * B_KV_W, B_Q_W + 5 * B_KV_W
    qt_ref[0] = (rope(tr(0, r_q), B_HEADS) * (B_HD ** -0.5 * LOG2_E)).astype(BF16)
    k3 = rope(tr(r_q, r_k), 3 * B_GROUPS).T
    kc_ref[0] = k3[:, 0:B_KV_W]
    kw_ref[0] = k3[:, 2 * B_KV_W:3 * B_KV_W].astype(BF16)
    tm = k3.shape[0]
    tok = pl.program_id(1) * tm + lax.broadcasted_iota(jnp.int32, (tm, SEL_ONEHOT_W), 0)
    lane = lax.broadcasted_iota(jnp.int32, (tm, SEL_ONEHOT_W), 1)
    ks_ref[0, :, 0:B_KV_W] = k3[:, B_KV_W:2 * B_KV_W].astype(BF16)
    ks_ref[0, :, B_KV_W:B_KV_W + SEL_ONEHOT_W] = jnp.where((tok >> SEL_SHIFT) == lane, 1.0, 0.0).astype(BF16)
    vt = tr(r_k, r_v)
    aux = jnp.where(lax.broadcasted_iota(jnp.int32, (V_AUX, tm), 0) == 0, 1.0, 0.0)
    vt_ref[0] = jnp.concatenate(
        [piece for j in range(2 * B_GROUPS) for piece in (vt[j * B_HD:(j + 1) * B_HD], aux)], axis=0).astype(BF16)
    gt_ref[0] = jax.nn.sigmoid(tr(r_v, r_v + B_GROUPS * GATE_ROWS))


def _proj(x, pos3, g1, invf, w_nat, w_t):
    bsz, s, _ = x.shape
    tm = min(512, s)
    n_nat = w_nat.shape[1]
    n_t = w_t.shape[0]
    tok = lambda w: pl.BlockSpec((1, tm, w), lambda b, t: (b, t, 0))
    trn = lambda r: pl.BlockSpec((1, r, tm), lambda b, t: (b, 0, t))
    const = lambda shp: pl.BlockSpec(shp, lambda b, t: (0,) * len(shp), pipeline_mode=pl.Buffered(1))
    out_shape = (
        jax.ShapeDtypeStruct((bsz, s, 4 * A_W), BF16),
        jax.ShapeDtypeStruct((bsz, s, 2 * D_MODEL), BF16),
        jax.ShapeDtypeStruct((bsz, s, B_KV_W), F32),
        jax.ShapeDtypeStruct((bsz, s, B_KV_W), F32),
        jax.ShapeDtypeStruct((bsz, s, KS_AUG_W), BF16),
        jax.ShapeDtypeStruct((bsz, s, B_KV_W), BF16),
        jax.ShapeDtypeStruct((bsz, B_Q_W, s), BF16),
        jax.ShapeDtypeStruct((bsz, VT_ROWS, s), BF16),
        jax.ShapeDtypeStruct((bsz, B_GROUPS * GATE_ROWS, s), F32),
    )
    return pl.pallas_call(
        _proj_kernel,
        grid=(bsz, s // tm),
        in_specs=[tok(D_MODEL), trn(1), const((1, D_MODEL)), const((ROT_HALF, 1)),
                  const((D_MODEL, n_nat)), const((n_t, D_MODEL))],
        out_specs=(tok(4 * A_W), tok(2 * D_MODEL), tok(B_KV_W), tok(B_KV_W), tok(KS_AUG_W), tok(B_KV_W),
                   trn(B_Q_W), trn(VT_ROWS), trn(B_GROUPS * GATE_ROWS)),
        out_shape=out_shape,
        compiler_params=pltpu.CompilerParams(
            dimension_semantics=("parallel", "parallel"), vmem_limit_bytes=VMEM_LIMIT),
        name="proj",
    )(x, pos3, g1, invf, w_nat, w_t)


def _hgrn_kernel(hg_ref, lbp_ref, ng_ref, o_ref, st_ref, *, n_chunks):
    @pl.when(pl.program_id(1) == 0)
    def _():
        st_ref[...] = jnp.zeros_like(st_ref)

    lbp = lbp_ref[...]
    e = jnp.exp(lbp - jnp.max(lbp, axis=0, keepdims=True))
    lb = e[0:1] / jnp.sum(e, axis=0, keepdims=True)
    ng = ng_ref[...]
    row = lax.broadcasted_iota(jnp.int32, (A_CHUNK, A_CHUNK), 0)
    col = lax.broadcasted_iota(jnp.int32, (A_CHUNK, A_CHUNK), 1)
    causal = row >= col
    tri = jnp.where(causal, 1.0, 0.0).astype(BF16)

    heads = [slice(h * A_DK, (h + 1) * A_DK) for h in range(A_HEADS)]

    def stage1(c):
        blk = hg_ref[0, c * A_CHUNK:(c + 1) * A_CHUNK, :]
        f = lb + (1.0 - lb) * jax.nn.sigmoid(blk[:, A_W:2 * A_W].astype(F32))
        lf = jnp.log(f)
        lf_hi = lf.astype(BF16)
        r1 = lf - lf_hi.astype(F32)
        lf_mid = r1.astype(BF16)
        lf_lo = (r1 - lf_mid.astype(F32)).astype(BF16)
        b = _dot(tri, lf_hi) + _dot(tri, lf_mid) + _dot(tri, lf_lo)
        return dict(c=c, k=1.0 - f, b=b)

    def stage2(s):
        c, k, b = s["c"], s["k"], s["b"]
        blk = hg_ref[0, c * A_CHUNK:(c + 1) * A_CHUNK, :]
        dec = jnp.exp(b[A_CHUNK - 1:A_CHUNK, :])
        k_dec = k * jnp.exp(-b)
        return dict(c=c, v=blk[:, 2 * A_W:3 * A_W],
                    q_dec=(blk[:, 0:A_W].astype(F32) * jnp.exp(b)).astype(BF16),
                    k_dec=k_dec.astype(BF16), k_til=(k_dec * dec).astype(BF16), dec=dec)

    def stage3(s):
        s = dict(s)
        attn = [jnp.where(causal, _dot_nt(s["q_dec"][:, sl], s["k_dec"][:, sl]), 0.0).astype(BF16)
                for sl in heads]
        s["o_intra"] = [_dot(attn[h], s["v"][:, sl]) for h, sl in enumerate(heads)]
        s["kv_t"] = [_dot_tn(s["v"][:, sl], s["k_til"][:, sl]) for sl in heads]
        return s

    def stage4(s, state):
        c = s["c"]
        g = hg_ref[0, c * A_CHUNK:(c + 1) * A_CHUNK, 3 * A_W:4 * A_W].astype(F32)
        new_state = []
        for h, sl in enumerate(heads):
            o = s["o_intra"][h] + _dot_nt(s["q_dec"][:, sl], state[h].astype(BF16))
            new_state.append(state[h] * s["dec"][:, sl] + s["kv_t"][h])
            gh = g[:, sl]
            y = _rms(o, ng) * (gh * jax.nn.sigmoid(gh))
            o_ref[0, c * A_CHUNK:(c + 1) * A_CHUNK, sl] = y.astype(BF16)
        return new_state

    state = [st_ref[h] for h in range(A_HEADS)]
    pipe = [None, None, None]
    for t in range(n_chunks + 3):
        if pipe[2] is not None:
            state = stage4(pipe[2], state)
        pipe[2] = stage3(pipe[1]) if pipe[1] is not None else None
        pipe[1] = stage2(pipe[0]) if pipe[0] is not None else None
        pipe[0] = stage1(t) if t < n_chunks else None
    for h in range(A_HEADS):
        st_ref[h] = state[h]


def _hgrn(hg, lb_param, norm_g):
    bsz, s, _ = hg.shape
    tt = min(2048, s)
    return pl.pallas_call(
        functools.partial(_hgrn_kernel, n_chunks=tt // A_CHUNK),
        grid=(bsz, s // tt),
        in_specs=[pl.BlockSpec((1, tt, 4 * A_W), lambda b, t: (b, t, 0)),
                  pl.BlockSpec(lb_param.shape, lambda b, t: (0, 0)),
                  pl.BlockSpec((1, A_DV), lambda b, t: (0, 0))],
        out_specs=pl.BlockSpec((1, tt, A_W), lambda b, t: (b, t, 0)),
        out_shape=jax.ShapeDtypeStruct((bsz, s, A_W), BF16),
        scratch_shapes=[pltpu.VMEM((A_HEADS, A_DV, A_DK), F32)],
        compiler_params=pltpu.CompilerParams(
            dimension_semantics=("parallel", "arbitrary"), vmem_limit_bytes=VMEM_LIMIT),
        name="hgrn",
    )(hg, lb_param, norm_g)


def _compress_kernel(kt_ref, vt_ref, pek_ref, pev_ref, w1k_ref, w1v_ref, w2k_ref, w2vt_ref,
                     kc_ref, vct_ref, *, n_blk):
    def hidden(t_ref, pe_ref, w1_ref):
        rows = jnp.concatenate(
            [t_ref[0, pl.ds(l, n_blk, stride=CMP_STRIDE), :] for l in range(CMP_STRIDE)], axis=1)
        halves = [(rows + pe_ref[a:a + 1, :]).astype(BF16) for a in range(2)]
        out = []
        for g in range(B_GROUPS):
            pre = _dot(halves[0], w1_ref[g, 0]) + pltpu.roll(_dot(halves[1], w1_ref[g, 1]), n_blk - 1, 0)
            out.append(jax.nn.gelu(pre).astype(BF16))
        return out

    valid_r = lax.broadcasted_iota(jnp.int32, (n_blk, B_KV_W), 0) < n_blk - 1
    valid_c = lax.broadcasted_iota(jnp.int32, (B_HD, n_blk), 1) < n_blk - 1
    hk = hidden(kt_ref, pek_ref, w1k_ref)
    hv = hidden(vt_ref, pev_ref, w1v_ref)
    kc = jnp.zeros((n_blk, B_KV_W), F32)
    for g in range(B_GROUPS):
        kc = kc + _dot(hk[g], w2k_ref[g])
        vct = _dot_nt(w2vt_ref[...], hv[g])
        vct_ref[0, g * B_HD:(g + 1) * B_HD, :] = jnp.where(valid_c, vct, 0.0).astype(BF16)
    kc_ref[0] = jnp.where(valid_r, kc, 0.0).astype(BF16)


def _compress(kc_tok, vc_tok, pek, pev, w1k, w1v, w2k, w2vt):
    bsz, s, _ = kc_tok.shape
    n_blk = s // CMP_STRIDE
    full = lambda a: pl.BlockSpec(a.shape, lambda b: (0,) * a.ndim)
    tokspec = pl.BlockSpec((1, s, B_KV_W), lambda b: (b, 0, 0))
    return pl.pallas_call(
        functools.partial(_compress_kernel, n_blk=n_blk),
        grid=(bsz,),
        in_specs=[tokspec, tokspec, full(pek), full(pev), full(w1k), full(w1v), full(w2k), full(w2vt)],
        out_specs=(pl.BlockSpec((1, n_blk, B_KV_W), lambda b: (b, 0, 0)),
                   pl.BlockSpec((1, B_KV_W, n_blk), lambda b: (b, 0, 0))),
        out_shape=(jax.ShapeDtypeStruct((bsz, n_blk, B_KV_W), BF16),
                   jax.ShapeDtypeStruct((bsz, B_KV_W, n_blk), BF16)),
        compiler_params=pltpu.CompilerParams(
            dimension_semantics=("parallel",), vmem_limit_bytes=VMEM_LIMIT),
        name="compress",
    )(kc_tok, vc_tok, pek, pev, w1k, w1v, w2k, w2vt)


KV_TILE = 512
WIN_KEYS = WINDOW + Q_BLOCK
QW = B_HPG * Q_BLOCK
LW = B_GROUPS * QW
STRIP = 256
TILE_CLASS = 1
LOOKAHEAD = 4


def _block_rank(imp, n_slc):
    sub = lax.broadcasted_iota(jnp.int32, (8, Q_BLOCK), 0)
    rank = jnp.zeros((n_slc, Q_BLOCK), F32)
    for i in range(n_slc):
        ri = imp[i:i + 1, :]
        parts = []
        for r in range(n_slc // 8):
            blk = imp[8 * r:8 * r + 8, :]
            gt = jnp.where(ri > blk, 1.0, 0.0)
            ge = jnp.where(ri >= blk, 1.0, 0.0)
            if 8 * r + 7 <= i:
                parts.append(gt)
            elif 8 * r > i:
                parts.append(ge)
            else:
                parts.append(jnp.where(sub > i - 8 * r, ge, gt))
        rank = rank + jnp.concatenate(parts, axis=0)
    return rank


def _nsa_kernel(*refs, n_cmp_rows, n_slc, n_key_tiles):
    qi = pl.program_id(1)
    n_tiles = (qi * Q_BLOCK + KV_TILE - 1) // KV_TILE
    n_class = (n_key_tiles + TILE_CLASS - 1) // TILE_CLASS
    for cb in range(n_class + 1):
        n_max = min(cb * TILE_CLASS, n_key_tiles)
        n_min = max((cb - 1) * TILE_CLASS + 1, 0) if cb else 0
        q0_min = KV_TILE * (n_min - 1) + Q_BLOCK if n_min else 0
        pl.when((n_tiles + TILE_CLASS - 1) // TILE_CLASS == cb)(functools.partial(
            _nsa_step, n_max, q0_min, qi, *refs, n_cmp_rows=n_cmp_rows, n_slc=n_slc))


def _nsa_step(n_tiles, q0_min, qi, qt_ref, gt_ref, kc_ref, vct_ref, ks_ref, kw_ref, vt_ref, ovt_ref, cend_ref,
              o_ref, rhs_ref, *, n_cmp_rows, n_slc):
    q0 = qi * Q_BLOCK
    n_blk = min(n_slc, 16 * ((KV_TILE // SEL_LEN * n_tiles + Q_BLOCK // SEL_LEN + 15) // 16))
    n_cmp = min(n_cmp_rows, 128 * (((KV_TILE * n_tiles + Q_BLOCK) // CMP_STRIDE + 127) // 128))
    steady = q0_min >= WINDOW

    qblk = qt_ref[0]
    zq = jnp.zeros((B_HD, QW), BF16)
    qrows = []
    for g in range(B_GROUPS):
        qg = jnp.concatenate(
            [qblk[(g * B_HPG + h) * B_HD:(g * B_HPG + h + 1) * B_HD, :] for h in range(B_HPG)], axis=1)
        qrows.append(jnp.concatenate([qg if gg == g else zq for gg in range(B_GROUPS)], axis=1))
    qbd = jnp.concatenate(qrows, axis=0)

    qs = pl.multiple_of(q0, Q_BLOCK)
    ws = pl.multiple_of(jnp.maximum(q0 - WINDOW, 0), Q_BLOCK)
    sc = _dot(kc_ref[0, 0:n_cmp, :], qbd)
    sw = _dot(kw_ref[0, pl.ds(ws, WIN_KEYS), :], qbd)
    s_d = _dot(ks_ref[0, pl.ds(qs, Q_BLOCK), 0:B_KV_W], qbd)

    okc = cend_ref[0:n_cmp, :] <= q0
    sc = jnp.where(okc, sc, NEG)
    pc = jnp.exp2(sc - jnp.max(sc, axis=0, keepdims=True))
    l_c = jnp.sum(pc, axis=0, keepdims=True)
    if n_tiles == 0:
        pc = jnp.where(okc, pc, 0.0)
        l_c = jnp.maximum(jnp.sum(pc, axis=0, keepdims=True), 1e-30)
    pc = pc / l_c
    pcb = pc.astype(BF16)
    oc = [_dot(vct_ref[0, g * B_HD:(g + 1) * B_HD, 0:n_cmp], pcb[:, g * QW:(g + 1) * QW])
          for g in range(B_GROUPS)]

    def strip_update(state, s, v):
        m, acc = state
        m_new = jnp.maximum(m, jnp.max(s, axis=0, keepdims=True))
        pb = jnp.exp2(s - m_new).astype(BF16)
        return m_new, jnp.exp2(m - m_new) * acc + _dot(v, pb)

    def v_block(j, start, size):
        return vt_ref[0, j * V_BLK:(j + 1) * V_BLK, pl.ds(start, size)]

    n_str = LW // STRIP
    grp = lambda j: j * STRIP // QW
    lanes = lambda j: slice(j * STRIP, (j + 1) * STRIP)
    fresh = lambda: (jnp.full((1, STRIP), NEG, F32), jnp.zeros((V_BLK, STRIP), F32))
    row_q = lax.broadcasted_iota(jnp.int32, (Q_BLOCK, STRIP), 0)
    tq_q = lax.broadcasted_iota(jnp.int32, (Q_BLOCK, STRIP), 1) & (Q_BLOCK - 1)

    jrow = lax.broadcasted_iota(jnp.int32, (n_blk, Q_BLOCK), 0)
    cur = (q0 + lax.broadcasted_iota(jnp.int32, (n_blk, Q_BLOCK), 1)) >> SEL_SHIFT
    causal_blk = jrow <= cur
    forced = (jrow == 0) | (jrow == cur) | (jrow == cur - 1)
    ovt = ovt_ref[0:n_blk, 0:n_cmp]
    imps = []
    for g in range(B_GROUPS):
        psum = pc[:, g * QW:g * QW + Q_BLOCK]
        for h in range(1, B_HPG):
            psum = psum + pc[:, g * QW + h * Q_BLOCK:g * QW + (h + 1) * Q_BLOCK]
        p_hi = psum.astype(BF16)
        p_lo = (psum - p_hi.astype(F32)).astype(BF16)
        imp = _dot(ovt, p_hi) + _dot(ovt, p_lo)
        imp = jnp.where(forced & causal_blk, FORCE_SCORE, imp)
        imps.append(jnp.where(causal_blk, imp, NEG))

    biases = []
    for g in range(B_GROUPS):
        sel = (_block_rank(imps[g], n_blk) < float(min(SEL_TOPK, n_slc))) & causal_blk
        sel = sel & (jrow // (Q_BLOCK // SEL_LEN) != qi)
        biases += [jnp.where(sel, 0.0, NEG).astype(BF16)] * B_HPG
    bias = jnp.concatenate(biases, axis=1)
    rhs_ref[...] = jnp.concatenate(
        [qbd, bias, jnp.zeros((KS_AUG_W - B_KV_W - n_blk, LW), BF16)], axis=0)

    states = [
        strip_update(fresh(), jnp.where(row_q <= tq_q, s_d[:, lanes(j)], NEG), v_block(grp(j), qs, Q_BLOCK))
        for j in range(n_str)]

    order = [(kt, j) for kt in range(n_tiles) for j in range(n_str)]

    def strip_scores(kt, j):
        return _dot(ks_ref[0, kt * KV_TILE:(kt + 1) * KV_TILE, :], rhs_ref[:, lanes(j)])

    pending = [strip_scores(*order[i]) for i in range(min(LOOKAHEAD, len(order)))]
    for i, (kt, j) in enumerate(order):
        s = pending.pop(0)
        if i + LOOKAHEAD < len(order):
            pending.append(strip_scores(*order[i + LOOKAHEAD]))
        states[j] = strip_update(states[j], s, v_block(grp(j), kt * KV_TILE, KV_TILE))

    acc_w = []
    for j in range(n_str):
        swj = sw[:, lanes(j)]
        if steady:
            lo = jnp.where(row_q > tq_q, swj[0:Q_BLOCK], NEG)
            hi = jnp.where(row_q <= tq_q, swj[WINDOW:WIN_KEYS], NEG)
            swj = jnp.concatenate([lo, swj[Q_BLOCK:WINDOW], hi], axis=0)
        else:
            krow = lax.broadcasted_iota(jnp.int32, (WIN_KEYS, STRIP), 0)
            tqw = lax.broadcasted_iota(jnp.int32, (WIN_KEYS, STRIP), 1) & (Q_BLOCK - 1)
            dist = (q0 - ws) + tqw - krow
            swj = jnp.where(lax.bitcast_convert_type(dist, jnp.uint32) < jnp.uint32(WINDOW), swj, NEG)
        acc_w.append(strip_update(fresh(), swj, v_block(B_GROUPS + grp(j), ws, WIN_KEYS))[1])

    per_g = n_str // B_GROUPS
    acc_s = [jnp.concatenate([states[g * per_g + u][1] for u in range(per_g)], axis=1) for g in range(B_GROUPS)]
    acc_w = [jnp.concatenate(acc_w[g * per_g:(g + 1) * per_g], axis=1) for g in range(B_GROUPS)]

    gts = gt_ref[0]
    for g in range(B_GROUPS):
        o_s = acc_s[g][0:B_HD] / acc_s[g][B_HD:B_HD + 1]
        o_w = acc_w[g][0:B_HD] / acc_w[g][B_HD:B_HD + 1]
        for h in range(B_HPG):
            sl = slice(h * Q_BLOCK, (h + 1) * Q_BLOCK)
            r = g * GATE_ROWS + 3 * h
            o_h = gts[r:r + 1] * oc[g][:, sl] + gts[r + 1:r + 2] * o_s[:, sl] + gts[r + 2:r + 3] * o_w[:, sl]
            o_ref[0, (g * B_HPG + h) * B_HD:(g * B_HPG + h + 1) * B_HD, :] = o_h.astype(BF16)


def _nsa(qt, gt, kc, vct, ks, kw, vt, ovt):
    bsz, _, s = qt.shape
    n_cmp_rows = kc.shape[1]
    n_slc = s // SEL_LEN
    assert n_slc <= SEL_ONEHOT_W and s % KV_TILE == 0 and s >= WIN_KEYS
    whole = lambda a: pl.BlockSpec((1,) + a.shape[1:], lambda b, i: (b, 0, 0))
    cend = jnp.asarray((np.arange(n_cmp_rows)[:, None] * CMP_STRIDE + CMP_LEN - 1)
                       - (np.arange(LW)[None, :] % Q_BLOCK), jnp.int32)
    return pl.pallas_call(
        functools.partial(_nsa_kernel, n_cmp_rows=n_cmp_rows, n_slc=n_slc, n_key_tiles=s // KV_TILE),
        grid=(bsz, s // Q_BLOCK),
        in_specs=[
            pl.BlockSpec((1, B_Q_W, Q_BLOCK), lambda b, i: (b, 0, i)),
            pl.BlockSpec((1, B_GROUPS * GATE_ROWS, Q_BLOCK), lambda b, i: (b, 0, i)),
            whole(kc), whole(vct), whole(ks), whole(kw), whole(vt),
            pl.BlockSpec(ovt.shape, lambda b, i: (0, 0)),
            pl.BlockSpec(cend.shape, lambda b, i: (0, 0)),
        ],
        out_specs=pl.BlockSpec((1, B_Q_W, Q_BLOCK), lambda b, i: (b, 0, i)),
        out_shape=jax.ShapeDtypeStruct((bsz, B_Q_W, s), BF16),
        scratch_shapes=[pltpu.VMEM((KS_AUG_W, LW), BF16)],
        compiler_params=pltpu.CompilerParams(
            dimension_semantics=("parallel", "arbitrary"), vmem_limit_bytes=VMEM_LIMIT),
        name="nsa",
    )(qt, gt, kc, vct, ks, kw, vt, ovt, cend)


FF_CHUNK = 1024


def _tail_kernel(x_ref, ya_ref, ybt_ref, mg_ref, wa_ref, wb_ref, wo_ref, g2_ref, w1_ref, w2_ref, gf_ref, o_ref):
    pa = _dot(ya_ref[0], wa_ref[...])
    pb = _dot_tn(ybt_ref[0], wb_ref[...])
    mg = mg_ref[0].astype(F32)
    merged = jax.nn.sigmoid(mg[:, 0:D_MODEL]) * pa + jax.nn.sigmoid(mg[:, D_MODEL:2 * D_MODEL]) * pb
    h = x_ref[0] + _dot(merged.astype(BF16), wo_ref[...])
    hn = _rms(h, g2_ref[...]).astype(BF16)
    acc = jnp.zeros_like(h)
    for c in range(D_FF // FF_CHUNK):
        z = _dot(hn, w1_ref[:, c * FF_CHUNK:(c + 1) * FF_CHUNK])
        acc = acc + _dot(jnp.square(jnp.maximum(z, 0.0)).astype(BF16), w2_ref[c * FF_CHUNK:(c + 1) * FF_CHUNK, :])
    o_ref[0] = _rms(h + acc, gf_ref[...])


def _tail(x, ya, ybt, mg, wa, wb, wo, g2, w1, w2, gf):
    bsz, s, _ = x.shape
    tm = min(512, s)
    tok = lambda w: pl.BlockSpec((1, tm, w), lambda b, t: (b, t, 0))
    const = lambda a: pl.BlockSpec(a.shape, lambda b, t: (0,) * a.ndim, pipeline_mode=pl.Buffered(1))
    return pl.pallas_call(
        _tail_kernel,
        grid=(bsz, s // tm),
        in_specs=[tok(D_MODEL), tok(A_W), pl.BlockSpec((1, B_Q_W, tm), lambda b, t: (b, 0, t)), tok(2 * D_MODEL),
                  const(wa), const(wb), const(wo), const(g2), const(w1), const(w2), const(gf)],
        out_specs=tok(D_MODEL),
        out_shape=jax.ShapeDtypeStruct((bsz, s, D_MODEL), F32),
        compiler_params=pltpu.CompilerParams(
            dimension_semantics=("parallel", "parallel"), vmem_limit_bytes=VMEM_LIMIT),
        name="tail",
    )(x, ya, ybt, mg, wa, wb, wo, g2, w1, w2, gf)


def _pad_groups(a, axis):
    outs = []
    for g in range(B_GROUPS):
        pads = [(0, 0)] * a.ndim
        pads[axis] = (g * B_HD, (B_GROUPS - 1 - g) * B_HD)
        outs.append(jnp.pad(a, pads))
    return jnp.stack(outs)


def _overlap_t(s):
    n_cmp = (s - CMP_LEN) // CMP_STRIDE + 1
    n_slc = s // SEL_LEN
    cs = np.arange(n_cmp)[:, None] * CMP_STRIDE
    ss = np.arange(n_slc)[None, :] * SEL_LEN
    ov = np.clip(np.minimum(cs + CMP_LEN, ss + SEL_LEN) - np.maximum(cs, ss), 0, None) / CMP_LEN
    ovt = np.zeros((n_slc, s // CMP_STRIDE), np.float32)
    ovt[:, :n_cmp] = ov.T
    return jnp.asarray(ovt, BF16)


def kernel(x, positions, norm1_g, w_in, lb_param, hgrn_norm_g, cmp_pe_k, cmp_pe_v, cmp_w1_k, cmp_w2_k,
           cmp_w1_v, cmp_w2_v, w_br_a, w_br_b, w_out, norm2_g, w_ff1, w_ff2, final_g):
    bsz, s, _ = x.shape
    assert norm1_g.shape[0] == 1, "single-layer block"

    w = w_in[0]
    o_bq = 4 * A_W
    o_kv = o_bq + B_Q_W
    o_gate = o_kv + 6 * B_KV_W
    o_mg = o_gate + 3 * B_HEADS
    kv = lambda i: w[:, o_kv + i * B_KV_W:o_kv + (i + 1) * B_KV_W]
    w_nat = jnp.concatenate([w[:, 0:o_bq], w[:, o_mg:o_mg + 2 * D_MODEL], kv(1)], axis=1).astype(BF16)
    wg = w[:, o_gate:o_mg].reshape(D_MODEL, B_GROUPS, 3 * B_HPG)
    wg = jnp.pad(wg, ((0, 0), (0, 0), (0, GATE_ROWS - 3 * B_HPG))).reshape(D_MODEL, B_GROUPS * GATE_ROWS)
    w_t = jnp.concatenate([w[:, o_bq:o_kv], kv(0), kv(2), kv(4), kv(3), kv(5), wg], axis=1).astype(BF16).T
    invf = (ROPE_THETA ** (-jnp.arange(0, ROT_DIM, 2, dtype=F32) / ROT_DIM)).reshape(ROT_HALF, 1)

    hg, mg, vc_tok, kc_tok, ks, kw, qt, vt, gt = _proj(
        x, positions.reshape(bsz, 1, s), norm1_g, invf, w_nat, w_t)

    ya = _hgrn(hg, lb_param, hgrn_norm_g)

    half_w = CMP_LEN // 2 * B_KV_W

    def w1_halves(w1):
        padded = _pad_groups(w1.reshape(CMP_LEN, B_HD, CMP_HIDDEN), 1)
        return padded.reshape(B_GROUPS, 2, half_w, CMP_HIDDEN).astype(BF16)

    pe_halves = lambda pe: jnp.tile(pe, (1, B_GROUPS)).reshape(2, half_w)
    kc, vct = _compress(
        kc_tok, vc_tok, pe_halves(cmp_pe_k[0]), pe_halves(cmp_pe_v[0]), w1_halves(cmp_w1_k[0]),
        w1_halves(cmp_w1_v[0]), _pad_groups(cmp_w2_k[0], 1).astype(BF16), cmp_w2_v[0].T.astype(BF16))

    ybt = _nsa(qt, gt, kc, vct, ks, kw, vt, _overlap_t(s))

    return _tail(x, ya, ybt, mg, w_br_a[0].astype(BF16), w_br_b[0].astype(BF16), w_out[0].astype(BF16),
                 norm2_g, w_ff1[0].astype(BF16), w_ff2[0].astype(BF16), final_g.reshape(1, D_MODEL))
```

```python
import functools

import jax
import jax.numpy as jnp
import numpy as np
from jax import lax
from jax.experimental import pallas as pl
from jax.experimental.pallas import tpu as pltpu

F32 = jnp.float32
BF16 = jnp.bfloat16

D_MODEL = 1024
A_HEADS = 4
A_DK = 128
A_DV = 128
A_CHUNK = 64
A_W = A_HEADS * A_DK
B_HEADS = 8
B_GROUPS = 2
B_HPG = B_HEADS // B_GROUPS
B_HD = 64
B_Q_W = B_HEADS * B_HD
B_KV_W = B_GROUPS * B_HD
CMP_LEN = 32
CMP_STRIDE = 16
CMP_HIDDEN = 256
SEL_LEN = 64
SEL_TOPK = 16
WINDOW = 512
Q_BLOCK = 128
FORCE_SCORE = 1e4
NEG = -1e30
ROPE_THETA = 500000.0
ROT_DIM = B_HD // 4
ROT_HALF = ROT_DIM // 2
D_FF = 4 * D_MODEL
EPS = 1e-6
LOG2_E = 1.4426950408889634
GATE_ROWS = 16
SEL_SHIFT = SEL_LEN.bit_length() - 1
SEL_ONEHOT_W = 128
KS_AUG_W = B_KV_W + SEL_ONEHOT_W
V_AUX = 16
V_BLK = B_HD + V_AUX
VT_ROWS = 2 * B_GROUPS * V_BLK

V7X_VMEM_BYTES = 64 * 1024 * 1024
VMEM_LIMIT = V7X_VMEM_BYTES * 7 // 8

NT = (((1,), (1,)), ((), ()))
TN = (((0,), (0,)), ((), ()))


def _dot(a, b):
    return jnp.dot(a, b, preferred_element_type=F32)


def _dot_nt(a, b):
    return lax.dot_general(a, b, NT, preferred_element_type=F32)


def _dot_tn(a, b):
    return lax.dot_general(a, b, TN, preferred_element_type=F32)


def _rms(x, g):
    return x * lax.rsqrt(jnp.mean(x * x, axis=-1, keepdims=True) + EPS) * g


def _proj_kernel(x_ref, pos_ref, g1_ref, invf_ref, wnat_ref, wt_ref,
                 hg_ref, mg_ref, vc_ref, kc_ref, ks_ref, kw_ref, qt_ref, vt_ref, gt_ref):
    xn = _rms(x_ref[0], g1_ref[...]).astype(BF16)

    def nat(c0, c1):
        return _dot(xn, wnat_ref[:, c0:c1])

    half = 2 * A_W
    for c in range(2):
        hg_ref[0, :, c * half:(c + 1) * half] = nat(c * half, (c + 1) * half).astype(BF16)
        mg_ref[0, :, c * half:(c + 1) * half] = nat(4 * A_W + c * half, 4 * A_W + (c + 1) * half).astype(BF16)
    vc_ref[0] = nat(4 * A_W + 2 * D_MODEL, 4 * A_W + 2 * D_MODEL + B_KV_W)

    def tr(r0, r1):
        return _dot_nt(wt_ref[r0:r1, :], xn)

    ang = invf_ref[...] * pos_ref[0].astype(F32)
    cos = jnp.cos(ang)
    sin = jnp.sin(ang)

    def rope(xt, n_heads):
        pieces = []
        for h in range(n_heads):
            b0 = h * B_HD
            t1 = xt[b0:b0 + ROT_HALF]
            t2 = xt[b0 + ROT_HALF:b0 + ROT_DIM]
            pieces += [t1 * cos - t2 * sin, t2 * cos + t1 * sin, xt[b0 + ROT_DIM:b0 + B_HD]]
        return jnp.concatenate(pieces, axis=0)

    r_q, r_k, r_v = B_Q_W, B_Q_W + 3 * B_KV_W, B_Q_W + 5 * B_KV_W
    qt_ref[0] = (rope(tr(0, r_q), B_HEADS) * (B_HD ** -0.5 * LOG2_E)).astype(BF16)
    k3 = rope(tr(r_q, r_k), 3 * B_GROUPS).T
    kc_ref[0] = k3[:, 0:B_KV_W]
    kw_ref[0] = k3[:, 2 * B_KV_W:3 * B_KV_W].astype(BF16)
    tm = k3.shape[0]
    tok = pl.program_id(1) * tm + lax.broadcasted_iota(jnp.int32, (tm, SEL_ONEHOT_W), 0)
    lane = lax.broadcasted_iota(jnp.int32, (tm, SEL_ONEHOT_W), 1)
    ks_ref[0, :, 0:B_KV_W] = k3[:, B_KV_W:2 * B_KV_W].astype(BF16)
    ks_ref[0, :, B_KV_W:B_KV_W + SEL_ONEHOT_W] = jnp.where((tok >> SEL_SHIFT) == lane, 1.0, 0.0).astype(BF16)
    vt = tr(r_k, r_v)
    aux = jnp.where(lax.broadcasted_iota(jnp.int32, (V_AUX, tm), 0) == 0, 1.0, 0.0)
    vt_ref[0] = jnp.concatenate(
        [piece for j in range(2 * B_GROUPS) for piece in (vt[j * B_HD:(j + 1) * B_HD], aux)], axis=0).astype(BF16)
    gt_ref[0] = jax.nn.sigmoid(tr(r_v, r_v + B_GROUPS * GATE_ROWS))


def _proj(x, pos3, g1, invf, w_nat, w_t):
    bsz, s, _ = x.shape
    tm = min(512, s)
    n_nat = w_nat.shape[1]
    n_t = w_t.shape[0]
    tok = lambda w: pl.BlockSpec((1, tm, w), lambda b, t: (b, t, 0))
    trn = lambda r: pl.BlockSpec((1, r, tm), lambda b, t: (b, 0, t))
    const = lambda shp: pl.BlockSpec(shp, lambda b, t: (0,) * len(shp), pipeline_mode=pl.Buffered(1))
    out_shape = (
        jax.ShapeDtypeStruct((bsz, s, 4 * A_W), BF16),
        jax.ShapeDtypeStruct((bsz, s, 2 * D_MODEL), BF16),
        jax.ShapeDtypeStruct((bsz, s, B_KV_W), F32),
        jax.ShapeDtypeStruct((bsz, s, B_KV_W), F32),
        jax.ShapeDtypeStruct((bsz, s, KS_AUG_W), BF16),
        jax.ShapeDtypeStruct((bsz, s, B_KV_W), BF16),
        jax.ShapeDtypeStruct((bsz, B_Q_W, s), BF16),
        jax.ShapeDtypeStruct((bsz, VT_ROWS, s), BF16),
        jax.ShapeDtypeStruct((bsz, B_GROUPS * GATE_ROWS, s), F32),
    )
    return pl.pallas_call(
        _proj_kernel,
        grid=(bsz, s // tm),
        in_specs=[tok(D_MODEL), trn(1), const((1, D_MODEL)), const((ROT_HALF, 1)),
                  const((D_MODEL, n_nat)), const((n_t, D_MODEL))],
        out_specs=(tok(4 * A_W), tok(2 * D_MODEL), tok(B_KV_W), tok(B_KV_W), tok(KS_AUG_W), tok(B_KV_W),
                   trn(B_Q_W), trn(VT_ROWS), trn(B_GROUPS * GATE_ROWS)),
        out_shape=out_shape,
        compiler_params=pltpu.CompilerParams(
            dimension_semantics=("parallel", "parallel"), vmem_limit_bytes=VMEM_LIMIT),
        name="proj",
    )(x, pos3, g1, invf, w_nat, w_t)


def _hgrn_kernel(hg_ref, lbp_ref, ng_ref, o_ref, st_ref, *, n_chunks):
    @pl.when(pl.program_id(1) == 0)
    def _():
        st_ref[...] = jnp.zeros_like(st_ref)

    lbp = lbp_ref[...]
    e = jnp.exp(lbp - jnp.max(lbp, axis=0, keepdims=True))
    lb = e[0:1] / jnp.sum(e, axis=0, keepdims=True)
    ng = ng_ref[...]
    row = lax.broadcasted_iota(jnp.int32, (A_CHUNK, A_CHUNK), 0)
    col = lax.broadcasted_iota(jnp.int32, (A_CHUNK, A_CHUNK), 1)
    causal = row >= col
    tri = jnp.where(causal, 1.0, 0.0).astype(BF16)

    heads = [slice(h * A_DK, (h + 1) * A_DK) for h in range(A_HEADS)]

    def stage1(c):
        blk = hg_ref[0, c * A_CHUNK:(c + 1) * A_CHUNK, :]
        f = lb + (1.0 - lb) * jax.nn.sigmoid(blk[:, A_W:2 * A_W].astype(F32))
        lf = jnp.log(f)
        lf_hi = lf.astype(BF16)
        r1 = lf - lf_hi.astype(F32)
        lf_mid = r1.astype(BF16)
        lf_lo = (r1 - lf_mid.astype(F32)).astype(BF16)
        b = _dot(tri, lf_hi) + _dot(tri, lf_mid) + _dot(tri, lf_lo)
        return dict(c=c, k=1.0 - f, b=b)

    def stage2(s):
        c, k, b = s["c"], s["k"], s["b"]
        blk = hg_ref[0, c * A_CHUNK:(c + 1) * A_CHUNK, :]
        dec = jnp.exp(b[A_CHUNK - 1:A_CHUNK, :])
        k_dec = k * jnp.exp(-b)
        return dict(c=c, v=blk[:, 2 * A_W:3 * A_W],
                    q_dec=(blk[:, 0:A_W].astype(F32) * jnp.exp(b)).astype(BF16),
                    k_dec=k_dec.astype(BF16), k_til=(k_dec * dec).astype(BF16), dec=dec)

    def stage3(s):
        s = dict(s)
        attn = [jnp.where(causal, _dot_nt(s["q_dec"][:, sl], s["k_dec"][:, sl]), 0.0).astype(BF16)
                for sl in heads]
        s["o_intra"] = [_dot(attn[h], s["v"][:, sl]) for h, sl in enumerate(heads)]
        s["kv_t"] = [_dot_tn(s["v"][:, sl], s["k_til"][:, sl]) for sl in heads]
        return s

    def stage4(s, state):
        c = s["c"]
        g = hg_ref[0, c * A_CHUNK:(c + 1) * A_CHUNK, 3 * A_W:4 * A_W].astype(F32)
        new_state = []
        for h, sl in enumerate(heads):
            o = s["o_intra"][h] + _dot_nt(s["q_dec"][:, sl], state[h].astype(BF16))
            new_state.append(state[h] * s["dec"][:, sl] + s["kv_t"][h])
            gh = g[:, sl]
            y = _rms(o, ng) * (gh * jax.nn.sigmoid(gh))
            o_ref[0, c * A_CHUNK:(c + 1) * A_CHUNK, sl] = y.astype(BF16)
        return new_state

    state = [st_ref[h] for h in range(A_HEADS)]
    pipe = [None, None, None]
    for t in range(n_chunks + 3):
        if pipe[2] is not None:
            state = stage4(pipe[2], state)
        pipe[2] = stage3(pipe[1]) if pipe[1] is not None else None
        pipe[1] = stage2(pipe[0]) if pipe[0] is not None else None
        pipe[0] = stage1(t) if t < n_chunks else None
    for h in range(A_HEADS):
        st_ref[h] = state[h]


def _hgrn(hg, lb_param, norm_g):
    bsz, s, _ = hg.shape
    tt = min(2048, s)
    return pl.pallas_call(
        functools.partial(_hgrn_kernel, n_chunks=tt // A_CHUNK),
        grid=(bsz, s // tt),
        in_specs=[pl.BlockSpec((1, tt, 4 * A_W), lambda b, t: (b, t, 0)),
                  pl.BlockSpec(lb_param.shape, lambda b, t: (0, 0)),
                  pl.BlockSpec((1, A_DV), lambda b, t: (0, 0))],
        out_specs=pl.BlockSpec((1, tt, A_W), lambda b, t: (b, t, 0)),
        out_shape=jax.ShapeDtypeStruct((bsz, s, A_W), BF16),
        scratch_shapes=[pltpu.VMEM((A_HEADS, A_DV, A_DK), F32)],
        compiler_params=pltpu.CompilerParams(
            dimension_semantics=("parallel", "arbitrary"), vmem_limit_bytes=VMEM_LIMIT),
        name="hgrn",
    )(hg, lb_param, norm_g)


def _compress_kernel(kt_ref, vt_ref, pek_ref, pev_ref, w1k_ref, w1v_ref, w2k_ref, w2vt_ref,
                     kc_ref, vct_ref, *, n_blk):
    def hidden(t_ref, pe_ref, w1_ref):
        rows = jnp.concatenate(
            [t_ref[0, pl.ds(l, n_blk, stride=CMP_STRIDE), :] for l in range(CMP_STRIDE)], axis=1)
        halves = [(rows + pe_ref[a:a + 1, :]).astype(BF16) for a in range(2)]
        out = []
        for g in range(B_GROUPS):
            pre = _dot(halves[0], w1_ref[g, 0]) + pltpu.roll(_dot(halves[1], w1_ref[g, 1]), n_blk - 1, 0)
            out.append(jax.nn.gelu(pre).astype(BF16))
        return out

    valid_r = lax.broadcasted_iota(jnp.int32, (n_blk, B_KV_W), 0) < n_blk - 1
    valid_c = lax.broadcasted_iota(jnp.int32, (B_HD, n_blk), 1) < n_blk - 1
    hk = hidden(kt_ref, pek_ref, w1k_ref)
    hv = hidden(vt_ref, pev_ref, w1v_ref)
    kc = jnp.zeros((n_blk, B_KV_W), F32)
    for g in range(B_GROUPS):
        kc = kc + _dot(hk[g], w2k_ref[g])
        vct = _dot_nt(w2vt_ref[...], hv[g])
        vct_ref[0, g * B_HD:(g + 1) * B_HD, :] = jnp.where(valid_c, vct, 0.0).astype(BF16)
    kc_ref[0] = jnp.where(valid_r, kc, 0.0).astype(BF16)


def _compress(kc_tok, vc_tok, pek, pev, w1k, w1v, w2k, w2vt):
    bsz, s, _ = kc_tok.shape
    n_blk = s // CMP_STRIDE
    full = lambda a: pl.BlockSpec(a.shape, lambda b: (0,) * a.ndim)
    tokspec = pl.BlockSpec((1, s, B_KV_W), lambda b: (b, 0, 0))
    return pl.pallas_call(
        functools.partial(_compress_kernel, n_blk=n_blk),
        grid=(bsz,),
        in_specs=[tokspec, tokspec, full(pek), full(pev), full(w1k), full(w1v), full(w2k), full(w2vt)],
        out_specs=(pl.BlockSpec((1, n_blk, B_KV_W), lambda b: (b, 0, 0)),
                   pl.BlockSpec((1, B_KV_W, n_blk), lambda b: (b, 0, 0))),
        out_shape=(jax.ShapeDtypeStruct((bsz, n_blk, B_KV_W), BF16),
                   jax.ShapeDtypeStruct((bsz, B_KV_W, n_blk), BF16)),
        compiler_params=pltpu.CompilerParams(
            dimension_semantics=("parallel",), vmem_limit_bytes=VMEM_LIMIT),
        name="compress",
    )(kc_tok, vc_tok, pek, pev, w1k, w1v, w2k, w2vt)


KV_TILE = 512
WIN_KEYS = WINDOW + Q_BLOCK
QW = B_HPG * Q_BLOCK
LW = B_GROUPS * QW
STRIP = 256
TILE_CLASS = 1
TILES_PER_STEP = 4
LOOKAHEAD = 4


def _block_rank(imp, n_slc):
    sub = lax.broadcasted_iota(jnp.int32, (8, Q_BLOCK), 0)
    rank = jnp.zeros((n_slc, Q_BLOCK), F32)
    for i in range(n_slc):
        ri = imp[i:i + 1, :]
        parts = []
        for r in range(n_slc // 8):
            blk = imp[8 * r:8 * r + 8, :]
            gt = jnp.where(ri > blk, 1.0, 0.0)
            ge = jnp.where(ri >= blk, 1.0, 0.0)
            if 8 * r + 7 <= i:
                parts.append(gt)
            elif 8 * r > i:
                parts.append(ge)
            else:
                parts.append(jnp.where(sub > i - 8 * r, ge, gt))
        rank = rank + jnp.concatenate(parts, axis=0)
    return rank


def _nsa_kernel(*refs, n_cmp_rows, n_slc, n_key_tiles):
    n_class = (n_key_tiles + TILE_CLASS - 1) // TILE_CLASS

    def one_tile(u, carry):
        qi = pl.program_id(1) * TILES_PER_STEP + u
        n_tiles = (qi * Q_BLOCK + KV_TILE - 1) // KV_TILE
        for cb in range(n_class + 1):
            n_max = min(cb * TILE_CLASS, n_key_tiles)
            n_min = max((cb - 1) * TILE_CLASS + 1, 0) if cb else 0
            q0_min = KV_TILE * (n_min - 1) + Q_BLOCK if n_min else 0
            pl.when((n_tiles + TILE_CLASS - 1) // TILE_CLASS == cb)(functools.partial(
                _nsa_step, n_max, q0_min, qi, u, *refs, n_cmp_rows=n_cmp_rows, n_slc=n_slc))
        return carry

    lax.fori_loop(0, TILES_PER_STEP, one_tile, 0)


def _nsa_step(n_tiles, q0_min, qi, u, qt_ref, gt_ref, kc_ref, vct_ref, ks_ref, kw_ref, vt_ref, ovt_ref, cend_ref,
              o_ref, rhs_ref, *, n_cmp_rows, n_slc):
    q0 = qi * Q_BLOCK
    tile_lanes = pl.ds(pl.multiple_of(u * Q_BLOCK, Q_BLOCK), Q_BLOCK)
    n_blk = min(n_slc, 16 * ((KV_TILE // SEL_LEN * n_tiles + Q_BLOCK // SEL_LEN + 15) // 16))
    n_cmp = min(n_cmp_rows, 128 * (((KV_TILE * n_tiles + Q_BLOCK) // CMP_STRIDE + 127) // 128))
    steady = q0_min >= WINDOW

    qblk = qt_ref[0, :, tile_lanes]
    zq = jnp.zeros((B_HD, QW), BF16)
    qrows = []
    for g in range(B_GROUPS):
        qg = jnp.concatenate(
            [qblk[(g * B_HPG + h) * B_HD:(g * B_HPG + h + 1) * B_HD, :] for h in range(B_HPG)], axis=1)
        qrows.append(jnp.concatenate([qg if gg == g else zq for gg in range(B_GROUPS)], axis=1))
    qbd = jnp.concatenate(qrows, axis=0)

    qs = pl.multiple_of(q0, Q_BLOCK)
    ws = pl.multiple_of(jnp.maximum(q0 - WINDOW, 0), Q_BLOCK)
    sc = _dot(kc_ref[0, 0:n_cmp, :], qbd)
    sw = _dot(kw_ref[0, pl.ds(ws, WIN_KEYS), :], qbd)
    s_d = _dot(ks_ref[0, pl.ds(qs, Q_BLOCK), 0:B_KV_W], qbd)

    okc = cend_ref[0:n_cmp, :] <= q0
    sc = jnp.where(okc, sc, NEG)
    pc = jnp.exp2(sc - jnp.max(sc, axis=0, keepdims=True))
    l_c = jnp.sum(pc, axis=0, keepdims=True)
    if n_tiles == 0:
        pc = jnp.where(okc, pc, 0.0)
        l_c = jnp.maximum(jnp.sum(pc, axis=0, keepdims=True), 1e-30)
    pc = pc / l_c
    pcb = pc.astype(BF16)
    oc = [_dot(vct_ref[0, g * B_HD:(g + 1) * B_HD, 0:n_cmp], pcb[:, g * QW:(g + 1) * QW])
          for g in range(B_GROUPS)]

    def strip_update(state, s, v):
        m, acc = state
        m_new = jnp.maximum(m, jnp.max(s, axis=0, keepdims=True))
        pb = jnp.exp2(s - m_new).astype(BF16)
        return m_new, jnp.exp2(m - m_new) * acc + _dot(v, pb)

    def v_block(j, start, size):
        return vt_ref[0, j * V_BLK:(j + 1) * V_BLK, pl.ds(start, size)]

    n_str = LW // STRIP
    grp = lambda j: j * STRIP // QW
    lanes = lambda j: slice(j * STRIP, (j + 1) * STRIP)
    fresh = lambda: (jnp.full((1, STRIP), NEG, F32), jnp.zeros((V_BLK, STRIP), F32))
    row_q = lax.broadcasted_iota(jnp.int32, (Q_BLOCK, STRIP), 0)
    tq_q = lax.broadcasted_iota(jnp.int32, (Q_BLOCK, STRIP), 1) & (Q_BLOCK - 1)

    jrow = lax.broadcasted_iota(jnp.int32, (n_blk, Q_BLOCK), 0)
    cur = (q0 + lax.broadcasted_iota(jnp.int32, (n_blk, Q_BLOCK), 1)) >> SEL_SHIFT
    causal_blk = jrow <= cur
    forced = (jrow == 0) | (jrow == cur) | (jrow == cur - 1)
    ovt = ovt_ref[0:n_blk, 0:n_cmp]
    imps = []
    for g in range(B_GROUPS):
        psum = pc[:, g * QW:g * QW + Q_BLOCK]
        for h in range(1, B_HPG):
            psum = psum + pc[:, g * QW + h * Q_BLOCK:g * QW + (h + 1) * Q_BLOCK]
        p_hi = psum.astype(BF16)
        p_lo = (psum - p_hi.astype(F32)).astype(BF16)
        imp = _dot(ovt, p_hi) + _dot(ovt, p_lo)
        imp = jnp.where(forced & causal_blk, FORCE_SCORE, imp)
        imps.append(jnp.where(causal_blk, imp, NEG))

    biases = []
    for g in range(B_GROUPS):
        sel = (_block_rank(imps[g], n_blk) < float(min(SEL_TOPK, n_slc))) & causal_blk
        sel = sel & (jrow // (Q_BLOCK // SEL_LEN) != qi)
        biases += [jnp.where(sel, 0.0, NEG).astype(BF16)] * B_HPG
    bias = jnp.concatenate(biases, axis=1)
    rhs_ref[...] = jnp.concatenate(
        [qbd, bias, jnp.zeros((KS_AUG_W - B_KV_W - n_blk, LW), BF16)], axis=0)

    states = [
        strip_update(fresh(), jnp.where(row_q <= tq_q, s_d[:, lanes(j)], NEG), v_block(grp(j), qs, Q_BLOCK))
        for j in range(n_str)]

    order = [(kt, j) for kt in range(n_tiles) for j in range(n_str)]

    def strip_scores(kt, j):
        return _dot(ks_ref[0, kt * KV_TILE:(kt + 1) * KV_TILE, :], rhs_ref[:, lanes(j)])

    pending = [strip_scores(*order[i]) for i in range(min(LOOKAHEAD, len(order)))]
    for i, (kt, j) in enumerate(order):
        s = pending.pop(0)
        if i + LOOKAHEAD < len(order):
            pending.append(strip_scores(*order[i + LOOKAHEAD]))
        states[j] = strip_update(states[j], s, v_block(grp(j), kt * KV_TILE, KV_TILE))

    acc_w = []
    for j in range(n_str):
        swj = sw[:, lanes(j)]
        if steady:
            lo = jnp.where(row_q > tq_q, swj[0:Q_BLOCK], NEG)
            hi = jnp.where(row_q <= tq_q, swj[WINDOW:WIN_KEYS], NEG)
            swj = jnp.concatenate([lo, swj[Q_BLOCK:WINDOW], hi], axis=0)
        else:
            krow = lax.broadcasted_iota(jnp.int32, (WIN_KEYS, STRIP), 0)
            tqw = lax.broadcasted_iota(jnp.int32, (WIN_KEYS, STRIP), 1) & (Q_BLOCK - 1)
            dist = (q0 - ws) + tqw - krow
            swj = jnp.where(lax.bitcast_convert_type(dist, jnp.uint32) < jnp.uint32(WINDOW), swj, NEG)
        acc_w.append(strip_update(fresh(), swj, v_block(B_GROUPS + grp(j), ws, WIN_KEYS))[1])

    per_g = n_str // B_GROUPS
    acc_s = [jnp.concatenate([states[g * per_g + u][1] for u in range(per_g)], axis=1) for g in range(B_GROUPS)]
    acc_w = [jnp.concatenate(acc_w[g * per_g:(g + 1) * per_g], axis=1) for g in range(B_GROUPS)]

    gts = gt_ref[0, :, tile_lanes]
    for g in range(B_GROUPS):
        o_s = acc_s[g][0:B_HD] / acc_s[g][B_HD:B_HD + 1]
        o_w = acc_w[g][0:B_HD] / acc_w[g][B_HD:B_HD + 1]
        for h in range(B_HPG):
            sl = slice(h * Q_BLOCK, (h + 1) * Q_BLOCK)
            r = g * GATE_ROWS + 3 * h
            o_h = gts[r:r + 1] * oc[g][:, sl] + gts[r + 1:r + 2] * o_s[:, sl] + gts[r + 2:r + 3] * o_w[:, sl]
            o_ref[0, (g * B_HPG + h) * B_HD:(g * B_HPG + h + 1) * B_HD, tile_lanes] = o_h.astype(BF16)


def _nsa(qt, gt, kc, vct, ks, kw, vt, ovt):
    bsz, _, s = qt.shape
    n_cmp_rows = kc.shape[1]
    n_slc = s // SEL_LEN
    step_w = TILES_PER_STEP * Q_BLOCK
    assert n_slc <= SEL_ONEHOT_W and s % KV_TILE == 0 and s >= WIN_KEYS and s % step_w == 0
    whole = lambda a: pl.BlockSpec((1,) + a.shape[1:], lambda b, i: (b, 0, 0))
    cend = jnp.asarray((np.arange(n_cmp_rows)[:, None] * CMP_STRIDE + CMP_LEN - 1)
                       - (np.arange(LW)[None, :] % Q_BLOCK), jnp.int32)
    return pl.pallas_call(
        functools.partial(_nsa_kernel, n_cmp_rows=n_cmp_rows, n_slc=n_slc, n_key_tiles=s // KV_TILE),
        grid=(bsz, s // step_w),
        in_specs=[
            pl.BlockSpec((1, B_Q_W, step_w), lambda b, i: (b, 0, i)),
            pl.BlockSpec((1, B_GROUPS * GATE_ROWS, step_w), lambda b, i: (b, 0, i)),
            whole(kc), whole(vct), whole(ks), whole(kw), whole(vt),
            pl.BlockSpec(ovt.shape, lambda b, i: (0, 0)),
            pl.BlockSpec(cend.shape, lambda b, i: (0, 0)),
        ],
        out_specs=pl.BlockSpec((1, B_Q_W, step_w), lambda b, i: (b, 0, i)),
        out_shape=jax.ShapeDtypeStruct((bsz, B_Q_W, s), BF16),
        scratch_shapes=[pltpu.VMEM((KS_AUG_W, LW), BF16)],
        compiler_params=pltpu.CompilerParams(
            dimension_semantics=("parallel", "arbitrary"), vmem_limit_bytes=VMEM_LIMIT),
        name="nsa",
    )(qt, gt, kc, vct, ks, kw, vt, ovt, cend)


FF_CHUNK = 1024


def _tail_kernel(x_ref, ya_ref, ybt_ref, mg_ref, wa_ref, wb_ref, wo_ref, g2_ref, w1_ref, w2_ref, gf_ref, o_ref):
    pa = _dot(ya_ref[0], wa_ref[...])
    pb = _dot_tn(ybt_ref[0], wb_ref[...])
    mg = mg_ref[0].astype(F32)
    merged = jax.nn.sigmoid(mg[:, 0:D_MODEL]) * pa + jax.nn.sigmoid(mg[:, D_MODEL:2 * D_MODEL]) * pb
    h = x_ref[0] + _dot(merged.astype(BF16), wo_ref[...])
    hn = _rms(h, g2_ref[...]).astype(BF16)
    acc = jnp.zeros_like(h)
    for c in range(D_FF // FF_CHUNK):
        z = _dot(hn, w1_ref[:, c * FF_CHUNK:(c + 1) * FF_CHUNK])
        acc = acc + _dot(jnp.square(jnp.maximum(z, 0.0)).astype(BF16), w2_ref[c * FF_CHUNK:(c + 1) * FF_CHUNK, :])
    o_ref[0] = _rms(h + acc, gf_ref[...])


def _tail(x, ya, ybt, mg, wa, wb, wo, g2, w1, w2, gf):
    bsz, s, _ = x.shape
    tm = min(512, s)
    tok = lambda w: pl.BlockSpec((1, tm, w), lambda b, t: (b, t, 0))
    const = lambda a: pl.BlockSpec(a.shape, lambda b, t: (0,) * a.ndim, pipeline_mode=pl.Buffered(1))
    return pl.pallas_call(
        _tail_kernel,
        grid=(bsz, s // tm),
        in_specs=[tok(D_MODEL), tok(A_W), pl.BlockSpec((1, B_Q_W, tm), lambda b, t: (b, 0, t)), tok(2 * D_MODEL),
                  const(wa), const(wb), const(wo), const(g2), const(w1), const(w2), const(gf)],
        out_specs=tok(D_MODEL),
        out_shape=jax.ShapeDtypeStruct((bsz, s, D_MODEL), F32),
        compiler_params=pltpu.CompilerParams(
            dimension_semantics=("parallel", "parallel"), vmem_limit_bytes=VMEM_LIMIT),
        name="tail",
    )(x, ya, ybt, mg, wa, wb, wo, g2, w1, w2, gf)


def _pad_groups(a, axis):
    outs = []
    for g in range(B_GROUPS):
        pads = [(0, 0)] * a.ndim
        pads[axis] = (g * B_HD, (B_GROUPS - 1 - g) * B_HD)
        outs.append(jnp.pad(a, pads))
    return jnp.stack(outs)


def _overlap_t(s):
    n_cmp = (s - CMP_LEN) // CMP_STRIDE + 1
    n_slc = s // SEL_LEN
    cs = np.arange(n_cmp)[:, None] * CMP_STRIDE
    ss = np.arange(n_slc)[None, :] * SEL_LEN
    ov = np.clip(np.minimum(cs + CMP_LEN, ss + SEL_LEN) - np.maximum(cs, ss), 0, None) / CMP_LEN
    ovt = np.zeros((n_slc, s // CMP_STRIDE), np.float32)
    ovt[:, :n_cmp] = ov.T
    return jnp.asarray(ovt, BF16)


def kernel(x, positions, norm1_g, w_in, lb_param, hgrn_norm_g, cmp_pe_k, cmp_pe_v, cmp_w1_k, cmp_w2_k,
           cmp_w1_v, cmp_w2_v, w_br_a, w_br_b, w_out, norm2_g, w_ff1, w_ff2, final_g):
    bsz, s, _ = x.shape
    assert norm1_g.shape[0] == 1, "single-layer block"

    w = w_in[0]
    o_bq = 4 * A_W
    o_kv = o_bq + B_Q_W
    o_gate = o_kv + 6 * B_KV_W
    o_mg = o_gate + 3 * B_HEADS
    kv = lambda i: w[:, o_kv + i * B_KV_W:o_kv + (i + 1) * B_KV_W]
    w_nat = jnp.concatenate([w[:, 0:o_bq], w[:, o_mg:o_mg + 2 * D_MODEL], kv(1)], axis=1).astype(BF16)
    wg = w[:, o_gate:o_mg].reshape(D_MODEL, B_GROUPS, 3 * B_HPG)
    wg = jnp.pad(wg, ((0, 0), (0, 0), (0, GATE_ROWS - 3 * B_HPG))).reshape(D_MODEL, B_GROUPS * GATE_ROWS)
    w_t = jnp.concatenate([w[:, o_bq:o_kv], kv(0), kv(2), kv(4), kv(3), kv(5), wg], axis=1).astype(BF16).T
    invf = (ROPE_THETA ** (-jnp.arange(0, ROT_DIM, 2, dtype=F32) / ROT_DIM)).reshape(ROT_HALF, 1)

    hg, mg, vc_tok, kc_tok, ks, kw, qt, vt, gt = _proj(
        x, positions.reshape(bsz, 1, s), norm1_g, invf, w_nat, w_t)

    ya = _hgrn(hg, lb_param, hgrn_norm_g)

    half_w = CMP_LEN // 2 * B_KV_W

    def w1_halves(w1):
        padded = _pad_groups(w1.reshape(CMP_LEN, B_HD, CMP_HIDDEN), 1)
        return padded.reshape(B_GROUPS, 2, half_w, CMP_HIDDEN).astype(BF16)

    pe_halves = lambda pe: jnp.tile(pe, (1, B_GROUPS)).reshape(2, half_w)
    kc, vct = _compress(
        kc_tok, vc_tok, pe_halves(cmp_pe_k[0]), pe_halves(cmp_pe_v[0]), w1_halves(cmp_w1_k[0]),
        w1_halves(cmp_w1_v[0]), _pad_groups(cmp_w2_k[0], 1).astype(BF16), cmp_w2_v[0].T.astype(BF16))

    ybt = _nsa(qt, gt, kc, vct, ks, kw, vt, _overlap_t(s))

    return _tail(x, ya, ybt, mg, w_br_a[0].astype(BF16), w_br_b[0].astype(BF16), w_out[0].astype(BF16),
                 norm2_g, w_ff1[0].astype(BF16), w_ff2[0].astype(BF16), final_g.reshape(1, D_MODEL))
```
